```python
import math
import jax, jax.numpy as jnp
from jax import lax
import numpy as np

D_MODEL = 2048
BATCH = 8
SEQ = 2048
DEPTH = 1
DEC_BATCH = 128
DEC_SEQ = 1
PAST_LEN = 2048
PAGE_SIZE = 128

MIX_WIDTH = D_MODEL
CHUNK = 128
A_WIDTH = MIX_WIDTH // 2
A_GROUPS = 8
A_CH = A_WIDTH // A_GROUPS
ATT_WIDTH = MIX_WIDTH - A_WIDTH
HEAD_DIM = 128
N_HEADS = ATT_WIDTH // HEAD_DIM
N_KV_HEADS = 2
GQA_GROUP = N_HEADS // N_KV_HEADS
KV_WIDTH = N_KV_HEADS * HEAD_DIM
IDX_HEADS = 16
IDX_DIM = 64
TOPK_MAX = 256
Q_BLOCK = 128
D_FF = 4 * D_MODEL
N_MOD = 6
DEEPNORM_ALPHA = (2 * DEPTH) ** 0.25
DEEPNORM_BETA = (8 * DEPTH) ** -0.25
LN_EPS = 1e-5
ATT_SCALE = HEAD_DIM ** -0.5
SPLIT_SIZES = (A_WIDTH, A_WIDTH, ATT_WIDTH, KV_WIDTH, KV_WIDTH, IDX_HEADS * IDX_DIM, IDX_DIM, IDX_HEADS)
IN_WIDTH = 2 * A_WIDTH + ATT_WIDTH + 2 * KV_WIDTH + IDX_HEADS * IDX_DIM + IDX_DIM + IDX_HEADS

kernel_name = "hybrid_gmlp_dsa_decoder_step"


def layer_norm(x, g, b):
    xf = x.astype(jnp.float32)
    mu = jnp.mean(xf, axis=-1, keepdims=True)
    var = jnp.mean(jnp.square(xf - mu), axis=-1, keepdims=True)
    y = (xf - mu) * lax.rsqrt(var + LN_EPS)
    return (y * g.astype(jnp.float32) + b.astype(jnp.float32)).astype(x.dtype)


def alibi_slopes():
    s = 2.0 ** (-8.0 * np.arange(1, N_HEADS + 1, dtype=np.float32) / N_HEADS)
    return jnp.asarray(s.reshape(N_KV_HEADS, GQA_GROUP), jnp.float32)


def cond_params(c, w_cond, b_cond):
    z = jnp.einsum('bd,de->be', jax.nn.silu(c), w_cond) + b_cond
    return jnp.split(z, N_MOD, axis=-1)


def modulate(x, shift, scale):
    return x * (1.0 + scale[:, None, :]) + shift[:, None, :]


def post_block(x, sub, gate, g, b):
    return layer_norm(DEEPNORM_ALPHA * x + (1.0 + gate[:, None, :]) * sub, g, b)


def sq_relu_ffn(h, w1, b1, w2, b2):
    a = jax.nn.relu(jnp.einsum('btd,df->btf', h, w1) + b1)
    return jnp.einsum('btf,fd->btd', a * a, w2) + b2


def project(h, w_in):
    B, T = h.shape[0], h.shape[1]
    z = jnp.einsum('btd,de->bte', h, w_in)
    points = np.cumsum(np.array(SPLIT_SIZES))[:-1].tolist()
    u, v, q, k, vv, qi, ki, wi = jnp.split(z, points, axis=-1)
    return (u.reshape(B, T, A_GROUPS, A_CH),
            v.reshape(B, T, A_GROUPS, A_CH),
            q.reshape(B, T, N_KV_HEADS, GQA_GROUP, HEAD_DIM),
            k.reshape(B, T, N_KV_HEADS, HEAD_DIM),
            vv.reshape(B, T, N_KV_HEADS, HEAD_DIM),
            qi.reshape(B, T, IDX_HEADS, IDX_DIM),
            ki,
            wi * (IDX_HEADS ** -0.5))


def chunk_gate(u, v, ln_g, ln_b, w_s, b_s):
    B, T = u.shape[0], u.shape[1]
    n = min(T, CHUNK)
    vn = layer_norm(v, ln_g, ln_b)
    mask = jnp.tril(jnp.ones((n, n), dtype=bool))
    ws = jnp.where(mask[None], w_s[:, :n, :n], 0.0)
    vc = vn.reshape(B, T // n, n, A_GROUPS, A_CH)
    sp = jnp.einsum('gts,bnsgc->bntgc', ws, vc) + jnp.transpose(b_s[:, :n])[None, None, :, :, None]
    out = u * sp.reshape(B, T, A_GROUPS, A_CH)
    return out.reshape(B, T, A_WIDTH), vn


def indexer_scores(qi, wi, ki, t_pos, s_pos):
    s = jnp.einsum('bqhd,bsd->bqhs', qi, ki) * (IDX_DIM ** -0.5)
    score = jnp.einsum('bqhs,bqh->bqs', jax.nn.relu(s), wi).astype(jnp.float32)
    admissible = s_pos[None, None, :] <= t_pos[None, :, None]
    return jnp.where(admissible, score, -jnp.inf)


def gather_attend(q, k_sel, v_sel, sel, t_pos, slopes):
    logits = jnp.einsum('bqjgd,bqnjd->bqjgn', q, k_sel).astype(jnp.float32) * ATT_SCALE
    dist = (t_pos[None, :, None] - sel).astype(jnp.float32)
    logits = logits - slopes[None, None, :, :, None] * dist[:, :, None, None, :]
    logits = jnp.where((dist >= 0)[:, :, None, None, :], logits, -jnp.inf)
    p = jax.nn.softmax(logits, axis=-1).astype(v_sel.dtype)
    return jnp.einsum('bqjgn,bqnjd->bqjgd', p, v_sel)


def prompt_sparse_attention(q, k, v, qi, wi, ki, slopes):
    B, T = q.shape[0], q.shape[1]
    n_sel = min(TOPK_MAX, T // 4)
    nblk = T // Q_BLOCK
    s_pos = jnp.arange(T)
    bidx = jnp.arange(B)[:, None, None]

    def block(args):
        qb, qib, wib, tb = args
        scores = indexer_scores(qib, wib, ki, tb, s_pos)
        _, sel = lax.top_k(scores, n_sel)
        return gather_attend(qb, k[bidx, sel], v[bidx, sel], sel, tb, slopes)

    def to_blocks(a):
        return jnp.moveaxis(a.reshape((B, nblk, Q_BLOCK) + a.shape[2:]), 1, 0)

    o = lax.map(block, (to_blocks(q), to_blocks(qi), to_blocks(wi), s_pos.reshape(nblk, Q_BLOCK)))
    return jnp.moveaxis(o, 0, 1).reshape(B, T, ATT_WIDTH)


def sample_sparse_attention(q, k_new, v_new, qi, wi, ki_new, cache_k, cache_v, cache_kidx, page_table, slopes):
    Bd, Ts = q.shape[0], q.shape[1]
    L = PAST_LEN + Ts
    n_sel = min(TOPK_MAX, L // 4)
    ki_past = cache_kidx[page_table].reshape(Bd, PAST_LEN, IDX_DIM)
    ki_all = jnp.concatenate([ki_past, ki_new.astype(ki_past.dtype)], axis=1)
    t_pos = PAST_LEN + jnp.arange(Ts)
    scores = indexer_scores(qi, wi, ki_all, t_pos, jnp.arange(L))
    _, sel = lax.top_k(scores, n_sel)
    bidx = jnp.arange(Bd)[:, None, None]
    in_past = sel < PAST_LEN
    ps = jnp.minimum(sel, PAST_LEN - 1)
    phys = page_table[bidx, ps // PAGE_SIZE]
    off = ps % PAGE_SIZE
    new_i = jnp.clip(sel - PAST_LEN, 0, Ts - 1)

    def pick(cache, new):
        return jnp.where(in_past[..., None, None], cache[phys, off], new[bidx, new_i].astype(cache.dtype))

    o = gather_attend(q, pick(cache_k, k_new), pick(cache_v, v_new), sel, t_pos, slopes)
    return o.reshape(Bd, Ts, ATT_WIDTH)


def setup_inputs(seed: int = 0) -> dict:
    key = jax.random.key(seed)
    ks = jax.random.split(key, 32)

    def nrm(k, shape, scale):
        return jax.random.normal(k, shape, jnp.float32) * scale

    n_pages = PAST_LEN // PAGE_SIZE
    n_used = DEC_BATCH * n_pages
    n_pool = n_used + max(1, n_used // 4)
    page_table = jax.random.permutation(ks[0], n_pool)[:n_used].reshape(DEC_BATCH, n_pages).astype(jnp.int32)
    return {
        "x_prompt": nrm(ks[1], (BATCH, SEQ, D_MODEL), 1.0),
        "x_sample": nrm(ks[2], (DEC_BATCH, DEC_SEQ, D_MODEL), 1.0),
        "cache_k": nrm(ks[3], (DEPTH, n_pool, PAGE_SIZE, N_KV_HEADS, HEAD_DIM), 1.0),
        "cache_v": nrm(ks[4], (DEPTH, n_pool, PAGE_SIZE, N_KV_HEADS, HEAD_DIM), 1.0),
        "cache_kidx": nrm(ks[5], (DEPTH, n_pool, PAGE_SIZE, IDX_DIM), 1.0),
        "page_table": page_table,
        "c_prompt": nrm(ks[6], (BATCH, D_MODEL), 1.0),
        "c_sample": nrm(ks[7], (DEC_BATCH, D_MODEL), 1.0),
        "w_cond": nrm(ks[8], (DEPTH, D_MODEL, N_MOD * D_MODEL), 0.1 * D_MODEL ** -0.5),
        "b_cond": nrm(ks[9], (DEPTH, N_MOD * D_MODEL), 0.01),
        "w_in": nrm(ks[10], (DEPTH, D_MODEL, IN_WIDTH), D_MODEL ** -0.5),
        "ln_v_g": 1.0 + nrm(ks[11], (DEPTH, A_GROUPS, A_CH), 0.02),
        "ln_v_b": nrm(ks[12], (DEPTH, A_GROUPS, A_CH), 0.02),
        "w_spatial": nrm(ks[13], (DEPTH, A_GROUPS, CHUNK, CHUNK), CHUNK ** -0.5),
        "b_spatial": 1.0 + nrm(ks[14], (DEPTH, A_GROUPS, CHUNK), 0.02),
        "w_out": nrm(ks[15], (DEPTH, MIX_WIDTH, D_MODEL), DEEPNORM_BETA * MIX_WIDTH ** -0.5),
        "ln1_g": 1.0 + nrm(ks[16], (DEPTH, D_MODEL), 0.02),
        "ln1_b": nrm(ks[17], (DEPTH, D_MODEL), 0.02),
        "w_ff1": nrm(ks[18], (DEPTH, D_MODEL, D_FF), D_MODEL ** -0.5),
        "b_ff1": nrm(ks[19], (DEPTH, D_FF), 0.01),
        "w_ff2": nrm(ks[20], (DEPTH, D_FF, D_MODEL), DEEPNORM_BETA * D_FF ** -0.5),
        "b_ff2": nrm(ks[21], (DEPTH, D_MODEL), 0.01),
        "ln2_g": 1.0 + nrm(ks[22], (DEPTH, D_MODEL), 0.02),
        "ln2_b": nrm(ks[23], (DEPTH, D_MODEL), 0.02),
    }


def reference(x_prompt, x_sample, cache_k, cache_v, cache_kidx, page_table, c_prompt, c_sample,
              w_cond, b_cond, w_in, ln_v_g, ln_v_b, w_spatial, b_spatial, w_out,
              ln1_g, ln1_b, w_ff1, b_ff1, w_ff2, b_ff2, ln2_g, ln2_b):
    slopes = alibi_slopes()
    xp, xs = x_prompt, x_sample
    kp_l, vp_l, kip_l, ks_l, vs_l, kis_l, vc_l = [], [], [], [], [], [], []
    for l in range(DEPTH):
        mp = cond_params(c_prompt, w_cond[l], b_cond[l])
        ms = cond_params(c_sample, w_cond[l], b_cond[l])

        hp = modulate(xp, mp[0], mp[1])
        u, v, q, k, vv, qi, ki, wi = project(hp, w_in[l])
        a_out, _ = chunk_gate(u, v, ln_v_g[l], ln_v_b[l], w_spatial[l], b_spatial[l])
        b_out = prompt_sparse_attention(q, k, vv, qi, wi, ki, slopes)
        mix = jnp.einsum('bte,ed->btd', jnp.concatenate([a_out, b_out], axis=-1), w_out[l])
        xp = post_block(xp, mix, mp[2], ln1_g[l], ln1_b[l])
        ff = sq_relu_ffn(modulate(xp, mp[3], mp[4]), w_ff1[l], b_ff1[l], w_ff2[l], b_ff2[l])
        xp = post_block(xp, ff, mp[5], ln2_g[l], ln2_b[l])
        kp_l.append(k); vp_l.append(vv); kip_l.append(ki)

        hs = modulate(xs, ms[0], ms[1])
        u, v, q, k, vv, qi, ki, wi = project(hs, w_in[l])
        a_out, vn = chunk_gate(u, v, ln_v_g[l], ln_v_b[l], w_spatial[l], b_spatial[l])
        b_out = sample_sparse_attention(q, k, vv, qi, wi, ki, cache_k[l], cache_v[l], cache_kidx[l],
                                        page_table, slopes)
        mix = jnp.einsum('bte,ed->btd', jnp.concatenate([a_out, b_out], axis=-1), w_out[l])
        xs = post_block(xs, mix, ms[2], ln1_g[l], ln1_b[l])
        ff = sq_relu_ffn(modulate(xs, ms[3], ms[4]), w_ff1[l], b_ff1[l], w_ff2[l], b_ff2[l])
        xs = post_block(xs, ff, ms[5], ln2_g[l], ln2_b[l])
        ks_l.append(k); vs_l.append(vv); kis_l.append(ki); vc_l.append(vn)

    return (xp, xs, jnp.stack(kp_l), jnp.stack(vp_l), jnp.stack(kip_l),
            jnp.stack(ks_l), jnp.stack(vs_l), jnp.stack(kis_l), jnp.stack(vc_l))
```

```python
import functools

import jax
import jax.numpy as jnp
import numpy as np
from jax import lax
from jax.experimental import pallas as pl
from jax.experimental.pallas import tpu as pltpu

F32 = jnp.float32
BF16 = jnp.bfloat16
I32 = jnp.int32

CHUNK = 128
A_GROUPS = 8
A_CH = 128
A_WIDTH = A_GROUPS * A_CH
HEAD_DIM = 128
N_HEADS = 8
N_KV_HEADS = 2
GQA_GROUP = N_HEADS // N_KV_HEADS
ATT_WIDTH = N_HEADS * HEAD_DIM
KV_WIDTH = N_KV_HEADS * HEAD_DIM
IDX_HEADS = 16
IDX_DIM = 64
IDX_WIDTH = IDX_HEADS * IDX_DIM
TOPK_MAX = 256
Q_BLOCK = 128
PAGE_SIZE = 128
N_MOD = 6
LN_EPS = 1e-5
ATT_SCALE = HEAD_DIM ** -0.5
MAIN_WIDTH = 2 * A_WIDTH + ATT_WIDTH + 2 * KV_WIDTH + IDX_WIDTH
PROJ_TN = 512
FFN_TN = 512
ALIBI_SLOPES = tuple(float(2.0 ** (-8.0 * h / N_HEADS)) for h in range(1, N_HEADS + 1))

VMEM_LIMIT_BYTES = 58 * 1024 * 1024
INT_MIN = -(2 ** 31)
NEG_BIG = -1e30

NT_DIMS = (((1,), (1,)), ((), ()))


def _dot(a, b):
    return jnp.dot(a, b, preferred_element_type=F32)


def _dot_nt(a, b):
    return lax.dot_general(a, b, NT_DIMS, preferred_element_type=F32)


def _layer_norm(x, g, b):
    mu = jnp.mean(x, axis=-1, keepdims=True)
    xc = x - mu
    var = jnp.mean(xc * xc, axis=-1, keepdims=True)
    return xc * lax.rsqrt(var + LN_EPS) * g + b


def _sort_key(x):
    bits = pltpu.bitcast(x, I32)
    return jnp.where(bits < 0, bits ^ jnp.int32(0x7FFFFFFF), bits)


def _params(*sem):
    return pltpu.CompilerParams(dimension_semantics=sem, vmem_limit_bytes=VMEM_LIMIT_BYTES)


def _cond_kernel(c_ref, w_ref, b_ref, o_ref):
    c = c_ref[...]
    a = (c * jax.nn.sigmoid(c)).astype(BF16)
    o_ref[...] = _dot(a, w_ref[...].astype(BF16)) + b_ref[...]


def _cond(c, w, b):
    m, d = c.shape
    n = w.shape[1]
    tn = 1024
    return pl.pallas_call(
        _cond_kernel,
        out_shape=jax.ShapeDtypeStruct((m, n), F32),
        grid=(n // tn,),
        in_specs=[pl.BlockSpec((m, d), lambda j: (0, 0)),
                  pl.BlockSpec((d, tn), lambda j: (0, j)),
                  pl.BlockSpec((1, tn), lambda j: (0, j))],
        out_specs=pl.BlockSpec((m, tn), lambda j: (0, j)),
        compiler_params=_params("arbitrary"),
        name="cond",
    )(c, w, b.reshape(1, n))


_J_V, _J_Q, _J_KV, _J_QI, _J_END = 2, 4, 6, 7, 9


def _proj_kernel(x_ref, shift_ref, scale_ref, wm_ref, wt_ref, wvt_ref, wwt_ref, lng_ref, lnb_ref,
                 u_ref, vn_ref, q_ref, k_ref, v_ref, kb_ref, vt_ref, qi_ref, ki_ref, ki2_ref, wit_ref,
                 h_scr):
    j = pl.program_id(1)

    @pl.when(j == 0)
    def _():
        h = (x_ref[...] * (1.0 + scale_ref[...]) + shift_ref[...]).astype(BF16)
        h_scr[...] = h
        tail = _dot(h, wt_ref[...])
        ki_ref[...] = tail[:, :IDX_DIM]
        ki2_ref[...] = tail.astype(BF16)
        vt = _dot_nt(wvt_ref[...], h).astype(BF16)
        for c in range(vt_ref.shape[0]):
            vt_ref[c] = vt[:, c * 128:(c + 1) * 128]
        wit_ref[...] = _dot_nt(wwt_ref[...], h) * (IDX_HEADS ** -0.5)

    z = _dot(h_scr[...], wm_ref[...])

    @pl.when(j < _J_V)
    def _():
        u_ref[...] = z

    @pl.when(jnp.logical_and(j >= _J_V, j < _J_Q))
    def _():
        for g in range(PROJ_TN // A_CH):
            sl = slice(g * A_CH, (g + 1) * A_CH)
            vn_ref[:, sl] = _layer_norm(z[:, sl], lng_ref[:, sl], lnb_ref[:, sl]).astype(vn_ref.dtype)

    @pl.when(jnp.logical_and(j >= _J_Q, j < _J_KV))
    def _():
        q_ref[...] = z.astype(BF16)

    @pl.when(j == _J_KV)
    def _():
        k_ref[...] = z[:, :KV_WIDTH]
        v_ref[...] = z[:, KV_WIDTH:]
        kb_ref[...] = z[:, :KV_WIDTH].astype(BF16)

    @pl.when(j >= _J_QI)
    def _():
        qi_ref[...] = z.astype(BF16)


def _proj(x, shift, scale, wm, wt, wvt, wwt, lng, lnb, *, tm, vn_dtype):
    n, d = x.shape
    nb, r, _ = shift.shape
    rows_per_mod = n // nb
    assert n % tm == 0 and rows_per_mod % tm == 0 and r in (1, tm)
    tiles_per_mod = rows_per_mod // tm
    tn = PROJ_TN

    def clipj(lo, cnt):
        return lambda i, j: (i, jnp.clip(j - lo, 0, cnt - 1))

    mod_spec = pl.BlockSpec((None, r, d), lambda i, j: (i // tiles_per_mod, 0, 0))
    out_shape = (
        jax.ShapeDtypeStruct((n, A_WIDTH), F32),
        jax.ShapeDtypeStruct((n, A_WIDTH), vn_dtype),
        jax.ShapeDtypeStruct((n, ATT_WIDTH), BF16),
        jax.ShapeDtypeStruct((n, KV_WIDTH), F32),
        jax.ShapeDtypeStruct((n, KV_WIDTH), F32),
        jax.ShapeDtypeStruct((n, KV_WIDTH), BF16),
        jax.ShapeDtypeStruct((n // 128, KV_WIDTH, 128), BF16),
        jax.ShapeDtypeStruct((n, IDX_WIDTH), BF16),
        jax.ShapeDtypeStruct((n, IDX_DIM), F32),
        jax.ShapeDtypeStruct((n, 2 * IDX_DIM), BF16),
        jax.ShapeDtypeStruct((IDX_HEADS, n), F32),
    )
    out_specs = (
        pl.BlockSpec((tm, tn), clipj(0, 2)),
        pl.BlockSpec((tm, tn), clipj(_J_V, 2)),
        pl.BlockSpec((tm, tn), clipj(_J_Q, 2)),
        pl.BlockSpec((tm, KV_WIDTH), lambda i, j: (i, 0)),
        pl.BlockSpec((tm, KV_WIDTH), lambda i, j: (i, 0)),
        pl.BlockSpec((tm, KV_WIDTH), lambda i, j: (i, 0)),
        pl.BlockSpec((tm // 128, KV_WIDTH, 128), lambda i, j: (i, 0, 0)),
        pl.BlockSpec((tm, tn), clipj(_J_QI, 2)),
        pl.BlockSpec((tm, IDX_DIM), lambda i, j: (i, 0)),
        pl.BlockSpec((tm, 2 * IDX_DIM), lambda i, j: (i, 0)),
        pl.BlockSpec((IDX_HEADS, tm), lambda i, j: (0, i)),
    )
    in_specs = [
        pl.BlockSpec((tm, d), lambda i, j: (i, 0)),
        mod_spec, mod_spec,
        pl.BlockSpec((d, tn), lambda i, j: (0, j)),
        pl.BlockSpec((d, 2 * IDX_DIM), lambda i, j: (0, 0)),
        pl.BlockSpec((KV_WIDTH, d), lambda i, j: (0, 0)),
        pl.BlockSpec((IDX_HEADS, d), lambda i, j: (0, 0)),
        pl.BlockSpec((1, tn), lambda i, j: (0, jnp.clip(j - _J_V, 0, 1))),
        pl.BlockSpec((1, tn), lambda i, j: (0, jnp.clip(j - _J_V, 0, 1))),
    ]
    return pl.pallas_call(
        _proj_kernel,
        out_shape=out_shape,
        grid=(n // tm, _J_END),
        in_specs=in_specs,
        out_specs=out_specs,
        scratch_shapes=[pltpu.VMEM((tm, d), BF16)],
        compiler_params=_params("arbitrary", "arbitrary"),
        name="proj",
    )(x, shift, scale, wm, wt, wvt, wwt, lng, lnb)


def _mix_kernel(u_ref, vn_ref, q_ref, qi_ref, wit_ref, k_ref, vt_ref, ki2_ref, ws_ref, bsp_ref,
                o_ref, key_scr, sel_scr, qim_scr, *, n_sel):
    jq = pl.program_id(1)
    nt = jq + 1
    row = lax.broadcasted_iota(I32, (128, 128), 0)
    col = lax.broadcasted_iota(I32, (128, 128), 1)

    for p in range(IDX_HEADS // 2):
        pair = qi_ref[:, p * 128:(p + 1) * 128]
        zero = jnp.zeros_like(pair)
        qim_scr[2 * p] = jnp.where(col < IDX_DIM, pair, zero)
        qim_scr[2 * p + 1] = jnp.where(col >= IDX_DIM, pair, zero)
    wis = wit_ref[...] * (IDX_DIM ** -0.5)

    def idx_body(kt, carry):
        kk = ki2_ref[pl.ds(pl.multiple_of(kt * 128, 128), 128), :]
        acc = jnp.zeros((128, 128), F32)
        for h in range(IDX_HEADS):
            s = _dot_nt(kk, qim_scr[h])
            acc = acc + jnp.maximum(s, 0.0) * wis[h:h + 1, :]
        key = _sort_key(acc)
        inadmissible = (row - col) > (jq - kt) * 128
        key_scr[kt] = jnp.where(inadmissible, jnp.int32(INT_MIN), key)
        return carry

    lax.fori_loop(0, nt, idx_body, 0)

    def count_ge(cand):
        def body(kt, c):
            m = jnp.where(key_scr[kt] >= cand, 1, 0).astype(I32)
            return c + jnp.sum(m.reshape(16, 8, 128), axis=0)
        c = lax.fori_loop(0, nt, body, jnp.zeros((8, 128), I32))
        return jnp.sum(c, axis=0, keepdims=True)

    def bit_body(i, t):
        cand = t ^ lax.shift_left(jnp.int32(1), 31 - i)
        return jnp.where(count_ge(cand) >= n_sel, cand, t)

    thr = lax.fori_loop(0, 32, bit_body, jnp.full((1, 128), INT_MIN, I32))
    c_ge = count_ge(thr)
    c_gt = count_ge(jnp.where(thr == jnp.int32(2 ** 31 - 1), thr, thr + 1))
    c_gt = jnp.where(thr == jnp.int32(2 ** 31 - 1), 0, c_gt)
    has_tie_overflow = jnp.max(c_ge) > n_sel

    @pl.when(jnp.logical_not(has_tie_overflow))
    def _():
        t_adm = jnp.maximum(thr, jnp.int32(INT_MIN + 1))

        def body(kt, carry):
            sel_scr[kt] = jnp.where(key_scr[kt] >= t_adm, 1.0, 0.0).astype(F32)
            return carry
        lax.fori_loop(0, nt, body, 0)

    @pl.when(has_tie_overflow)
    def _():
        need = (n_sel - c_gt).astype(F32)
        lstrict = jnp.where(col < row, 1.0, 0.0).astype(BF16)

        def body(kt, before):
            key = key_scr[kt]
            eq = key == thr
            eqf = jnp.where(eq, 1.0, 0.0).astype(F32)
            rank = before + _dot(lstrict, eqf.astype(BF16))
            sel = jnp.logical_or(key > thr, jnp.logical_and(eq, rank < need))
            sel = jnp.logical_and(sel, key > jnp.int32(INT_MIN))
            sel_scr[kt] = jnp.where(sel, 1.0, 0.0).astype(F32)
            return before + jnp.sum(eqf, axis=0, keepdims=True)
        lax.fori_loop(0, nt, body, jnp.zeros((1, 128), F32))

    dist0 = col - row
    for h in range(N_HEADS):
        g = h // GQA_GROUP
        q_h = q_ref[:, h * HEAD_DIM:(h + 1) * HEAD_DIM]
        slope = ALIBI_SLOPES[h]

        def att_body(kt, carry, q_h=q_h, g=g, slope=slope):
            m, l, acc = carry
            k_t = k_ref[pl.ds(pl.multiple_of(kt * 128, 128), 128), g * HEAD_DIM:(g + 1) * HEAD_DIM]
            lg = _dot_nt(k_t, q_h) * ATT_SCALE
            dist = (dist0 + (jq - kt) * 128).astype(F32)
            lg = lg - slope * dist
            lg = jnp.where(sel_scr[kt] > 0.5, lg, -jnp.inf)
            m_new = jnp.maximum(m, jnp.max(lg, axis=0, keepdims=True))
            alpha = jnp.exp(m - m_new)
            p = jnp.exp(lg - m_new)
            l_new = alpha * l + jnp.sum(p, axis=0, keepdims=True)
            vt_t = vt_ref[kt, g * HEAD_DIM:(g + 1) * HEAD_DIM, :]
            acc_new = alpha * acc + _dot(vt_t, p.astype(BF16))
            return m_new, l_new, acc_new

        init = (jnp.full((1, 128), NEG_BIG, F32), jnp.zeros((1, 128), F32), jnp.zeros((HEAD_DIM, 128), F32))
        _, l, acc = lax.fori_loop(0, nt, att_body, init)
        out_t = acc / l
        o_ref[:, A_WIDTH + h * HEAD_DIM:A_WIDTH + (h + 1) * HEAD_DIM] = out_t.T.astype(o_ref.dtype)

    for g in range(A_GROUPS):
        sl = slice(g * A_CH, (g + 1) * A_CH)
        sp = _dot(ws_ref[g], vn_ref[:, sl]) + bsp_ref[:, sl]
        o_ref[:, sl] = (u_ref[:, sl] * sp).astype(o_ref.dtype)


def _mix(u, vn, q, qi, wit, kb, vt, ki2, ws, bsp, *, batch, seq, n_sel):
    n = batch * seq
    nblk = seq // Q_BLOCK
    ntile = seq // 128
    row_spec = lambda w: pl.BlockSpec((Q_BLOCK, w), lambda b, j: (b * nblk + j, 0))
    in_specs = [
        row_spec(A_WIDTH), row_spec(A_WIDTH), row_spec(ATT_WIDTH), row_spec(IDX_WIDTH),
        pl.BlockSpec((IDX_HEADS, Q_BLOCK), lambda b, j: (0, b * nblk + j)),
        pl.BlockSpec((seq, KV_WIDTH), lambda b, j: (b, 0)),
        pl.BlockSpec((ntile, KV_WIDTH, 128), lambda b, j: (b, 0, 0)),
        pl.BlockSpec((seq, 2 * IDX_DIM), lambda b, j: (b, 0)),
        pl.BlockSpec((A_GROUPS, CHUNK, CHUNK), lambda b, j: (0, 0, 0)),
        pl.BlockSpec((CHUNK, A_WIDTH), lambda b, j: (0, 0)),
    ]
    return pl.pallas_call(
        functools.partial(_mix_kernel, n_sel=n_sel),
        out_shape=jax.ShapeDtypeStruct((n, A_WIDTH + ATT_WIDTH), BF16),
        grid=(batch, nblk),
        in_specs=in_specs,
        out_specs=pl.BlockSpec((Q_BLOCK, A_WIDTH + ATT_WIDTH), lambda b, j: (b * nblk + j, 0)),
        scratch_shapes=[pltpu.VMEM((ntile, 128, 128), I32),
                        pltpu.VMEM((ntile, 128, 128), F32),
                        pltpu.VMEM((IDX_HEADS, 128, 128), BF16)],
        compiler_params=_params("arbitrary", "arbitrary"),
        name="mix",
    )(u, vn, q, qi, wit, kb, vt, ki2, ws, bsp)


def _s_scores_kernel(pt_ref, qi_ref, wi_ref, kin_ref, *rest, n_pages):
    del pt_ref
    pages = rest[:n_pages]
    sc_ref, self_ref = rest[n_pages], rest[n_pages + 1]
    qi = qi_ref[...]
    w = wi_ref[...] * (IDX_DIM ** -0.5)
    for p in range(n_pages):
        kp = pages[p][...].astype(BF16)
        s = _dot_nt(qi, kp)
        sc_ref[p:p + 1, :] = jnp.sum(jnp.maximum(s, 0.0) * w, axis=0, keepdims=True)
    kin = kin_ref[...].astype(BF16).astype(F32)
    s_self = jnp.sum(qi.astype(F32) * kin, axis=1, keepdims=True)
    v_self = jnp.sum(jnp.maximum(s_self, 0.0) * w, axis=0, keepdims=True)
    self_ref[...] = jnp.broadcast_to(v_self, self_ref.shape)


def _s_scores(page_table_flat, qi3, wi_col, ki_new3, cache_kidx_l, *, n_pages):
    bd = qi3.shape[0]

    def page_spec(p):
        return pl.BlockSpec((None, PAGE_SIZE, IDX_DIM), lambda b, pt, p=p: (pt[b * n_pages + p], 0, 0))

    grid_spec = pltpu.PrefetchScalarGridSpec(
        num_scalar_prefetch=1,
        grid=(bd,),
        in_specs=[pl.BlockSpec((None, IDX_HEADS, IDX_DIM), lambda b, pt: (b, 0, 0)),
                  pl.BlockSpec((None, IDX_HEADS, 1), lambda b, pt: (b, 0, 0)),
                  pl.BlockSpec((None, 1, IDX_DIM), lambda b, pt: (b, 0, 0))]
                 + [page_spec(p) for p in range(n_pages)],
        out_specs=(pl.BlockSpec((None, n_pages, PAGE_SIZE), lambda b, pt: (b, 0, 0)),
                   pl.BlockSpec((None, 1, 128), lambda b, pt: (b, 0, 0))),
    )
    return pl.pallas_call(
        functools.partial(_s_scores_kernel, n_pages=n_pages),
        out_shape=(jax.ShapeDtypeStruct((bd, n_pages, PAGE_SIZE), F32),
                   jax.ShapeDtypeStruct((bd, 1, 128), F32)),
        grid_spec=grid_spec,
        compiler_params=_params("arbitrary"),
        name="s_scores",
    )(page_table_flat, qi3, wi_col, ki_new3, *([cache_kidx_l] * n_pages))


def _s_select_kernel(sc_ref, self_ref, u_ref, vn_ref, ws0_ref, bs0_ref,
                     sel_ref, selself_ref, a_ref, *, n_sel):
    keys = _sort_key(sc_ref[...])
    kself = _sort_key(self_ref[:, 0:1])
    bd, past = keys.shape

    def count_ge(cand):
        c = jnp.sum(jnp.where(keys >= cand, 1, 0).astype(I32), axis=1, keepdims=True)
        return c + jnp.where(kself >= cand, 1, 0).astype(I32)

    def bit_body(i, t):
        cand = t ^ lax.shift_left(jnp.int32(1), 31 - i)
        return jnp.where(count_ge(cand) >= n_sel, cand, t)

    thr = lax.fori_loop(0, 32, bit_body, jnp.full((bd, 1), INT_MIN, I32))
    c_gt = (jnp.sum(jnp.where(keys > thr, 1, 0).astype(I32), axis=1, keepdims=True)
            + jnp.where(kself > thr, 1, 0).astype(I32))
    need = (n_sel - c_gt).astype(F32)
    r = lax.broadcasted_iota(I32, (128, 128), 0)
    c = lax.broadcasted_iota(I32, (128, 128), 1)
    ustrict = jnp.where(r < c, 1.0, 0.0).astype(BF16)
    before = jnp.zeros((bd, 1), F32)
    for t in range(past // 128):
        kt = keys[:, t * 128:(t + 1) * 128]
        eq = kt == thr
        eqf = jnp.where(eq, 1.0, 0.0).astype(F32)
        rank = before + _dot(eqf.astype(BF16), ustrict)
        sel = jnp.logical_or(kt > thr, jnp.logical_and(eq, rank < need))
        sel_ref[:, t * 128:(t + 1) * 128] = jnp.where(sel, 1.0, 0.0).astype(F32)
        before = before + jnp.sum(eqf, axis=1, keepdims=True)
    sel_self = jnp.logical_or(kself > thr, jnp.logical_and(kself == thr, before < need))
    selself_ref[...] = jnp.broadcast_to(jnp.where(sel_self, 1.0, 0.0).astype(F32), selself_ref.shape)
    sp = ws0_ref[...] * vn_ref[...] + bs0_ref[...]
    a_ref[...] = (u_ref[...] * sp).astype(a_ref.dtype)


def _s_select(sc, sc_self, u, vn, ws0, bs0, *, n_sel):
    bd, past = sc.shape
    full = lambda a: pl.BlockSpec(a.shape, lambda i: (0,) * a.ndim)
    args = (sc, sc_self, u, vn, ws0, bs0)
    return pl.pallas_call(
        functools.partial(_s_select_kernel, n_sel=n_sel),
        out_shape=(jax.ShapeDtypeStruct((bd, past), F32),
                   jax.ShapeDtypeStruct((bd, 128), F32),
                   jax.ShapeDtypeStruct((bd, A_WIDTH), BF16)),
        grid=(1,),
        in_specs=[full(a) for a in args],
        out_specs=(pl.BlockSpec((bd, past), lambda i: (0, 0)),
                   pl.BlockSpec((bd, 128), lambda i: (0, 0)),
                   pl.BlockSpec((bd, A_WIDTH), lambda i: (0, 0))),
        compiler_params=_params("arbitrary"),
        name="s_select",
    )(*args)


def _s_attn_kernel(pt_ref, q_ref, sel_ref, selself_ref, knew_ref, vnew_ref, *rest, n_pages, past):
    del pt_ref
    kpages = rest[:n_pages]
    vpages = rest[n_pages:2 * n_pages]
    o_ref = rest[2 * n_pages]
    q = q_ref[...]
    hrow = lax.broadcasted_iota(I32, (N_HEADS, 2 * PAGE_SIZE), 0)
    ccol = lax.broadcasted_iota(I32, (N_HEADS, 2 * PAGE_SIZE), 1)
    own_kv = (ccol & 1) == (hrow // GQA_GROUP)
    pos_in_page = ccol >> 1
    hcol = lax.broadcasted_iota(I32, (N_HEADS, 1), 0)
    slope = jnp.zeros((N_HEADS, 1), F32)
    for h in range(N_HEADS):
        slope = jnp.where(hcol == h, ALIBI_SLOPES[h], slope)

    logits = []
    for p in range(n_pages):
        kp = kpages[p][...].astype(BF16)
        lg = _dot_nt(q, kp) * ATT_SCALE
        dist = (past - (p * PAGE_SIZE + pos_in_page)).astype(F32)
        lg = lg - slope * dist
        ok = jnp.logical_and(own_kv, sel_ref[p:p + 1, :] > 0.5)
        logits.append(jnp.where(ok, lg, -jnp.inf))
    first_group = lax.broadcasted_iota(I32, (N_HEADS, HEAD_DIM), 0) < GQA_GROUP
    knew = knew_ref[...].astype(BF16).astype(F32)
    vnew = vnew_ref[...].astype(BF16).astype(F32)
    knew8 = jnp.where(first_group, knew[0:1, :], knew[1:2, :])
    vnew8 = jnp.where(first_group, vnew[0:1, :], vnew[1:2, :])
    lg_self = jnp.sum(q.astype(F32) * knew8, axis=1, keepdims=True) * ATT_SCALE
    lg_self = jnp.where(selself_ref[:, 0:1] > 0.5, lg_self, -jnp.inf)

    m = jnp.maximum(lg_self, NEG_BIG)
    for lg in logits:
        m = jnp.maximum(m, jnp.max(lg, axis=1, keepdims=True))
    p_self = jnp.exp(lg_self - m)
    l = p_self
    acc = p_self.astype(BF16).astype(F32) * vnew8
    for p in range(n_pages):
        pp = jnp.exp(logits[p] - m)
        l = l + jnp.sum(pp, axis=1, keepdims=True)
        acc = acc + _dot(pp.astype(BF16), vpages[p][...].astype(BF16))
    o_ref[...] = (acc / l).astype(o_ref.dtype)


def _s_attn(page_table_flat, q3, sel2, sel_self3, k_new3, v_new3, cache_k2, cache_v2, *, n_pages, past):
    bd = q3.shape[0]

    def page_spec(p):
        return pl.BlockSpec((None, 2 * PAGE_SIZE, HEAD_DIM), lambda b, pt, p=p: (pt[b * n_pages + p], 0, 0))

    per_b = lambda s1, s2: pl.BlockSpec((None, s1, s2), lambda b, pt: (b, 0, 0))
    grid_spec = pltpu.PrefetchScalarGridSpec(
        num_scalar_prefetch=1,
        grid=(bd,),
        in_specs=[per_b(N_HEADS, HEAD_DIM), per_b(n_pages, 2 * PAGE_SIZE), per_b(1, 128),
                  per_b(N_KV_HEADS, HEAD_DIM), per_b(N_KV_HEADS, HEAD_DIM)]
                 + [page_spec(p) for p in range(n_pages)] * 2,
        out_specs=per_b(N_HEADS, HEAD_DIM),
    )
    return pl.pallas_call(
        functools.partial(_s_attn_kernel, n_pages=n_pages, past=past),
        out_shape=jax.ShapeDtypeStruct((bd, N_HEADS, HEAD_DIM), BF16),
        grid_spec=grid_spec,
        compiler_params=_params("arbitrary"),
        name="s_attn",
    )(page_table_flat, q3, sel2, sel_self3, k_new3, v_new3, *([cache_k2] * n_pages), *([cache_v2] * n_pages))


def _outln_kernel(mix_ref, x_ref, gate_ref, w_ref, g_ref, b_ref, o_ref, *, alpha):
    mix = _dot(mix_ref[...], w_ref[...])
    y = alpha * x_ref[...] + (1.0 + gate_ref[...]) * mix
    o_ref[...] = _layer_norm(y, g_ref[...], b_ref[...])


def _outln(mixin, x, gate, w, g, b, *, tm, alpha):
    n, d = x.shape
    nb, r, _ = gate.shape
    tiles_per_mod = (n // nb) // tm
    kin = mixin.shape[1]
    return pl.pallas_call(
        functools.partial(_outln_kernel, alpha=alpha),
        out_shape=jax.ShapeDtypeStruct((n, d), F32),
        grid=(n // tm,),
        in_specs=[pl.BlockSpec((tm, kin), lambda i: (i, 0)),
                  pl.BlockSpec((tm, d), lambda i: (i, 0)),
                  pl.BlockSpec((None, r, d), lambda i: (i // tiles_per_mod, 0, 0)),
                  pl.BlockSpec((kin, d), lambda i: (0, 0), pipeline_mode=pl.Buffered(1)),
                  pl.BlockSpec((1, d), lambda i: (0, 0)),
                  pl.BlockSpec((1, d), lambda i: (0, 0))],
        out_specs=pl.BlockSpec((tm, d), lambda i: (i, 0)),
        compiler_params=_params("arbitrary"),
        name="outln",
    )(mixin, x, gate, w, g, b)


def _ffn_kernel(x_ref, shift_ref, scale_ref, gate_ref, w1_ref, b1_ref, w2_ref, b2_ref, g_ref, b_ref,
                o_ref, h_scr, *, alpha):
    f = pl.program_id(1)

    @pl.when(f == 0)
    def _():
        h_scr[...] = (x_ref[...] * (1.0 + scale_ref[...]) + shift_ref[...]).astype(BF16)
        o_ref[...] = jnp.zeros_like(o_ref)

    a = jnp.maximum(_dot(h_scr[...], w1_ref[...]) + b1_ref[...], 0.0)
    a2 = (a * a).astype(BF16)
    for c in range(o_ref.shape[1] // FFN_TN):
        cs = slice(c * FFN_TN, (c + 1) * FFN_TN)
        o_ref[:, cs] += _dot(a2, w2_ref[:, cs])

    @pl.when(f == pl.num_programs(1) - 1)
    def _():
        y = alpha * x_ref[...] + (1.0 + gate_ref[...]) * (o_ref[...] + b2_ref[...])
        o_ref[...] = _layer_norm(y, g_ref[...], b_ref[...])


def _ffn(x, shift, scale, gate, w1, b1, w2, b2, g, b, *, tm, tf, alpha):
    n, d = x.shape
    nb, r, _ = gate.shape
    dff = w1.shape[1]
    tiles_per_mod = (n // nb) // tm
    mod_spec = pl.BlockSpec((None, r, d), lambda i, f: (i // tiles_per_mod, 0, 0))
    return pl.pallas_call(
        functools.partial(_ffn_kernel, alpha=alpha),
        out_shape=jax.ShapeDtypeStruct((n, d), F32),
        grid=(n // tm, dff // tf),
        in_specs=[pl.BlockSpec((tm, d), lambda i, f: (i, 0), pipeline_mode=pl.Buffered(1)),
                  mod_spec, mod_spec, mod_spec,
                  pl.BlockSpec((d, tf), lambda i, f: (0, f)),
                  pl.BlockSpec((1, tf), lambda i, f: (0, f)),
                  pl.BlockSpec((tf, d), lambda i, f: (f, 0)),
                  pl.BlockSpec((1, d), lambda i, f: (0, 0)),
                  pl.BlockSpec((1, d), lambda i, f: (0, 0)),
                  pl.BlockSpec((1, d), lambda i, f: (0, 0))],
        out_specs=pl.BlockSpec((tm, d), lambda i, f: (i, 0)),
        scratch_shapes=[pltpu.VMEM((tm, d), BF16)],
        compiler_params=_params("arbitrary", "arbitrary"),
        name="ffn",
    )(x, shift, scale, gate, w1, b1, w2, b2, g, b)


def kernel(x_prompt, x_sample, cache_k, cache_v, cache_kidx, page_table, c_prompt, c_sample,
           w_cond, b_cond, w_in, ln_v_g, ln_v_b, w_spatial, b_spatial, w_out,
           ln1_g, ln1_b, w_ff1, b_ff1, w_ff2, b_ff2, ln2_g, ln2_b):
    batch, seq, d = x_prompt.shape
    bd, ts, _ = x_sample.shape
    depth = w_in.shape[0]
    n_pages = page_table.shape[1]
    past = n_pages * PAGE_SIZE
    n_pool = cache_k.shape[1]
    dff = w_ff1.shape[2]
    assert ts == 1 and seq % Q_BLOCK == 0 and d == A_WIDTH + ATT_WIDTH
    assert w_in.shape[2] == MAIN_WIDTH + IDX_DIM + IDX_HEADS
    n = batch * seq
    alpha = (2 * depth) ** 0.25
    n_sel_p = min(TOPK_MAX, seq // 4)
    n_sel_s = min(TOPK_MAX, (past + ts) // 4)
    tm_p = min(1024, seq)
    tm_o = min(512, seq)
    tf = min(512, dff)
    pt_flat = page_table.reshape(-1).astype(I32)

    xp = x_prompt.reshape(n, d)
    xs = x_sample.reshape(bd, d)
    c_all = jnp.concatenate([c_prompt, c_sample], axis=0)

    outs = {k: [] for k in ("kp", "vp", "kip", "ks", "vs", "kis", "vc")}
    for l in range(depth):
        w_in_l = w_in[l]
        wm = w_in_l[:, :MAIN_WIDTH].astype(BF16)
        w_ki = w_in_l[:, MAIN_WIDTH:MAIN_WIDTH + IDX_DIM]
        wt = jnp.concatenate([w_ki, w_ki], axis=1).astype(BF16)
        v_lo = 2 * A_WIDTH + ATT_WIDTH + KV_WIDTH
        wvt = w_in_l[:, v_lo:v_lo + KV_WIDTH].T.astype(BF16)
        wwt = w_in_l[:, MAIN_WIDTH + IDX_DIM:].T.astype(BF16)
        lng = ln_v_g[l].reshape(1, A_WIDTH)
        lnb = ln_v_b[l].reshape(1, A_WIDTH)
        tril = jnp.tril(jnp.ones((CHUNK, CHUNK), dtype=bool))
        ws = jnp.where(tril[None], w_spatial[l], 0.0).astype(BF16)
        bsp = jnp.repeat(jnp.transpose(b_spatial[l]), A_CH, axis=1)
        ws0 = jnp.repeat(w_spatial[l][:, 0, 0], A_CH).reshape(1, A_WIDTH)
        bs0 = jnp.repeat(b_spatial[l][:, 0], A_CH).reshape(1, A_WIDTH)
        w_out_b = w_out[l].astype(BF16)
        w1_b = w_ff1[l].astype(BF16)
        w2_b = w_ff2[l].astype(BF16)
        b1 = b_ff1[l].reshape(1, dff)
        b2 = b_ff2[l].reshape(1, d)
        g1, be1 = ln1_g[l].reshape(1, d), ln1_b[l].reshape(1, d)
        g2, be2 = ln2_g[l].reshape(1, d), ln2_b[l].reshape(1, d)

        z = _cond(c_all, w_cond[l], b_cond[l])
        mods = [z[:, i * d:(i + 1) * d] for i in range(N_MOD)]
        mp = [m[:batch].reshape(batch, 1, d) for m in mods]
        ms = [m[batch:].reshape(1, bd, d) for m in mods]

        (u, vn, q, k, v, kb, vt, qi, ki, ki2, wit) = _proj(
            xp, mp[0], mp[1], wm, wt, wvt, wwt, lng, lnb, tm=tm_p, vn_dtype=BF16)
        mixin = _mix(u, vn, q, qi, wit, kb, vt, ki2, ws, bsp, batch=batch, seq=seq, n_sel=n_sel_p)
        x1 = _outln(mixin, xp, mp[2], w_out_b, g1, be1, tm=tm_o, alpha=alpha)
        xp = _ffn(x1, mp[3], mp[4], mp[5], w1_b, b1, w2_b, b2, g2, be2, tm=tm_p, tf=tf, alpha=alpha)
        outs["kp"].append(k.reshape(batch, seq, N_KV_HEADS, HEAD_DIM))
        outs["vp"].append(v.reshape(batch, seq, N_KV_HEADS, HEAD_DIM))
        outs["kip"].append(ki.reshape(batch, seq, IDX_DIM))

        (u, vn, q, k, v, _, _, qi, ki, _, wit) = _proj(
            xs, ms[0], ms[1], wm, wt, wvt, wwt, lng, lnb, tm=bd, vn_dtype=F32)
        sc, sc_self = _s_scores(pt_flat, qi.reshape(bd, IDX_HEADS, IDX_DIM),
                                jnp.transpose(wit).reshape(bd, IDX_HEADS, 1),
                                ki.reshape(bd, 1, IDX_DIM), cache_kidx[l], n_pages=n_pages)
        sel, sel_self, a_out = _s_select(sc.reshape(bd, past), sc_self.reshape(bd, 128), u, vn, ws0, bs0,
                                         n_sel=n_sel_s)
        sel2 = jnp.repeat(sel.reshape(bd, n_pages, PAGE_SIZE), 2, axis=2)
        b_out = _s_attn(pt_flat, q.reshape(bd, N_HEADS, HEAD_DIM), sel2, sel_self.reshape(bd, 1, 128),
                        k.reshape(bd, N_KV_HEADS, HEAD_DIM), v.reshape(bd, N_KV_HEADS, HEAD_DIM),
                        cache_k[l].reshape(n_pool, 2 * PAGE_SIZE, HEAD_DIM),
                        cache_v[l].reshape(n_pool, 2 * PAGE_SIZE, HEAD_DIM),
                        n_pages=n_pages, past=past)
        mixin = jnp.concatenate([a_out, b_out.reshape(bd, ATT_WIDTH)], axis=1)
        x1 = _outln(mixin, xs, ms[2], w_out_b, g1, be1, tm=bd, alpha=alpha)
        xs = _ffn(x1, ms[3], ms[4], ms[5], w1_b, b1, w2_b, b2, g2, be2, tm=bd, tf=tf, alpha=alpha)
        outs["ks"].append(k.reshape(bd, ts, N_KV_HEADS, HEAD_DIM))
        outs["vs"].append(v.reshape(bd, ts, N_KV_HEADS, HEAD_DIM))
        outs["kis"].append(ki.reshape(bd, ts, IDX_DIM))
        outs["vc"].append(vn.reshape(bd, ts, A_GROUPS, A_CH))

    st = lambda name: jnp.stack(outs[name])
    return (xp.reshape(batch, seq, d), xs.reshape(bd, ts, d),
            st("kp"), st("vp"), st("kip"), st("ks"), st("vs"), st("kis"), st("vc"))
```

```python
import functools

import jax
import jax.numpy as jnp
import numpy as np
from jax import lax
from jax.experimental import pallas as pl
from jax.experimental.pallas import tpu as pltpu

F32 = jnp.float32
BF16 = jnp.bfloat16
I32 = jnp.int32

CHUNK = 128
A_GROUPS = 8
A_CH = 128
A_WIDTH = A_GROUPS * A_CH
HEAD_DIM = 128
N_HEADS = 8
N_KV_HEADS = 2
GQA_GROUP = N_HEADS // N_KV_HEADS
ATT_WIDTH = N_HEADS * HEAD_DIM
KV_WIDTH = N_KV_HEADS * HEAD_DIM
IDX_HEADS = 16
IDX_DIM = 64
IDX_WIDTH = IDX_HEADS * IDX_DIM
TOPK_MAX = 256
Q_BLOCK = 128
PAGE_SIZE = 128
N_MOD = 6
LN_EPS = 1e-5
ATT_SCALE = HEAD_DIM ** -0.5
MAIN_WIDTH = 2 * A_WIDTH + ATT_WIDTH + 2 * KV_WIDTH + IDX_WIDTH
PROJ_TN = 512
FFN_TN = 512
ALIBI_SLOPES = tuple(float(2.0 ** (-8.0 * h / N_HEADS)) for h in range(1, N_HEADS + 1))

VMEM_LIMIT_BYTES = 58 * 1024 * 1024
INT_MIN = -(2 ** 31)
NEG_BIG = -1e30
FLT_MAX = float(np.finfo(np.float32).max)
LOG2E = float(np.log2(np.e))
KC = 512
AC = 256

NT_DIMS = (((1,), (1,)), ((), ()))


def _dot(a, b):
    return jnp.dot(a, b, preferred_element_type=F32)


def _dot_nt(a, b):
    return lax.dot_general(a, b, NT_DIMS, preferred_element_type=F32)


def _layer_norm(x, g, b):
    mu = jnp.mean(x, axis=-1, keepdims=True)
    xc = x - mu
    var = jnp.mean(xc * xc, axis=-1, keepdims=True)
    return xc * lax.rsqrt(var + LN_EPS) * g + b


def _key_to_float(key):
    bits = jnp.where(key < 0, key ^ jnp.int32(0x7FFFFFFF), key)
    return pltpu.bitcast(bits, F32)


def _params(*sem):
    return pltpu.CompilerParams(dimension_semantics=sem, vmem_limit_bytes=VMEM_LIMIT_BYTES)


def _cond_kernel(c_ref, w_ref, b_ref, o_ref):
    c = c_ref[...]
    a = (c * jax.nn.sigmoid(c)).astype(BF16)
    o_ref[...] = _dot(a, w_ref[...].astype(BF16)) + b_ref[...]


def _cond(c, w, b):
    m, d = c.shape
    n = w.shape[1]
    tn = 1024
    return pl.pallas_call(
        _cond_kernel,
        out_shape=jax.ShapeDtypeStruct((m, n), F32),
        grid=(n // tn,),
        in_specs=[pl.BlockSpec((m, d), lambda j: (0, 0)),
                  pl.BlockSpec((d, tn), lambda j: (0, j)),
                  pl.BlockSpec((1, tn), lambda j: (0, j))],
        out_specs=pl.BlockSpec((m, tn), lambda j: (0, j)),
        compiler_params=_params("arbitrary"),
        name="cond",
    )(c, w, b.reshape(1, n))


_J_V, _J_Q, _J_KV, _J_QI, _J_END = 2, 4, 6, 7, 9


def _proj_kernel(x_ref, shift_ref, scale_ref, wm_ref, wt_ref, wvt_ref, wwt_ref, lng_ref, lnb_ref,
                 u_ref, vn_ref, q_ref, k_ref, v_ref, kb_ref, vt_ref, qi_ref, ki_ref, ki2_ref, wit_ref,
                 h_scr):
    j = pl.program_id(1)

    @pl.when(j == 0)
    def _():
        h = (x_ref[...] * (1.0 + scale_ref[...]) + shift_ref[...]).astype(BF16)
        h_scr[...] = h
        tail = _dot(h, wt_ref[...])
        ki_ref[...] = tail[:, :IDX_DIM]
        ki2_ref[...] = tail.astype(BF16)
        vt = _dot_nt(wvt_ref[...], h).astype(BF16)
        for c in range(vt_ref.shape[0]):
            vt_ref[c] = vt[:, c * 128:(c + 1) * 128]
        wit_ref[...] = _dot_nt(wwt_ref[...], h) * (IDX_HEADS ** -0.5)

    z = _dot(h_scr[...], wm_ref[...])

    @pl.when(j < _J_V)
    def _():
        u_ref[...] = z

    @pl.when(jnp.logical_and(j >= _J_V, j < _J_Q))
    def _():
        for g in range(PROJ_TN // A_CH):
            sl = slice(g * A_CH, (g + 1) * A_CH)
            vn_ref[:, sl] = _layer_norm(z[:, sl], lng_ref[:, sl], lnb_ref[:, sl]).astype(vn_ref.dtype)

    @pl.when(jnp.logical_and(j >= _J_Q, j < _J_KV))
    def _():
        q_ref[...] = z.astype(BF16)

    @pl.when(j == _J_KV)
    def _():
        k_ref[...] = z[:, :KV_WIDTH]
        v_ref[...] = z[:, KV_WIDTH:]
        kb_ref[...] = z[:, :KV_WIDTH].astype(BF16)

    @pl.when(j >= _J_QI)
    def _():
        qi_ref[...] = z.astype(BF16)


def _proj(x, shift, scale, wm, wt, wvt, wwt, lng, lnb, *, tm, vn_dtype):
    n, d = x.shape
    nb, r, _ = shift.shape
    rows_per_mod = n // nb
    assert n % tm == 0 and rows_per_mod % tm == 0 and r in (1, tm)
    tiles_per_mod = rows_per_mod // tm
    tn = PROJ_TN

    def clipj(lo, cnt):
        return lambda i, j: (i, jnp.clip(j - lo, 0, cnt - 1))

    mod_spec = pl.BlockSpec((None, r, d), lambda i, j: (i // tiles_per_mod, 0, 0))
    out_shape = (
        jax.ShapeDtypeStruct((n, A_WIDTH), F32),
        jax.ShapeDtypeStruct((n, A_WIDTH), vn_dtype),
        jax.ShapeDtypeStruct((n, ATT_WIDTH), BF16),
        jax.ShapeDtypeStruct((n, KV_WIDTH), F32),
        jax.ShapeDtypeStruct((n, KV_WIDTH), F32),
        jax.ShapeDtypeStruct((n, KV_WIDTH), BF16),
        jax.ShapeDtypeStruct((n // 128, KV_WIDTH, 128), BF16),
        jax.ShapeDtypeStruct((n, IDX_WIDTH), BF16),
        jax.ShapeDtypeStruct((n, IDX_DIM), F32),
        jax.ShapeDtypeStruct((n, 2 * IDX_DIM), BF16),
        jax.ShapeDtypeStruct((IDX_HEADS, n), F32),
    )
    out_specs = (
        pl.BlockSpec((tm, tn), clipj(0, 2)),
        pl.BlockSpec((tm, tn), clipj(_J_V, 2)),
        pl.BlockSpec((tm, tn), clipj(_J_Q, 2)),
        pl.BlockSpec((tm, KV_WIDTH), lambda i, j: (i, 0)),
        pl.BlockSpec((tm, KV_WIDTH), lambda i, j: (i, 0)),
        pl.BlockSpec((tm, KV_WIDTH), lambda i, j: (i, 0)),
        pl.BlockSpec((tm // 128, KV_WIDTH, 128), lambda i, j: (i, 0, 0)),
        pl.BlockSpec((tm, tn), clipj(_J_QI, 2)),
        pl.BlockSpec((tm, IDX_DIM), lambda i, j: (i, 0)),
        pl.BlockSpec((tm, 2 * IDX_DIM), lambda i, j: (i, 0)),
        pl.BlockSpec((IDX_HEADS, tm), lambda i, j: (0, i)),
    )
    in_specs = [
        pl.BlockSpec((tm, d), lambda i, j: (i, 0)),
        mod_spec, mod_spec,
        pl.BlockSpec((d, tn), lambda i, j: (0, j)),
        pl.BlockSpec((d, 2 * IDX_DIM), lambda i, j: (0, 0)),
        pl.BlockSpec((KV_WIDTH, d), lambda i, j: (0, 0)),
        pl.BlockSpec((IDX_HEADS, d), lambda i, j: (0, 0)),
        pl.BlockSpec((1, tn), lambda i, j: (0, jnp.clip(j - _J_V, 0, 1))),
        pl.BlockSpec((1, tn), lambda i, j: (0, jnp.clip(j - _J_V, 0, 1))),
    ]
    return pl.pallas_call(
        _proj_kernel,
        out_shape=out_shape,
        grid=(n // tm, _J_END),
        in_specs=in_specs,
        out_specs=out_specs,
        scratch_shapes=[pltpu.VMEM((tm, d), BF16)],
        compiler_params=_params("arbitrary", "arbitrary"),
        name="proj",
    )(x, shift, scale, wm, wt, wvt, wwt, lng, lnb)


def _mix_kernel(u_ref, vn_ref, q_ref, qi_ref, wit_ref, k_ref, vt_ref, ki2_ref, ws_ref, bsp_ref,
                o_ref, sc_scr, msk_scr, qim_scr, ab_scr, acc_scr, *, n_sel):
    jq = pl.program_id(1)
    n_kc = jq // (KC // 128) + 1
    n_ac = jq // (AC // 128) + 1
    row = lax.broadcasted_iota(I32, (128, 128), 0)
    col = lax.broadcasted_iota(I32, (128, 128), 1)

    @pl.when(jnp.logical_and(pl.program_id(0) == 0, jq == 0))
    def _():
        d0 = (lax.broadcasted_iota(I32, (AC, 128), 1) - lax.broadcasted_iota(I32, (AC, 128), 0)).astype(F32)
        for h in range(N_HEADS):
            lanes = slice((h % GQA_GROUP) * 128, (h % GQA_GROUP + 1) * 128)
            ab_scr[h // GQA_GROUP, :, lanes] = (ALIBI_SLOPES[h] * LOG2E) * d0

    for p in range(IDX_HEADS // 2):
        pair = qi_ref[:, p * 128:(p + 1) * 128]
        zero = jnp.zeros_like(pair)
        qim_scr[p, 0:128, :] = jnp.where(col < IDX_DIM, pair, zero)
        qim_scr[p, 128:256, :] = jnp.where(col >= IDX_DIM, pair, zero)
    wis = wit_ref[...] * (IDX_DIM ** -0.5)
    key_minus_query = (lax.broadcasted_iota(I32, (KC, 128), 0) - lax.broadcasted_iota(I32, (KC, 128), 1))

    def idx_body(kc, carry):
        base = pl.multiple_of(kc * KC, KC)
        kk = ki2_ref[pl.ds(base, KC), :]
        acc = jnp.zeros((KC, 128), F32)
        for p in range(IDX_HEADS // 2):
            s = _dot_nt(kk, qim_scr[p])
            acc = (acc + jnp.maximum(s[:, :128], 0.0) * wis[2 * p:2 * p + 1, :]
                   + jnp.maximum(s[:, 128:], 0.0) * wis[2 * p + 1:2 * p + 2, :])
        inadmissible = key_minus_query > (jq * 128 - kc * KC)
        sc_scr[pl.ds(base, KC), :] = jnp.where(inadmissible, -jnp.inf, acc)
        return carry

    lax.fori_loop(0, n_kc, idx_body, 0)

    def count(pred):
        def body(kc, c):
            x = sc_scr[pl.ds(pl.multiple_of(kc * KC, KC), KC), :]
            m = jnp.where(pred(x), 1, 0).astype(I32)
            return c + jnp.sum(m.reshape(KC // 8, 8, 128), axis=0)
        c = lax.fori_loop(0, n_kc, body, jnp.zeros((8, 128), I32))
        return jnp.sum(c, axis=0, keepdims=True)

    def bit_body(i, t):
        cand = t ^ lax.shift_left(jnp.int32(1), 31 - i)
        cand_f = _key_to_float(cand)
        return jnp.where(count(lambda x: x >= cand_f) >= n_sel, cand, t)

    thr_key = lax.fori_loop(0, 32, bit_body, jnp.full((1, 128), INT_MIN, I32))
    thr = jnp.where(thr_key == jnp.int32(INT_MIN), -FLT_MAX, _key_to_float(thr_key))
    c_ge = count(lambda x: x >= thr)
    c_gt = count(lambda x: x > thr)
    has_tie_overflow = jnp.max(c_ge) > n_sel

    @pl.when(jnp.logical_not(has_tie_overflow))
    def _():
        def body(kc, carry):
            sl = pl.ds(pl.multiple_of(kc * KC, KC), KC)
            msk_scr[sl, :] = jnp.where(sc_scr[sl, :] >= thr, 0.0, -jnp.inf).astype(F32)
            return carry
        lax.fori_loop(0, n_kc, body, 0)

    @pl.when(has_tie_overflow)
    def _():
        need = (n_sel - c_gt).astype(F32)
        lstrict = jnp.where(col < row, 1.0, 0.0).astype(BF16)

        def body(kt, before):
            sl = pl.ds(pl.multiple_of(kt * 128, 128), 128)
            x = sc_scr[sl, :]
            eq = x == thr
            eqf = jnp.where(eq, 1.0, 0.0).astype(F32)
            rank = before + _dot(lstrict, eqf.astype(BF16))
            sel = jnp.logical_or(x > thr, jnp.logical_and(eq, rank < need))
            msk_scr[sl, :] = jnp.where(sel, 0.0, -jnp.inf).astype(F32)
            return before + jnp.sum(eqf, axis=0, keepdims=True)
        lax.fori_loop(0, n_kc * (KC // 128), body, jnp.zeros((1, 128), F32))

    gw = GQA_GROUP * 128
    lane_head = lax.broadcasted_iota(I32, (1, gw), 1) // 128
    q_grp, slope_vec = [], []
    for g in range(N_KV_HEADS):
        q_grp.append(jnp.concatenate(
            [q_ref[:, (g * GQA_GROUP + h) * HEAD_DIM:(g * GQA_GROUP + h + 1) * HEAD_DIM] for h in range(GQA_GROUP)],
            axis=0))
        sv = jnp.zeros((1, gw), F32)
        for h in range(GQA_GROUP):
            sv = jnp.where(lane_head == h, ALIBI_SLOPES[g * GQA_GROUP + h] * LOG2E, sv)
        slope_vec.append(sv)
        acc_scr[g] = jnp.zeros((HEAD_DIM, gw), F32)

    def att_body(ac, carry):
        base = pl.multiple_of(ac * AC, AC)
        off = (jq * 128 - ac * AC).astype(F32)
        mk = msk_scr[pl.ds(base, AC), :]
        mk4 = jnp.concatenate([mk] * GQA_GROUP, axis=1)
        new = []
        for g in range(N_KV_HEADS):
            m, l = carry[2 * g], carry[2 * g + 1]
            k_c = k_ref[pl.ds(base, AC), g * HEAD_DIM:(g + 1) * HEAD_DIM]
            x = _dot_nt(k_c, q_grp[g]) * (ATT_SCALE * LOG2E) - ab_scr[g] + mk4
            cvec = slope_vec[g] * off
            m_new = jnp.maximum(m, jnp.max(x, axis=0, keepdims=True) - cvec)
            alpha = jnp.exp2(m - m_new)
            p = jnp.exp2(x - (m_new + cvec))
            l_new = alpha * l + jnp.sum(p, axis=0, keepdims=True)
            vt_c = jnp.concatenate(
                [vt_ref[ac * (AC // 128) + t, g * HEAD_DIM:(g + 1) * HEAD_DIM, :] for t in range(AC // 128)],
                axis=1)
            acc_scr[g] = alpha * acc_scr[g] + _dot(vt_c, p.astype(BF16))
            new += [m_new, l_new]
        return tuple(new)

    init = (jnp.full((1, gw), NEG_BIG, F32), jnp.zeros((1, gw), F32)) * N_KV_HEADS
    fin = lax.fori_loop(0, n_ac, att_body, init)
    for g in range(N_KV_HEADS):
        out_t = acc_scr[g] / fin[2 * g + 1]
        for h in range(GQA_GROUP):
            lo = A_WIDTH + (g * GQA_GROUP + h) * HEAD_DIM
            o_ref[:, lo:lo + HEAD_DIM] = out_t[:, h * 128:(h + 1) * 128].T.astype(o_ref.dtype)

    for g in range(A_GROUPS):
        sl = slice(g * A_CH, (g + 1) * A_CH)
        sp = _dot(ws_ref[g], vn_ref[:, sl]) + bsp_ref[:, sl]
        o_ref[:, sl] = (u_ref[:, sl] * sp).astype(o_ref.dtype)


def _mix(u, vn, q, qi, wit, kb, vt, ki2, ws, bsp, *, batch, seq, n_sel):
    n = batch * seq
    nblk = seq // Q_BLOCK
    ntile = seq // 128
    row_spec = lambda w: pl.BlockSpec((Q_BLOCK, w), lambda b, j: (b * nblk + j, 0))
    in_specs = [
        row_spec(A_WIDTH), row_spec(A_WIDTH), row_spec(ATT_WIDTH), row_spec(IDX_WIDTH),
        pl.BlockSpec((IDX_HEADS, Q_BLOCK), lambda b, j: (0, b * nblk + j)),
        pl.BlockSpec((seq, KV_WIDTH), lambda b, j: (b, 0)),
        pl.BlockSpec((ntile, KV_WIDTH, 128), lambda b, j: (b, 0, 0)),
        pl.BlockSpec((seq, 2 * IDX_DIM), lambda b, j: (b, 0)),
        pl.BlockSpec((A_GROUPS, CHUNK, CHUNK), lambda b, j: (0, 0, 0)),
        pl.BlockSpec((CHUNK, A_WIDTH), lambda b, j: (0, 0)),
    ]
    return pl.pallas_call(
        functools.partial(_mix_kernel, n_sel=n_sel),
        out_shape=jax.ShapeDtypeStruct((n, A_WIDTH + ATT_WIDTH), BF16),
        grid=(batch, nblk),
        in_specs=in_specs,
        out_specs=pl.BlockSpec((Q_BLOCK, A_WIDTH + ATT_WIDTH), lambda b, j: (b * nblk + j, 0)),
        scratch_shapes=[pltpu.VMEM((seq, 128), F32),
                        pltpu.VMEM((seq, 128), F32),
                        pltpu.VMEM((IDX_HEADS // 2, 256, 128), BF16),
                        pltpu.VMEM((N_KV_HEADS, AC, GQA_GROUP * 128), F32),
                        pltpu.VMEM((N_KV_HEADS, HEAD_DIM, GQA_GROUP * 128), F32)],
        compiler_params=_params("arbitrary", "arbitrary"),
        name="mix",
    )(u, vn, q, qi, wit, kb, vt, ki2, ws, bsp)


def _s_scores_kernel(pt_ref, qi_ref, wi_ref, kin_ref, *rest, n_pages):
    del pt_ref
    pages = rest[:n_pages]
    sc_ref, self_ref = rest[n_pages], rest[n_pages + 1]
    qi = qi_ref[...]
    w = wi_ref[...] * (IDX_DIM ** -0.5)
    for p in range(n_pages):
        kp = pages[p][...].astype(BF16)
        s = _dot_nt(qi, kp)
        sc_ref[p:p + 1, :] = jnp.sum(jnp.maximum(s, 0.0) * w, axis=0, keepdims=True)
    kin = kin_ref[...].astype(BF16).astype(F32)
    s_self = jnp.sum(qi.astype(F32) * kin, axis=1, keepdims=True)
    v_self = jnp.sum(jnp.maximum(s_self, 0.0) * w, axis=0, keepdims=True)
    self_ref[...] = jnp.broadcast_to(v_self, self_ref.shape)


def _s_scores(page_table_flat, qi3, wi_col, ki_new3, cache_kidx_l, *, n_pages):
    bd = qi3.shape[0]

    def page_spec(p):
        return pl.BlockSpec((None, PAGE_SIZE, IDX_DIM), lambda b, pt, p=p: (pt[b * n_pages + p], 0, 0))

    grid_spec = pltpu.PrefetchScalarGridSpec(
        num_scalar_prefetch=1,
        grid=(bd,),
        in_specs=[pl.BlockSpec((None, IDX_HEADS, IDX_DIM), lambda b, pt: (b, 0, 0)),
                  pl.BlockSpec((None, IDX_HEADS, 1), lambda b, pt: (b, 0, 0)),
                  pl.BlockSpec((None, 1, IDX_DIM), lambda b, pt: (b, 0, 0))]
                 + [page_spec(p) for p in range(n_pages)],
        out_specs=(pl.BlockSpec((None, n_pages, PAGE_SIZE), lambda b, pt: (b, 0, 0)),
                   pl.BlockSpec((None, 1, 128), lambda b, pt: (b, 0, 0))),
    )
    return pl.pallas_call(
        functools.partial(_s_scores_kernel, n_pages=n_pages),
        out_shape=(jax.ShapeDtypeStruct((bd, n_pages, PAGE_SIZE), F32),
                   jax.ShapeDtypeStruct((bd, 1, 128), F32)),
        grid_spec=grid_spec,
        compiler_params=_params("arbitrary"),
        name="s_scores",
    )(page_table_flat, qi3, wi_col, ki_new3, *([cache_kidx_l] * n_pages))


def _s_select_kernel(sc_ref, self_ref, u_ref, vn_ref, ws0_ref, bs0_ref,
                     sel_ref, selself_ref, a_ref, *, n_sel):
    keys = sc_ref[...]
    kself = self_ref[:, 0:1]
    bd, past = keys.shape

    def count_ge(cand):
        c = jnp.sum(jnp.where(keys >= cand, 1, 0).astype(I32), axis=1, keepdims=True)
        return c + jnp.where(kself >= cand, 1, 0).astype(I32)

    def bit_body(i, t):
        cand = t ^ lax.shift_left(jnp.int32(1), 31 - i)
        return jnp.where(count_ge(_key_to_float(cand)) >= n_sel, cand, t)

    thr = _key_to_float(lax.fori_loop(0, 32, bit_body, jnp.full((bd, 1), INT_MIN, I32)))
    c_gt = (jnp.sum(jnp.where(keys > thr, 1, 0).astype(I32), axis=1, keepdims=True)
            + jnp.where(kself > thr, 1, 0).astype(I32))
    need = (n_sel - c_gt).astype(F32)
    r = lax.broadcasted_iota(I32, (128, 128), 0)
    c = lax.broadcasted_iota(I32, (128, 128), 1)
    ustrict = jnp.where(r < c, 1.0, 0.0).astype(BF16)
    before = jnp.zeros((bd, 1), F32)
    for t in range(past // 128):
        kt = keys[:, t * 128:(t + 1) * 128]
        eq = kt == thr
        eqf = jnp.where(eq, 1.0, 0.0).astype(F32)
        rank = before + _dot(eqf.astype(BF16), ustrict)
        sel = jnp.logical_or(kt > thr, jnp.logical_and(eq, rank < need))
        sel_ref[:, t * 128:(t + 1) * 128] = jnp.where(sel, 1.0, 0.0).astype(F32)
        before = before + jnp.sum(eqf, axis=1, keepdims=True)
    sel_self = jnp.logical_or(kself > thr, jnp.logical_and(kself == thr, before < need))
    selself_ref[...] = jnp.broadcast_to(jnp.where(sel_self, 1.0, 0.0).astype(F32), selself_ref.shape)
    sp = ws0_ref[...] * vn_ref[...] + bs0_ref[...]
    a_ref[...] = (u_ref[...] * sp).astype(a_ref.dtype)


def _s_select(sc, sc_self, u, vn, ws0, bs0, *, n_sel):
    bd, past = sc.shape
    full = lambda a: pl.BlockSpec(a.shape, lambda i: (0,) * a.ndim)
    args = (sc, sc_self, u, vn, ws0, bs0)
    return pl.pallas_call(
        functools.partial(_s_select_kernel, n_sel=n_sel),
        out_shape=(jax.ShapeDtypeStruct((bd, past), F32),
                   jax.ShapeDtypeStruct((bd, 128), F32),
                   jax.ShapeDtypeStruct((bd, A_WIDTH), BF16)),
        grid=(1,),
        in_specs=[full(a) for a in args],
        out_specs=(pl.BlockSpec((bd, past), lambda i: (0, 0)),
                   pl.BlockSpec((bd, 128), lambda i: (0, 0)),
                   pl.BlockSpec((bd, A_WIDTH), lambda i: (0, 0))),
        compiler_params=_params("arbitrary"),
        name="s_select",
    )(*args)


def _s_attn_kernel(pt_ref, q_ref, sel_ref, selself_ref, knew_ref, vnew_ref, *rest, n_pages, past):
    del pt_ref
    kpages = rest[:n_pages]
    vpages = rest[n_pages:2 * n_pages]
    o_ref = rest[2 * n_pages]
    q = q_ref[...]
    hrow = lax.broadcasted_iota(I32, (N_HEADS, 2 * PAGE_SIZE), 0)
    ccol = lax.broadcasted_iota(I32, (N_HEADS, 2 * PAGE_SIZE), 1)
    own_kv = (ccol & 1) == (hrow // GQA_GROUP)
    pos_in_page = ccol >> 1
    hcol = lax.broadcasted_iota(I32, (N_HEADS, 1), 0)
    slope = jnp.zeros((N_HEADS, 1), F32)
    for h in range(N_HEADS):
        slope = jnp.where(hcol == h, ALIBI_SLOPES[h], slope)

    logits = []
    for p in range(n_pages):
        kp = kpages[p][...].astype(BF16)
        lg = _dot_nt(q, kp) * ATT_SCALE
        dist = (past - (p * PAGE_SIZE + pos_in_page)).astype(F32)
        lg = lg - slope * dist
        ok = jnp.logical_and(own_kv, sel_ref[p:p + 1, :] > 0.5)
        logits.append(jnp.where(ok, lg, -jnp.inf))
    first_group = lax.broadcasted_iota(I32, (N_HEADS, HEAD_DIM), 0) < GQA_GROUP
    knew = knew_ref[...].astype(BF16).astype(F32)
    vnew = vnew_ref[...].astype(BF16).astype(F32)
    knew8 = jnp.where(first_group, knew[0:1, :], knew[1:2, :])
    vnew8 = jnp.where(first_group, vnew[0:1, :], vnew[1:2, :])
    lg_self = jnp.sum(q.astype(F32) * knew8, axis=1, keepdims=True) * ATT_SCALE
    lg_self = jnp.where(selself_ref[:, 0:1] > 0.5, lg_self, -jnp.inf)

    m = jnp.maximum(lg_self, NEG_BIG)
    for lg in logits:
        m = jnp.maximum(m, jnp.max(lg, axis=1, keepdims=True))
    p_self = jnp.exp(lg_self - m)
    l = p_self
    acc = p_self.astype(BF16).astype(F32) * vnew8
    for p in range(n_pages):
        pp = jnp.exp(logits[p] - m)
        l = l + jnp.sum(pp, axis=1, keepdims=True)
        acc = acc + _dot(pp.astype(BF16), vpages[p][...].astype(BF16))
    o_ref[...] = (acc / l).astype(o_ref.dtype)


def _s_attn(page_table_flat, q3, sel2, sel_self3, k_new3, v_new3, cache_k2, cache_v2, *, n_pages, past):
    bd = q3.shape[0]

    def page_spec(p):
        return pl.BlockSpec((None, 2 * PAGE_SIZE, HEAD_DIM), lambda b, pt, p=p: (pt[b * n_pages + p], 0, 0))

    per_b = lambda s1, s2: pl.BlockSpec((None, s1, s2), lambda b, pt: (b, 0, 0))
    grid_spec = pltpu.PrefetchScalarGridSpec(
        num_scalar_prefetch=1,
        grid=(bd,),
        in_specs=[per_b(N_HEADS, HEAD_DIM), per_b(n_pages, 2 * PAGE_SIZE), per_b(1, 128),
                  per_b(N_KV_HEADS, HEAD_DIM), per_b(N_KV_HEADS, HEAD_DIM)]
                 + [page_spec(p) for p in range(n_pages)] * 2,
        out_specs=per_b(N_HEADS, HEAD_DIM),
    )
    return pl.pallas_call(
        functools.partial(_s_attn_kernel, n_pages=n_pages, past=past),
        out_shape=jax.ShapeDtypeStruct((bd, N_HEADS, HEAD_DIM), BF16),
        grid_spec=grid_spec,
        compiler_params=_params("arbitrary"),
        name="s_attn",
    )(page_table_flat, q3, sel2, sel_self3, k_new3, v_new3, *([cache_k2] * n_pages), *([cache_v2] * n_pages))


def _outln_kernel(mix_ref, x_ref, gate_ref, w_ref, g_ref, b_ref, o_ref, *, alpha):
    mix = _dot(mix_ref[...], w_ref[...])
    y = alpha * x_ref[...] + (1.0 + gate_ref[...]) * mix
    o_ref[...] = _layer_norm(y, g_ref[...], b_ref[...])


def _outln(mixin, x, gate, w, g, b, *, tm, alpha):
    n, d = x.shape
    nb, r, _ = gate.shape
    tiles_per_mod = (n // nb) // tm
    kin = mixin.shape[1]
    return pl.pallas_call(
        functools.partial(_outln_kernel, alpha=alpha),
        out_shape=jax.ShapeDtypeStruct((n, d), F32),
        grid=(n // tm,),
        in_specs=[pl.BlockSpec((tm, kin), lambda i: (i, 0)),
                  pl.BlockSpec((tm, d), lambda i: (i, 0)),
                  pl.BlockSpec((None, r, d), lambda i: (i // tiles_per_mod, 0, 0)),
                  pl.BlockSpec((kin, d), lambda i: (0, 0), pipeline_mode=pl.Buffered(1)),
                  pl.BlockSpec((1, d), lambda i: (0, 0)),
                  pl.BlockSpec((1, d), lambda i: (0, 0))],
        out_specs=pl.BlockSpec((tm, d), lambda i: (i, 0)),
        compiler_params=_params("arbitrary"),
        name="outln",
    )(mixin, x, gate, w, g, b)


def _ffn_kernel(x_ref, shift_ref, scale_ref, gate_ref, w1_ref, b1_ref, w2_ref, b2_ref, g_ref, b_ref,
                o_ref, h_scr, *, alpha):
    f = pl.program_id(1)

    @pl.when(f == 0)
    def _():
        h_scr[...] = (x_ref[...] * (1.0 + scale_ref[...]) + shift_ref[...]).astype(BF16)
        o_ref[...] = jnp.zeros_like(o_ref)

    a = jnp.maximum(_dot(h_scr[...], w1_ref[...]) + b1_ref[...], 0.0)
    a2 = (a * a).astype(BF16)
    for c in range(o_ref.shape[1] // FFN_TN):
        cs = slice(c * FFN_TN, (c + 1) * FFN_TN)
        o_ref[:, cs] += _dot(a2, w2_ref[:, cs])

    @pl.when(f == pl.num_programs(1) - 1)
    def _():
        y = alpha * x_ref[...] + (1.0 + gate_ref[...]) * (o_ref[...] + b2_ref[...])
        o_ref[...] = _layer_norm(y, g_ref[...], b_ref[...])


def _ffn(x, shift, scale, gate, w1, b1, w2, b2, g, b, *, tm, tf, alpha):
    n, d = x.shape
    nb, r, _ = gate.shape
    dff = w1.shape[1]
    tiles_per_mod = (n // nb) // tm
    mod_spec = pl.BlockSpec((None, r, d), lambda i, f: (i // tiles_per_mod, 0, 0))
    return pl.pallas_call(
        functools.partial(_ffn_kernel, alpha=alpha),
        out_shape=jax.ShapeDtypeStruct((n, d), F32),
        grid=(n // tm, dff // tf),
        in_specs=[pl.BlockSpec((tm, d), lambda i, f: (i, 0), pipeline_mode=pl.Buffered(1)),
                  mod_spec, mod_spec, mod_spec,
                  pl.BlockSpec((d, tf), lambda i, f: (0, f)),
                  pl.BlockSpec((1, tf), lambda i, f: (0, f)),
                  pl.BlockSpec((tf, d), lambda i, f: (f, 0)),
                  pl.BlockSpec((1, d), lambda i, f: (0, 0)),
                  pl.BlockSpec((1, d), lambda i, f: (0, 0)),
                  pl.BlockSpec((1, d), lambda i, f: (0, 0))],
        out_specs=pl.BlockSpec((tm, d), lambda i, f: (i, 0)),
        scratch_shapes=[pltpu.VMEM((tm, d), BF16)],
        compiler_params=_params("arbitrary", "arbitrary"),
        name="ffn",
    )(x, shift, scale, gate, w1, b1, w2, b2, g, b)


def kernel(x_prompt, x_sample, cache_k, cache_v, cache_kidx, page_table, c_prompt, c_sample,
           w_cond, b_cond, w_in, ln_v_g, ln_v_b, w_spatial, b_spatial, w_out,
           ln1_g, ln1_b, w_ff1, b_ff1, w_ff2, b_ff2, ln2_g, ln2_b):
    batch, seq, d = x_prompt.shape
    bd, ts, _ = x_sample.shape
    depth = w_in.shape[0]
    n_pages = page_table.shape[1]
    past = n_pages * PAGE_SIZE
    n_pool = cache_k.shape[1]
    dff = w_ff1.shape[2]
    assert ts == 1 and seq % KC == 0 and d == A_WIDTH + ATT_WIDTH
    assert w_in.shape[2] == MAIN_WIDTH + IDX_DIM + IDX_HEADS
    n = batch * seq
    alpha = (2 * depth) ** 0.25
    n_sel_p = min(TOPK_MAX, seq // 4)
    n_sel_s = min(TOPK_MAX, (past + ts) // 4)
    tm_p = min(1024, seq)
    tm_o = min(512, seq)
    tf = min(512, dff)
    pt_flat = page_table.reshape(-1).astype(I32)

    xp = x_prompt.reshape(n, d)
    xs = x_sample.reshape(bd, d)
    c_all = jnp.concatenate([c_prompt, c_sample], axis=0)

    outs = {k: [] for k in ("kp", "vp", "kip", "ks", "vs", "kis", "vc")}
    for l in range(depth):
        w_in_l = w_in[l]
        wm = w_in_l[:, :MAIN_WIDTH].astype(BF16)
        w_ki = w_in_l[:, MAIN_WIDTH:MAIN_WIDTH + IDX_DIM]
        wt = jnp.concatenate([w_ki, w_ki], axis=1).astype(BF16)
        v_lo = 2 * A_WIDTH + ATT_WIDTH + KV_WIDTH
        wvt = w_in_l[:, v_lo:v_lo + KV_WIDTH].T.astype(BF16)
        wwt = w_in_l[:, MAIN_WIDTH + IDX_DIM:].T.astype(BF16)
        lng = ln_v_g[l].reshape(1, A_WIDTH)
        lnb = ln_v_b[l].reshape(1, A_WIDTH)
        tril = jnp.tril(jnp.ones((CHUNK, CHUNK), dtype=bool))
        ws = jnp.where(tril[None], w_spatial[l], 0.0).astype(BF16)
        bsp = jnp.repeat(jnp.transpose(b_spatial[l]), A_CH, axis=1)
        ws0 = jnp.repeat(w_spatial[l][:, 0, 0], A_CH).reshape(1, A_WIDTH)
        bs0 = jnp.repeat(b_spatial[l][:, 0], A_CH).reshape(1, A_WIDTH)
        w_out_b = w_out[l].astype(BF16)
        w1_b = w_ff1[l].astype(BF16)
        w2_b = w_ff2[l].astype(BF16)
        b1 = b_ff1[l].reshape(1, dff)
        b2 = b_ff2[l].reshape(1, d)
        g1, be1 = ln1_g[l].reshape(1, d), ln1_b[l].reshape(1, d)
        g2, be2 = ln2_g[l].reshape(1, d), ln2_b[l].reshape(1, d)

        z = _cond(c_all, w_cond[l], b_cond[l])
        mods = [z[:, i * d:(i + 1) * d] for i in range(N_MOD)]
        mp = [m[:batch].reshape(batch, 1, d) for m in mods]
        ms = [m[batch:].reshape(1, bd, d) for m in mods]

        (u, vn, q, k, v, kb, vt, qi, ki, ki2, wit) = _proj(
            xp, mp[0], mp[1], wm, wt, wvt, wwt, lng, lnb, tm=tm_p, vn_dtype=BF16)
        mixin = _mix(u, vn, q, qi, wit, kb, vt, ki2, ws, bsp, batch=batch, seq=seq, n_sel=n_sel_p)
        x1 = _outln(mixin, xp, mp[2], w_out_b, g1, be1, tm=tm_o, alpha=alpha)
        xp = _ffn(x1, mp[3], mp[4], mp[5], w1_b, b1, w2_b, b2, g2, be2, tm=tm_p, tf=tf, alpha=alpha)
        outs["kp"].append(k.reshape(batch, seq, N_KV_HEADS, HEAD_DIM))
        outs["vp"].append(v.reshape(batch, seq, N_KV_HEADS, HEAD_DIM))
        outs["kip"].append(ki.reshape(batch, seq, IDX_DIM))

        (u, vn, q, k, v, _, _, qi, ki, _, wit) = _proj(
            xs, ms[0], ms[1], wm, wt, wvt, wwt, lng, lnb, tm=bd, vn_dtype=F32)
        sc, sc_self = _s_scores(pt_flat, qi.reshape(bd, IDX_HEADS, IDX_DIM),
                                jnp.transpose(wit).reshape(bd, IDX_HEADS, 1),
                                ki.reshape(bd, 1, IDX_DIM), cache_kidx[l], n_pages=n_pages)
        sel, sel_self, a_out = _s_select(sc.reshape(bd, past), sc_self.reshape(bd, 128), u, vn, ws0, bs0,
                                         n_sel=n_sel_s)
        sel2 = jnp.repeat(sel.reshape(bd, n_pages, PAGE_SIZE), 2, axis=2)
        b_out = _s_attn(pt_flat, q.reshape(bd, N_HEADS, HEAD_DIM), sel2, sel_self.reshape(bd, 1, 128),
                        k.reshape(bd, N_KV_HEADS, HEAD_DIM), v.reshape(bd, N_KV_HEADS, HEAD_DIM),
                        cache_k[l].reshape(n_pool, 2 * PAGE_SIZE, HEAD_DIM),
                        cache_v[l].reshape(n_pool, 2 * PAGE_SIZE, HEAD_DIM),
                        n_pages=n_pages, past=past)
        mixin = jnp.concatenate([a_out, b_out.reshape(bd, ATT_WIDTH)], axis=1)
        x1 = _outln(mixin, xs, ms[2], w_out_b, g1, be1, tm=bd, alpha=alpha)
        xs = _ffn(x1, ms[3], ms[4], ms[5], w1_b, b1, w2_b, b2, g2, be2, tm=bd, tf=tf, alpha=alpha)
        outs["ks"].append(k.reshape(bd, ts, N_KV_HEADS, HEAD_DIM))
        outs["vs"].append(v.reshape(bd, ts, N_KV_HEADS, HEAD_DIM))
        outs["kis"].append(ki.reshape(bd, ts, IDX_DIM))
        outs["vc"].append(vn.reshape(bd, ts, A_GROUPS, A_CH))

    st = lambda name: jnp.stack(outs[name])
    return (xp.reshape(batch, seq, d), xs.reshape(bd, ts, d),
            st("kp"), st("vp"), st("kip"), st("ks"), st("vs"), st("kis"), st("vc"))
```

```python
import functools

import jax
import jax.numpy as jnp
import numpy as np
from jax import lax
from jax.experimental import pallas as pl
from jax.experimental.pallas import tpu as pltpu

F32 = jnp.float32
BF16 = jnp.bfloat16
I32 = jnp.int32

CHUNK = 128
A_GROUPS = 8
A_CH = 128
A_WIDTH = A_GROUPS * A_CH
HEAD_DIM = 128
N_HEADS = 8
N_KV_HEADS = 2
GQA_GROUP = N_HEADS // N_KV_HEADS
ATT_WIDTH = N_HEADS * HEAD_DIM
KV_WIDTH = N_KV_HEADS * HEAD_DIM
IDX_HEADS = 16
IDX_DIM = 64
IDX_WIDTH = IDX_HEADS * IDX_DIM
TOPK_MAX = 256
Q_BLOCK = 128
PAGE_SIZE = 128
N_MOD = 6
LN_EPS = 1e-5
ATT_SCALE = HEAD_DIM ** -0.5
MAIN_WIDTH = 2 * A_WIDTH + ATT_WIDTH + 2 * KV_WIDTH + IDX_WIDTH
PROJ_TN = 512
FFN_TN = 512
ALIBI_SLOPES = tuple(float(2.0 ** (-8.0 * h / N_HEADS)) for h in range(1, N_HEADS + 1))

VMEM_LIMIT_BYTES = 58 * 1024 * 1024
INT_MIN = -(2 ** 31)
NEG_BIG = -1e30
FLT_MAX = float(np.finfo(np.float32).max)
LOG2E = float(np.log2(np.e))
KC = 512
AC = 512

NT_DIMS = (((1,), (1,)), ((), ()))


def _dot(a, b):
    return jnp.dot(a, b, preferred_element_type=F32)


def _dot_nt(a, b):
    return lax.dot_general(a, b, NT_DIMS, preferred_element_type=F32)


def _layer_norm(x, g, b):
    mu = jnp.mean(x, axis=-1, keepdims=True)
    xc = x - mu
    var = jnp.mean(xc * xc, axis=-1, keepdims=True)
    return xc * lax.rsqrt(var + LN_EPS) * g + b


def _key_to_float(key):
    bits = jnp.where(key < 0, key ^ jnp.int32(0x7FFFFFFF), key)
    return pltpu.bitcast(bits, F32)


def _params(*sem):
    return pltpu.CompilerParams(dimension_semantics=sem, vmem_limit_bytes=VMEM_LIMIT_BYTES)


def _cond_kernel(c_ref, w_ref, b_ref, o_ref):
    c = c_ref[...]
    a = (c * jax.nn.sigmoid(c)).astype(BF16)
    o_ref[...] = _dot(a, w_ref[...].astype(BF16)) + b_ref[...]


def _cond(c, w, b):
    m, d = c.shape
    n = w.shape[1]
    tn = 1024
    return pl.pallas_call(
        _cond_kernel,
        out_shape=jax.ShapeDtypeStruct((m, n), F32),
        grid=(n // tn,),
        in_specs=[pl.BlockSpec((m, d), lambda j: (0, 0)),
                  pl.BlockSpec((d, tn), lambda j: (0, j)),
                  pl.BlockSpec((1, tn), lambda j: (0, j))],
        out_specs=pl.BlockSpec((m, tn), lambda j: (0, j)),
        compiler_params=_params("arbitrary"),
        name="cond",
    )(c, w, b.reshape(1, n))


_J_V, _J_Q, _J_KV, _J_QI, _J_END = 2, 4, 6, 7, 9


def _proj_kernel(x_ref, shift_ref, scale_ref, wm_ref, wt_ref, lng_ref, lnb_ref,
                 u_ref, vn_ref, q_ref, k_ref, v_ref, kb_ref, vt_ref, qi_ref, ki_ref, ki2_ref, wit_ref,
                 h_scr):
    j = pl.program_id(1)

    @pl.when(j == 0)
    def _():
        h = (x_ref[...] * (1.0 + scale_ref[...]) + shift_ref[...]).astype(BF16)
        h_scr[...] = h
        tail = _dot_nt(h, wt_ref[...].astype(BF16))
        ki_ref[...] = tail[:, :IDX_DIM]
        lane = lax.broadcasted_iota(I32, tail.shape, 1)
        ki2_ref[...] = jnp.where(lane < IDX_DIM, tail, pltpu.roll(tail, IDX_DIM, axis=1)).astype(BF16)
        wit_ref[...] = tail.T[IDX_DIM:IDX_DIM + IDX_HEADS, :] * (IDX_HEADS ** -0.5)

    z = _dot_nt(h_scr[...], wm_ref[...].astype(BF16))

    @pl.when(j < _J_V)
    def _():
        u_ref[...] = z

    @pl.when(jnp.logical_and(j >= _J_V, j < _J_Q))
    def _():
        for g in range(PROJ_TN // A_CH):
            sl = slice(g * A_CH, (g + 1) * A_CH)
            vn_ref[:, sl] = _layer_norm(z[:, sl], lng_ref[:, sl], lnb_ref[:, sl]).astype(vn_ref.dtype)

    @pl.when(jnp.logical_and(j >= _J_Q, j < _J_KV))
    def _():
        q_ref[...] = z.astype(BF16)

    @pl.when(j == _J_KV)
    def _():
        k_ref[...] = z[:, :KV_WIDTH]
        v_ref[...] = z[:, KV_WIDTH:]
        kb_ref[...] = z[:, :KV_WIDTH].astype(BF16)
        vt = z[:, KV_WIDTH:].T.astype(BF16)
        for c in range(vt_ref.shape[0]):
            vt_ref[c] = vt[:, c * 128:(c + 1) * 128]

    @pl.when(j >= _J_QI)
    def _():
        qi_ref[...] = z.astype(BF16)


def _proj(x, shift, scale, wm, wt, lng, lnb, *, tm, vn_dtype):
    n, d = x.shape
    nb, r, _ = shift.shape
    rows_per_mod = n // nb
    assert n % tm == 0 and rows_per_mod % tm == 0 and r in (1, tm)
    tiles_per_mod = rows_per_mod // tm
    tn = PROJ_TN

    def clipj(lo, cnt):
        return lambda i, j: (i, jnp.clip(j - lo, 0, cnt - 1))

    mod_spec = pl.BlockSpec((None, r, d), lambda i, j: (i // tiles_per_mod, 0, 0))
    out_shape = (
        jax.ShapeDtypeStruct((n, A_WIDTH), F32),
        jax.ShapeDtypeStruct((n, A_WIDTH), vn_dtype),
        jax.ShapeDtypeStruct((n, ATT_WIDTH), BF16),
        jax.ShapeDtypeStruct((n, KV_WIDTH), F32),
        jax.ShapeDtypeStruct((n, KV_WIDTH), F32),
        jax.ShapeDtypeStruct((n, KV_WIDTH), BF16),
        jax.ShapeDtypeStruct((n // 128, KV_WIDTH, 128), BF16),
        jax.ShapeDtypeStruct((n, IDX_WIDTH), BF16),
        jax.ShapeDtypeStruct((n, IDX_DIM), F32),
        jax.ShapeDtypeStruct((n, 2 * IDX_DIM), BF16),
        jax.ShapeDtypeStruct((IDX_HEADS, n), F32),
    )
    out_specs = (
        pl.BlockSpec((tm, tn), clipj(0, 2)),
        pl.BlockSpec((tm, tn), clipj(_J_V, 2)),
        pl.BlockSpec((tm, tn), clipj(_J_Q, 2)),
        pl.BlockSpec((tm, KV_WIDTH), lambda i, j: (i, 0)),
        pl.BlockSpec((tm, KV_WIDTH), lambda i, j: (i, 0)),
        pl.BlockSpec((tm, KV_WIDTH), lambda i, j: (i, 0)),
        pl.BlockSpec((tm // 128, KV_WIDTH, 128), lambda i, j: (i, 0, 0)),
        pl.BlockSpec((tm, tn), clipj(_J_QI, 2)),
        pl.BlockSpec((tm, IDX_DIM), lambda i, j: (i, 0)),
        pl.BlockSpec((tm, 2 * IDX_DIM), lambda i, j: (i, 0)),
        pl.BlockSpec((IDX_HEADS, tm), lambda i, j: (0, i)),
    )
    in_specs = [
        pl.BlockSpec((tm, d), lambda i, j: (i, 0)),
        mod_spec, mod_spec,
        pl.BlockSpec((tn, d), lambda i, j: (j, 0)),
        pl.BlockSpec((2 * IDX_DIM, d), lambda i, j: (0, 0)),
        pl.BlockSpec((1, tn), lambda i, j: (0, jnp.clip(j - _J_V, 0, 1))),
        pl.BlockSpec((1, tn), lambda i, j: (0, jnp.clip(j - _J_V, 0, 1))),
    ]
    return pl.pallas_call(
        _proj_kernel,
        out_shape=out_shape,
        grid=(n // tm, _J_END),
        in_specs=in_specs,
        out_specs=out_specs,
        scratch_shapes=[pltpu.VMEM((tm, d), BF16)],
        compiler_params=_params("arbitrary", "arbitrary"),
        name="proj",
    )(x, shift, scale, wm, wt, lng, lnb)


def _mix_kernel(u_ref, vn_ref, q_ref, qi_ref, wit_ref, k_ref, vt_ref, ki2_ref, ws_ref, bsp_ref,
                o_ref, sc_scr, msk_scr, qim_scr, ab_scr, acc_scr, *, n_sel):
    jq = pl.program_id(1)
    n_kc = jq // (KC // 128) + 1
    n_ac = jq // (AC // 128) + 1
    row = lax.broadcasted_iota(I32, (128, 128), 0)
    col = lax.broadcasted_iota(I32, (128, 128), 1)

    @pl.when(jnp.logical_and(pl.program_id(0) == 0, jq == 0))
    def _():
        d0 = (lax.broadcasted_iota(I32, (AC, 128), 1) - lax.broadcasted_iota(I32, (AC, 128), 0)).astype(F32)
        for h in range(N_HEADS):
            lanes = slice((h % GQA_GROUP) * 128, (h % GQA_GROUP + 1) * 128)
            ab_scr[h // GQA_GROUP, :, lanes] = (ALIBI_SLOPES[h] * LOG2E) * d0

    for p in range(IDX_HEADS // 2):
        pair = qi_ref[:, p * 128:(p + 1) * 128]
        zero = jnp.zeros_like(pair)
        qim_scr[p, 0:128, :] = jnp.where(col < IDX_DIM, pair, zero)
        qim_scr[p, 128:256, :] = jnp.where(col >= IDX_DIM, pair, zero)
    wis = wit_ref[...] * (IDX_DIM ** -0.5)
    key_minus_query = (lax.broadcasted_iota(I32, (KC, 128), 0) - lax.broadcasted_iota(I32, (KC, 128), 1))

    def idx_body(kc, carry):
        base = pl.multiple_of(kc * KC, KC)
        kk = ki2_ref[pl.ds(base, KC), :]
        acc = jnp.zeros((KC, 128), F32)
        for p in range(IDX_HEADS // 2):
            s = _dot_nt(kk, qim_scr[p])
            acc = (acc + jnp.maximum(s[:, :128], 0.0) * wis[2 * p:2 * p + 1, :]
                   + jnp.maximum(s[:, 128:], 0.0) * wis[2 * p + 1:2 * p + 2, :])
        inadmissible = key_minus_query > (jq * 128 - kc * KC)
        sc_scr[pl.ds(base, KC), :] = jnp.where(inadmissible, -jnp.inf, acc)
        return carry

    lax.fori_loop(0, n_kc, idx_body, 0)

    def count(pred):
        def body(kc, c):
            x = sc_scr[pl.ds(pl.multiple_of(kc * KC, KC), KC), :]
            m = jnp.where(pred(x), 1, 0).astype(I32)
            return c + jnp.sum(m.reshape(KC // 8, 8, 128), axis=0)
        c = lax.fori_loop(0, n_kc, body, jnp.zeros((8, 128), I32))
        return jnp.sum(c, axis=0, keepdims=True)

    def bit_body(i, t):
        cand = t ^ lax.shift_left(jnp.int32(1), 31 - i)
        cand_f = _key_to_float(cand)
        return jnp.where(count(lambda x: x >= cand_f) >= n_sel, cand, t)

    thr_key = lax.fori_loop(0, 32, bit_body, jnp.full((1, 128), INT_MIN, I32))
    thr = jnp.where(thr_key == jnp.int32(INT_MIN), -FLT_MAX, _key_to_float(thr_key))
    c_ge = count(lambda x: x >= thr)
    c_gt = count(lambda x: x > thr)
    has_tie_overflow = jnp.max(c_ge) > n_sel

    @pl.when(jnp.logical_not(has_tie_overflow))
    def _():
        def body(kc, carry):
            sl = pl.ds(pl.multiple_of(kc * KC, KC), KC)
            msk_scr[sl, :] = jnp.where(sc_scr[sl, :] >= thr, 0.0, -jnp.inf).astype(F32)
            return carry
        lax.fori_loop(0, n_kc, body, 0)

    @pl.when(has_tie_overflow)
    def _():
        need = (n_sel - c_gt).astype(F32)
        lstrict = jnp.where(col < row, 1.0, 0.0).astype(BF16)

        def body(kt, before):
            sl = pl.ds(pl.multiple_of(kt * 128, 128), 128)
            x = sc_scr[sl, :]
            eq = x == thr
            eqf = jnp.where(eq, 1.0, 0.0).astype(F32)
            rank = before + _dot(lstrict, eqf.astype(BF16))
            sel = jnp.logical_or(x > thr, jnp.logical_and(eq, rank < need))
            msk_scr[sl, :] = jnp.where(sel, 0.0, -jnp.inf).astype(F32)
            return before + jnp.sum(eqf, axis=0, keepdims=True)
        lax.fori_loop(0, n_kc * (KC // 128), body, jnp.zeros((1, 128), F32))

    gw = GQA_GROUP * 128
    lane_head = lax.broadcasted_iota(I32, (1, gw), 1) // 128
    q_grp, slope_vec = [], []
    for g in range(N_KV_HEADS):
        q_grp.append(jnp.concatenate(
            [q_ref[:, (g * GQA_GROUP + h) * HEAD_DIM:(g * GQA_GROUP + h + 1) * HEAD_DIM] for h in range(GQA_GROUP)],
            axis=0))
        sv = jnp.zeros((1, gw), F32)
        for h in range(GQA_GROUP):
            sv = jnp.where(lane_head == h, ALIBI_SLOPES[g * GQA_GROUP + h] * LOG2E, sv)
        slope_vec.append(sv)
        acc_scr[g] = jnp.zeros((HEAD_DIM, gw), F32)

    def att_body(ac, carry):
        base = pl.multiple_of(ac * AC, AC)
        off = (jq * 128 - ac * AC).astype(F32)
        mk = msk_scr[pl.ds(base, AC), :]
        mk4 = jnp.concatenate([mk] * GQA_GROUP, axis=1)
        new = []
        for g in range(N_KV_HEADS):
            m, l = carry[2 * g], carry[2 * g + 1]
            k_c = k_ref[pl.ds(base, AC), g * HEAD_DIM:(g + 1) * HEAD_DIM]
            x = _dot_nt(k_c, q_grp[g]) * (ATT_SCALE * LOG2E) - ab_scr[g] + mk4
            cvec = slope_vec[g] * off
            m_new = jnp.maximum(m, jnp.max(x, axis=0, keepdims=True) - cvec)
            alpha = jnp.exp2(m - m_new)
            p = jnp.exp2(x - (m_new + cvec))
            l_new = alpha * l + jnp.sum(p, axis=0, keepdims=True)
            vt_c = jnp.concatenate(
                [vt_ref[ac * (AC // 128) + t, g * HEAD_DIM:(g + 1) * HEAD_DIM, :] for t in range(AC // 128)],
                axis=1)
            acc_scr[g] = alpha * acc_scr[g] + _dot(vt_c, p.astype(BF16))
            new += [m_new, l_new]
        return tuple(new)

    init = (jnp.full((1, gw), NEG_BIG, F32), jnp.zeros((1, gw), F32)) * N_KV_HEADS
    fin = lax.fori_loop(0, n_ac, att_body, init)
    for g in range(N_KV_HEADS):
        out_t = acc_scr[g] / fin[2 * g + 1]
        for h in range(GQA_GROUP):
            lo = A_WIDTH + (g * GQA_GROUP + h) * HEAD_DIM
            o_ref[:, lo:lo + HEAD_DIM] = out_t[:, h * 128:(h + 1) * 128].T.astype(o_ref.dtype)

    for g in range(A_GROUPS):
        sl = slice(g * A_CH, (g + 1) * A_CH)
        sp = _dot(ws_ref[g], vn_ref[:, sl]) + bsp_ref[:, sl]
        o_ref[:, sl] = (u_ref[:, sl] * sp).astype(o_ref.dtype)


def _mix(u, vn, q, qi, wit, kb, vt, ki2, ws, bsp, *, batch, seq, n_sel):
    n = batch * seq
    nblk = seq // Q_BLOCK
    ntile = seq // 128
    row_spec = lambda w: pl.BlockSpec((Q_BLOCK, w), lambda b, j: (b * nblk + j, 0))
    in_specs = [
        row_spec(A_WIDTH), row_spec(A_WIDTH), row_spec(ATT_WIDTH), row_spec(IDX_WIDTH),
        pl.BlockSpec((IDX_HEADS, Q_BLOCK), lambda b, j: (0, b * nblk + j)),
        pl.BlockSpec((seq, KV_WIDTH), lambda b, j: (b, 0)),
        pl.BlockSpec((ntile, KV_WIDTH, 128), lambda b, j: (b, 0, 0)),
        pl.BlockSpec((seq, 2 * IDX_DIM), lambda b, j: (b, 0)),
        pl.BlockSpec((A_GROUPS, CHUNK, CHUNK), lambda b, j: (0, 0, 0)),
        pl.BlockSpec((CHUNK, A_WIDTH), lambda b, j: (0, 0)),
    ]
    return pl.pallas_call(
        functools.partial(_mix_kernel, n_sel=n_sel),
        out_shape=jax.ShapeDtypeStruct((n, A_WIDTH + ATT_WIDTH), BF16),
        grid=(batch, nblk),
        in_specs=in_specs,
        out_specs=pl.BlockSpec((Q_BLOCK, A_WIDTH + ATT_WIDTH), lambda b, j: (b * nblk + j, 0)),
        scratch_shapes=[pltpu.VMEM((seq, 128), F32),
                        pltpu.VMEM((seq, 128), F32),
                        pltpu.VMEM((IDX_HEADS // 2, 256, 128), BF16),
                        pltpu.VMEM((N_KV_HEADS, AC, GQA_GROUP * 128), F32),
                        pltpu.VMEM((N_KV_HEADS, HEAD_DIM, GQA_GROUP * 128), F32)],
        compiler_params=_params("arbitrary", "arbitrary"),
        name="mix",
    )(u, vn, q, qi, wit, kb, vt, ki2, ws, bsp)


def _s_scores_kernel(pt_ref, qi_ref, wi_ref, kin_ref, *rest, n_pages):
    del pt_ref
    pages = rest[:n_pages]
    sc_ref, self_ref = rest[n_pages], rest[n_pages + 1]
    qi = qi_ref[...]
    w = wi_ref[...] * (IDX_DIM ** -0.5)
    kcat = jnp.concatenate([pages[p][...] for p in range(n_pages)], axis=1).astype(BF16)
    s = _dot(qi, kcat)
    sc_ref[...] = jnp.sum(jnp.maximum(s, 0.0) * w, axis=0, keepdims=True)
    kin = kin_ref[...].astype(BF16).astype(F32)
    s_self = jnp.sum(qi.astype(F32) * kin, axis=1, keepdims=True)
    v_self = jnp.sum(jnp.maximum(s_self, 0.0) * w, axis=0, keepdims=True)
    self_ref[...] = jnp.broadcast_to(v_self, self_ref.shape)


def _s_scores(page_table_flat, qi3, wi_col, ki_new3, cache_kidx_l, *, n_pages):
    bd = qi3.shape[0]

    def page_spec(p):
        return pl.BlockSpec((None, IDX_DIM, PAGE_SIZE), lambda b, pt, p=p: (pt[b * n_pages + p], 0, 0))

    past = n_pages * PAGE_SIZE
    grid_spec = pltpu.PrefetchScalarGridSpec(
        num_scalar_prefetch=1,
        grid=(bd,),
        in_specs=[pl.BlockSpec((None, IDX_HEADS, IDX_DIM), lambda b, pt: (b, 0, 0)),
                  pl.BlockSpec((None, IDX_HEADS, 1), lambda b, pt: (b, 0, 0)),
                  pl.BlockSpec((None, 1, IDX_DIM), lambda b, pt: (b, 0, 0))]
                 + [page_spec(p) for p in range(n_pages)],
        out_specs=(pl.BlockSpec((None, 1, past), lambda b, pt: (b, 0, 0)),
                   pl.BlockSpec((None, 1, 128), lambda b, pt: (b, 0, 0))),
    )
    return pl.pallas_call(
        functools.partial(_s_scores_kernel, n_pages=n_pages),
        out_shape=(jax.ShapeDtypeStruct((bd, 1, past), F32),
                   jax.ShapeDtypeStruct((bd, 1, 128), F32)),
        grid_spec=grid_spec,
        compiler_params=_params("arbitrary"),
        name="s_scores",
    )(page_table_flat, qi3, wi_col, ki_new3, *([cache_kidx_l] * n_pages))


def _s_select_kernel(sc_ref, self_ref, u_ref, vn_ref, ws0_ref, bs0_ref,
                     sel_ref, selself_ref, a_ref, *, n_sel):
    keys = sc_ref[...]
    kself = self_ref[:, 0:1]
    bd, past = keys.shape

    def count_ge(cand):
        c = jnp.sum(jnp.where(keys >= cand, 1, 0).astype(I32), axis=1, keepdims=True)
        return c + jnp.where(kself >= cand, 1, 0).astype(I32)

    def bit_body(i, t):
        cand = t ^ lax.shift_left(jnp.int32(1), 31 - i)
        return jnp.where(count_ge(_key_to_float(cand)) >= n_sel, cand, t)

    thr = _key_to_float(lax.fori_loop(0, 32, bit_body, jnp.full((bd, 1), INT_MIN, I32)))
    c_gt = (jnp.sum(jnp.where(keys > thr, 1, 0).astype(I32), axis=1, keepdims=True)
            + jnp.where(kself > thr, 1, 0).astype(I32))
    need = (n_sel - c_gt).astype(F32)
    r = lax.broadcasted_iota(I32, (128, 128), 0)
    c = lax.broadcasted_iota(I32, (128, 128), 1)
    ustrict = jnp.where(r < c, 1.0, 0.0).astype(BF16)
    before = jnp.zeros((bd, 1), F32)
    for t in range(past // 128):
        kt = keys[:, t * 128:(t + 1) * 128]
        eq = kt == thr
        eqf = jnp.where(eq, 1.0, 0.0).astype(F32)
        rank = before + _dot(eqf.astype(BF16), ustrict)
        sel = jnp.logical_or(kt > thr, jnp.logical_and(eq, rank < need))
        sel_ref[:, t * 128:(t + 1) * 128] = jnp.where(sel, 1.0, 0.0).astype(F32)
        before = before + jnp.sum(eqf, axis=1, keepdims=True)
    sel_self = jnp.logical_or(kself > thr, jnp.logical_and(kself == thr, before < need))
    selself_ref[...] = jnp.broadcast_to(jnp.where(sel_self, 1.0, 0.0).astype(F32), selself_ref.shape)
    sp = ws0_ref[...] * vn_ref[...] + bs0_ref[...]
    a_ref[...] = (u_ref[...] * sp).astype(a_ref.dtype)


def _s_select(sc, sc_self, u, vn, ws0, bs0, *, n_sel):
    bd, past = sc.shape
    full = lambda a: pl.BlockSpec(a.shape, lambda i: (0,) * a.ndim)
    args = (sc, sc_self, u, vn, ws0, bs0)
    return pl.pallas_call(
        functools.partial(_s_select_kernel, n_sel=n_sel),
        out_shape=(jax.ShapeDtypeStruct((bd, past), F32),
                   jax.ShapeDtypeStruct((bd, 128), F32),
                   jax.ShapeDtypeStruct((bd, A_WIDTH), BF16)),
        grid=(1,),
        in_specs=[full(a) for a in args],
        out_specs=(pl.BlockSpec((bd, past), lambda i: (0, 0)),
                   pl.BlockSpec((bd, 128), lambda i: (0, 0)),
                   pl.BlockSpec((bd, A_WIDTH), lambda i: (0, 0))),
        compiler_params=_params("arbitrary"),
        name="s_select",
    )(*args)


def _s_attn_kernel(pt_ref, q_ref, sel_ref, selself_ref, knew_ref, vnew_ref, *rest, n_pages, past):
    del pt_ref
    kpages = rest[:n_pages]
    vpages = rest[n_pages:2 * n_pages]
    o_ref = rest[2 * n_pages]
    q = q_ref[...]
    hrow = lax.broadcasted_iota(I32, (N_HEADS, 2 * PAGE_SIZE), 0)
    ccol = lax.broadcasted_iota(I32, (N_HEADS, 2 * PAGE_SIZE), 1)
    own_kv = (ccol & 1) == (hrow // GQA_GROUP)
    pos_in_page = ccol >> 1
    hcol = lax.broadcasted_iota(I32, (N_HEADS, 1), 0)
    slope = jnp.zeros((N_HEADS, 1), F32)
    for h in range(N_HEADS):
        slope = jnp.where(hcol == h, ALIBI_SLOPES[h], slope)

    logits = []
    for p in range(n_pages):
        kp = kpages[p][...].astype(BF16)
        lg = _dot_nt(q, kp) * ATT_SCALE
        dist = (past - (p * PAGE_SIZE + pos_in_page)).astype(F32)
        lg = lg - slope * dist
        ok = jnp.logical_and(own_kv, sel_ref[p:p + 1, :] > 0.5)
        logits.append(jnp.where(ok, lg, -jnp.inf))
    first_group = lax.broadcasted_iota(I32, (N_HEADS, HEAD_DIM), 0) < GQA_GROUP
    knew = knew_ref[...].astype(BF16).astype(F32)
    vnew = vnew_ref[...].astype(BF16).astype(F32)
    knew8 = jnp.where(first_group, knew[0:1, :], knew[1:2, :])
    vnew8 = jnp.where(first_group, vnew[0:1, :], vnew[1:2, :])
    lg_self = jnp.sum(q.astype(F32) * knew8, axis=1, keepdims=True) * ATT_SCALE
    lg_self = jnp.where(selself_ref[:, 0:1] > 0.5, lg_self, -jnp.inf)

    m = jnp.maximum(lg_self, NEG_BIG)
    for lg in logits:
        m = jnp.maximum(m, jnp.max(lg, axis=1, keepdims=True))
    p_self = jnp.exp(lg_self - m)
    l = p_self
    acc = p_self.astype(BF16).astype(F32) * vnew8
    for p in range(n_pages):
        pp = jnp.exp(logits[p] - m)
        l = l + jnp.sum(pp, axis=1, keepdims=True)
        acc = acc + _dot(pp.astype(BF16), vpages[p][...].astype(BF16))
    o_ref[...] = (acc / l).astype(o_ref.dtype)


def _s_attn(page_table_flat, q3, sel2, sel_self3, k_new3, v_new3, cache_k2, cache_v2, *, n_pages, past):
    bd = q3.shape[0]

    def page_spec(p):
        return pl.BlockSpec((None, 2 * PAGE_SIZE, HEAD_DIM), lambda b, pt, p=p: (pt[b * n_pages + p], 0, 0))

    per_b = lambda s1, s2: pl.BlockSpec((None, s1, s2), lambda b, pt: (b, 0, 0))
    grid_spec = pltpu.PrefetchScalarGridSpec(
        num_scalar_prefetch=1,
        grid=(bd,),
        in_specs=[per_b(N_HEADS, HEAD_DIM), per_b(n_pages, 2 * PAGE_SIZE), per_b(1, 128),
                  per_b(N_KV_HEADS, HEAD_DIM), per_b(N_KV_HEADS, HEAD_DIM)]
                 + [page_spec(p) for p in range(n_pages)] * 2,
        out_specs=per_b(N_HEADS, HEAD_DIM),
    )
    return pl.pallas_call(
        functools.partial(_s_attn_kernel, n_pages=n_pages, past=past),
        out_shape=jax.ShapeDtypeStruct((bd, N_HEADS, HEAD_DIM), BF16),
        grid_spec=grid_spec,
        compiler_params=_params("arbitrary"),
        name="s_attn",
    )(page_table_flat, q3, sel2, sel_self3, k_new3, v_new3, *([cache_k2] * n_pages), *([cache_v2] * n_pages))


def _outln_kernel(mix_ref, x_ref, gate_ref, w_ref, g_ref, b_ref, o_ref, *, alpha):
    mix = _dot(mix_ref[...], w_ref[...])
    y = alpha * x_ref[...] + (1.0 + gate_ref[...]) * mix
    o_ref[...] = _layer_norm(y, g_ref[...], b_ref[...])


def _outln(mixin, x, gate, w, g, b, *, tm, alpha):
    n, d = x.shape
    nb, r, _ = gate.shape
    tiles_per_mod = (n // nb) // tm
    kin = mixin.shape[1]
    return pl.pallas_call(
        functools.partial(_outln_kernel, alpha=alpha),
        out_shape=jax.ShapeDtypeStruct((n, d), F32),
        grid=(n // tm,),
        in_specs=[pl.BlockSpec((tm, kin), lambda i: (i, 0)),
                  pl.BlockSpec((tm, d), lambda i: (i, 0)),
                  pl.BlockSpec((None, r, d), lambda i: (i // tiles_per_mod, 0, 0)),
                  pl.BlockSpec((kin, d), lambda i: (0, 0), pipeline_mode=pl.Buffered(1)),
                  pl.BlockSpec((1, d), lambda i: (0, 0)),
                  pl.BlockSpec((1, d), lambda i: (0, 0))],
        out_specs=pl.BlockSpec((tm, d), lambda i: (i, 0)),
        compiler_params=_params("arbitrary"),
        name="outln",
    )(mixin, x, gate, w, g, b)


def _ffn_kernel(x_ref, shift_ref, scale_ref, gate_ref, w1_ref, b1_ref, w2_ref, b2_ref, g_ref, b_ref,
                o_ref, h_scr, *, alpha):
    f = pl.program_id(1)

    @pl.when(f == 0)
    def _():
        h_scr[...] = (x_ref[...] * (1.0 + scale_ref[...]) + shift_ref[...]).astype(BF16)
        o_ref[...] = jnp.zeros_like(o_ref)

    a = jnp.maximum(_dot(h_scr[...], w1_ref[...]) + b1_ref[...], 0.0)
    a2 = (a * a).astype(BF16)
    for c in range(o_ref.shape[1] // FFN_TN):
        cs = slice(c * FFN_TN, (c + 1) * FFN_TN)
        o_ref[:, cs] += _dot(a2, w2_ref[:, cs])

    @pl.when(f == pl.num_programs(1) - 1)
    def _():
        y = alpha * x_ref[...] + (1.0 + gate_ref[...]) * (o_ref[...] + b2_ref[...])
        o_ref[...] = _layer_norm(y, g_ref[...], b_ref[...])


def _ffn(x, shift, scale, gate, w1, b1, w2, b2, g, b, *, tm, tf, alpha):
    n, d = x.shape
    nb, r, _ = gate.shape
    dff = w1.shape[1]
    tiles_per_mod = (n // nb) // tm
    mod_spec = pl.BlockSpec((None, r, d), lambda i, f: (i // tiles_per_mod, 0, 0))
    return pl.pallas_call(
        functools.partial(_ffn_kernel, alpha=alpha),
        out_shape=jax.ShapeDtypeStruct((n, d), F32),
        grid=(n // tm, dff // tf),
        in_specs=[pl.BlockSpec((tm, d), lambda i, f: (i, 0), pipeline_mode=pl.Buffered(1)),
                  mod_spec, mod_spec, mod_spec,
                  pl.BlockSpec((d, tf), lambda i, f: (0, f)),
                  pl.BlockSpec((1, tf), lambda i, f: (0, f)),
                  pl.BlockSpec((tf, d), lambda i, f: (f, 0)),
                  pl.BlockSpec((1, d), lambda i, f: (0, 0)),
                  pl.BlockSpec((1, d), lambda i, f: (0, 0)),
                  pl.BlockSpec((1, d), lambda i, f: (0, 0))],
        out_specs=pl.BlockSpec((tm, d), lambda i, f: (i, 0)),
        scratch_shapes=[pltpu.VMEM((tm, d), BF16)],
        compiler_params=_params("arbitrary", "arbitrary"),
        name="ffn",
    )(x, shift, scale, gate, w1, b1, w2, b2, g, b)


def kernel(x_prompt, x_sample, cache_k, cache_v, cache_kidx, page_table, c_prompt, c_sample,
           w_cond, b_cond, w_in, ln_v_g, ln_v_b, w_spatial, b_spatial, w_out,
           ln1_g, ln1_b, w_ff1, b_ff1, w_ff2, b_ff2, ln2_g, ln2_b):
    batch, seq, d = x_prompt.shape
    bd, ts, _ = x_sample.shape
    depth = w_in.shape[0]
    n_pages = page_table.shape[1]
    past = n_pages * PAGE_SIZE
    n_pool = cache_k.shape[1]
    dff = w_ff1.shape[2]
    assert ts == 1 and seq % KC == 0 and d == A_WIDTH + ATT_WIDTH
    assert w_in.shape[2] == MAIN_WIDTH + IDX_DIM + IDX_HEADS
    n = batch * seq
    alpha = (2 * depth) ** 0.25
    n_sel_p = min(TOPK_MAX, seq // 4)
    n_sel_s = min(TOPK_MAX, (past + ts) // 4)
    tm_p = min(1024, seq)
    tm_o = min(512, seq)
    tf = min(1024, dff)
    pt_flat = page_table.reshape(-1).astype(I32)

    xp = x_prompt.reshape(n, d)
    xs = x_sample.reshape(bd, d)
    c_all = jnp.concatenate([c_prompt, c_sample], axis=0)

    outs = {k: [] for k in ("kp", "vp", "kip", "ks", "vs", "kis", "vc")}
    for l in range(depth):
        wm = jnp.swapaxes(w_in[l], 0, 1)
        wt = jnp.pad(wm[MAIN_WIDTH:], ((0, 2 * IDX_DIM - IDX_DIM - IDX_HEADS), (0, 0)))
        kidx_t = jnp.swapaxes(cache_kidx[l], 1, 2)
        lng = ln_v_g[l].reshape(1, A_WIDTH)
        lnb = ln_v_b[l].reshape(1, A_WIDTH)
        tril = jnp.tril(jnp.ones((CHUNK, CHUNK), dtype=bool))
        ws = jnp.where(tril[None], w_spatial[l], 0.0).astype(BF16)
        bsp = jnp.repeat(jnp.transpose(b_spatial[l]), A_CH, axis=1)
        ws0 = jnp.repeat(w_spatial[l][:, 0, 0], A_CH).reshape(1, A_WIDTH)
        bs0 = jnp.repeat(b_spatial[l][:, 0], A_CH).reshape(1, A_WIDTH)
        w_out_b = w_out[l].astype(BF16)
        w1_b = w_ff1[l].astype(BF16)
        w2_b = w_ff2[l].astype(BF16)
        b1 = b_ff1[l].reshape(1, dff)
        b2 = b_ff2[l].reshape(1, d)
        g1, be1 = ln1_g[l].reshape(1, d), ln1_b[l].reshape(1, d)
        g2, be2 = ln2_g[l].reshape(1, d), ln2_b[l].reshape(1, d)

        z = _cond(c_all, w_cond[l], b_cond[l])
        mods = [z[:, i * d:(i + 1) * d] for i in range(N_MOD)]
        mp = [m[:batch].reshape(batch, 1, d) for m in mods]
        ms = [m[batch:].reshape(1, bd, d) for m in mods]

        (u, vn, q, k, v, kb, vt, qi, ki, ki2, wit) = _proj(
            xp, mp[0], mp[1], wm, wt, lng, lnb, tm=tm_p, vn_dtype=BF16)
        mixin = _mix(u, vn, q, qi, wit, kb, vt, ki2, ws, bsp, batch=batch, seq=seq, n_sel=n_sel_p)
        x1 = _outln(mixin, xp, mp[2], w_out_b, g1, be1, tm=tm_o, alpha=alpha)
        xp = _ffn(x1, mp[3], mp[4], mp[5], w1_b, b1, w2_b, b2, g2, be2, tm=tm_p, tf=tf, alpha=alpha)
        outs["kp"].append(k.reshape(batch, seq, N_KV_HEADS, HEAD_DIM))
        outs["vp"].append(v.reshape(batch, seq, N_KV_HEADS, HEAD_DIM))
        outs["kip"].append(ki.reshape(batch, seq, IDX_DIM))

        (u, vn, q, k, v, _, _, qi, ki, _, wit) = _proj(
            xs, ms[0], ms[1], wm, wt, lng, lnb, tm=bd, vn_dtype=F32)
        sc, sc_self = _s_scores(pt_flat, qi.reshape(bd, IDX_HEADS, IDX_DIM),
                                jnp.transpose(wit).reshape(bd, IDX_HEADS, 1),
                                ki.reshape(bd, 1, IDX_DIM), kidx_t, n_pages=n_pages)
        sel, sel_self, a_out = _s_select(sc.reshape(bd, past), sc_self.reshape(bd, 128), u, vn, ws0, bs0,
                                         n_sel=n_sel_s)
        sel2 = jnp.repeat(sel.reshape(bd, n_pages, PAGE_SIZE), 2, axis=2)
        b_out = _s_attn(pt_flat, q.reshape(bd, N_HEADS, HEAD_DIM), sel2, sel_self.reshape(bd, 1, 128),
                        k.reshape(bd, N_KV_HEADS, HEAD_DIM), v.reshape(bd, N_KV_HEADS, HEAD_DIM),
                        cache_k[l].reshape(n_pool, 2 * PAGE_SIZE, HEAD_DIM),
                        cache_v[l].reshape(n_pool, 2 * PAGE_SIZE, HEAD_DIM),
                        n_pages=n_pages, past=past)
        mixin = jnp.concatenate([a_out, b_out.reshape(bd, ATT_WIDTH)], axis=1)
        x1 = _outln(mixin, xs, ms[2], w_out_b, g1, be1, tm=bd, alpha=alpha)
        xs = _ffn(x1, ms[3], ms[4], ms[5], w1_b, b1, w2_b, b2, g2, be2, tm=bd, tf=tf, alpha=alpha)
        outs["ks"].append(k.reshape(bd, ts, N_KV_HEADS, HEAD_DIM))
        outs["vs"].append(v.reshape(bd, ts, N_KV_HEADS, HEAD_DIM))
        outs["kis"].append(ki.reshape(bd, ts, IDX_DIM))
        outs["vc"].append(vn.reshape(bd, ts, A_GROUPS, A_CH))

    st = lambda name: jnp.stack(outs[name])
    return (xp.reshape(batch, seq, d), xs.reshape(bd, ts, d),
            st("kp"), st("vp"), st("kip"), st("ks"), st("vs"), st("kis"), st("vc"))
```

```python
import functools

import jax
import jax.numpy as jnp
import numpy as np
from jax import lax
from jax.experimental import pallas as pl
from jax.experimental.pallas import tpu as pltpu

F32 = jnp.float32
BF16 = jnp.bfloat16
I32 = jnp.int32

CHUNK = 128
A_GROUPS = 8
A_CH = 128
A_WIDTH = A_GROUPS * A_CH
HEAD_DIM = 128
N_HEADS = 8
N_KV_HEADS = 2
GQA_GROUP = N_HEADS // N_KV_HEADS
ATT_WIDTH = N_HEADS * HEAD_DIM
KV_WIDTH = N_KV_HEADS * HEAD_DIM
IDX_HEADS = 16
IDX_DIM = 64
IDX_WIDTH = IDX_HEADS * IDX_DIM
TOPK_MAX = 256
Q_BLOCK = 128
PAGE_SIZE = 128
N_MOD = 6
LN_EPS = 1e-5
ATT_SCALE = HEAD_DIM ** -0.5
MAIN_WIDTH = 2 * A_WIDTH + ATT_WIDTH + 2 * KV_WIDTH + IDX_WIDTH
PROJ_TN = 512
FFN_TN = 512
ALIBI_SLOPES = tuple(float(2.0 ** (-8.0 * h / N_HEADS)) for h in range(1, N_HEADS + 1))

VMEM_LIMIT_BYTES = 58 * 1024 * 1024
INT_MIN = -(2 ** 31)
NEG_BIG = -1e30
FLT_MAX = float(np.finfo(np.float32).max)
LOG2E = float(np.log2(np.e))
KC = 512
AC = 512
LANE_HEADS = 4
DEN_ROWS = 16

NT_DIMS = (((1,), (1,)), ((), ()))


def _dot(a, b):
    return jnp.dot(a, b, preferred_element_type=F32)


def _dot_nt(a, b):
    return lax.dot_general(a, b, NT_DIMS, preferred_element_type=F32)


def _layer_norm(x, g, b):
    mu = jnp.mean(x, axis=-1, keepdims=True)
    xc = x - mu
    var = jnp.mean(xc * xc, axis=-1, keepdims=True)
    return xc * lax.rsqrt(var + LN_EPS) * g + b


def _key_to_float(key):
    bits = jnp.where(key < 0, key ^ jnp.int32(0x7FFFFFFF), key)
    return pltpu.bitcast(bits, F32)


def _params(*sem):
    return pltpu.CompilerParams(dimension_semantics=sem, vmem_limit_bytes=VMEM_LIMIT_BYTES)


def _cond_kernel(c_ref, w_ref, b_ref, o_ref):
    c = c_ref[...]
    a = (c * jax.nn.sigmoid(c)).astype(BF16)
    o_ref[...] = _dot(a, w_ref[...].astype(BF16)) + b_ref[...]


def _cond(c, w, b):
    m, d = c.shape
    n = w.shape[1]
    tn = 1024
    return pl.pallas_call(
        _cond_kernel,
        out_shape=jax.ShapeDtypeStruct((m, n), F32),
        grid=(n // tn,),
        in_specs=[pl.BlockSpec((m, d), lambda j: (0, 0)),
                  pl.BlockSpec((d, tn), lambda j: (0, j)),
                  pl.BlockSpec((1, tn), lambda j: (0, j))],
        out_specs=pl.BlockSpec((m, tn), lambda j: (0, j)),
        compiler_params=_params("arbitrary"),
        name="cond",
    )(c, w, b.reshape(1, n))


_J_V, _J_Q, _J_KV, _J_QI, _J_END = 2, 4, 6, 7, 9


def _proj_kernel(x_ref, shift_ref, scale_ref, wm_ref, wt_ref, lng_ref, lnb_ref,
                 u_ref, vn_ref, q_ref, k_ref, v_ref, kb_ref, vt_ref, qi_ref, ki_ref, ki2_ref, wit_ref,
                 h_scr):
    j = pl.program_id(1)

    @pl.when(j == 0)
    def _():
        h = (x_ref[...] * (1.0 + scale_ref[...]) + shift_ref[...]).astype(BF16)
        h_scr[...] = h
        tail = _dot_nt(h, wt_ref[...].astype(BF16))
        ki_ref[...] = tail[:, :IDX_DIM]
        lane = lax.broadcasted_iota(I32, tail.shape, 1)
        ki2_ref[...] = jnp.where(lane < IDX_DIM, tail, pltpu.roll(tail, IDX_DIM, axis=1)).astype(BF16)
        wit_ref[...] = tail.T[IDX_DIM:IDX_DIM + IDX_HEADS, :] * (IDX_HEADS ** -0.5)

    z = _dot_nt(h_scr[...], wm_ref[...].astype(BF16))

    @pl.when(j < _J_V)
    def _():
        u_ref[...] = z

    @pl.when(jnp.logical_and(j >= _J_V, j < _J_Q))
    def _():
        for g in range(PROJ_TN // A_CH):
            sl = slice(g * A_CH, (g + 1) * A_CH)
            vn_ref[:, sl] = _layer_norm(z[:, sl], lng_ref[:, sl], lnb_ref[:, sl]).astype(vn_ref.dtype)

    @pl.when(jnp.logical_and(j >= _J_Q, j < _J_KV))
    def _():
        q_ref[...] = z.astype(BF16)

    @pl.when(j == _J_KV)
    def _():
        for hh in range(N_KV_HEADS):
            k_ref[:, hh, :] = z[:, hh * HEAD_DIM:(hh + 1) * HEAD_DIM]
            v_ref[:, hh, :] = z[:, KV_WIDTH + hh * HEAD_DIM:KV_WIDTH + (hh + 1) * HEAD_DIM]
        kb_ref[...] = z[:, :KV_WIDTH].astype(BF16)
        vt = z[:, KV_WIDTH:].T.astype(BF16)
        for c in range(vt_ref.shape[0]):
            vt_ref[c] = vt[:, c * 128:(c + 1) * 128]

    @pl.when(j >= _J_QI)
    def _():
        qi_ref[...] = z.astype(BF16)


def _proj(x, shift, scale, wm, wt, lng, lnb, *, tm, vn_dtype):
    n, d = x.shape
    nb, r, _ = shift.shape
    rows_per_mod = n // nb
    assert n % tm == 0 and rows_per_mod % tm == 0 and r in (1, tm)
    tiles_per_mod = rows_per_mod // tm
    tn = PROJ_TN

    def clipj(lo, cnt):
        return lambda i, j: (i, jnp.clip(j - lo, 0, cnt - 1))

    mod_spec = pl.BlockSpec((None, r, d), lambda i, j: (i // tiles_per_mod, 0, 0))
    out_shape = (
        jax.ShapeDtypeStruct((n, A_WIDTH), F32),
        jax.ShapeDtypeStruct((n, A_WIDTH), vn_dtype),
        jax.ShapeDtypeStruct((n, ATT_WIDTH), BF16),
        jax.ShapeDtypeStruct((n, N_KV_HEADS, HEAD_DIM), F32),
        jax.ShapeDtypeStruct((n, N_KV_HEADS, HEAD_DIM), F32),
        jax.ShapeDtypeStruct((n, KV_WIDTH), BF16),
        jax.ShapeDtypeStruct((n // 128, KV_WIDTH, 128), BF16),
        jax.ShapeDtypeStruct((n, IDX_WIDTH), BF16),
        jax.ShapeDtypeStruct((n, IDX_DIM), F32),
        jax.ShapeDtypeStruct((n, 2 * IDX_DIM), BF16),
        jax.ShapeDtypeStruct((IDX_HEADS, n), F32),
    )
    out_specs = (
        pl.BlockSpec((tm, tn), clipj(0, 2)),
        pl.BlockSpec((tm, tn), clipj(_J_V, 2)),
        pl.BlockSpec((tm, tn), clipj(_J_Q, 2)),
        pl.BlockSpec((tm, N_KV_HEADS, HEAD_DIM), lambda i, j: (i, 0, 0)),
        pl.BlockSpec((tm, N_KV_HEADS, HEAD_DIM), lambda i, j: (i, 0, 0)),
        pl.BlockSpec((tm, KV_WIDTH), lambda i, j: (i, 0)),
        pl.BlockSpec((tm // 128, KV_WIDTH, 128), lambda i, j: (i, 0, 0)),
        pl.BlockSpec((tm, tn), clipj(_J_QI, 2)),
        pl.BlockSpec((tm, IDX_DIM), lambda i, j: (i, 0)),
        pl.BlockSpec((tm, 2 * IDX_DIM), lambda i, j: (i, 0)),
        pl.BlockSpec((IDX_HEADS, tm), lambda i, j: (0, i)),
    )
    in_specs = [
        pl.BlockSpec((tm, d), lambda i, j: (i, 0)),
        mod_spec, mod_spec,
        pl.BlockSpec((tn, d), lambda i, j: (j, 0)),
        pl.BlockSpec((2 * IDX_DIM, d), lambda i, j: (0, 0)),
        pl.BlockSpec((1, tn), lambda i, j: (0, jnp.clip(j - _J_V, 0, 1))),
        pl.BlockSpec((1, tn), lambda i, j: (0, jnp.clip(j - _J_V, 0, 1))),
    ]
    return pl.pallas_call(
        _proj_kernel,
        out_shape=out_shape,
        grid=(n // tm, _J_END),
        in_specs=in_specs,
        out_specs=out_specs,
        scratch_shapes=[pltpu.VMEM((tm, d), BF16)],
        compiler_params=_params("arbitrary", "arbitrary"),
        name="proj",
    )(x, shift, scale, wm, wt, lng, lnb)


def _mix_kernel(u_ref, vn_ref, q_ref, qi_ref, wit_ref, k_ref, vt_ref, ki2_ref, ws_ref, bsp_ref,
                o_ref, sc_scr, msk_scr, qim_scr, ab_scr, acc_scr, *, n_sel):
    jq = pl.program_id(1)
    n_kc = jq // (KC // 128) + 1
    n_ac = jq // (AC // 128) + 1
    row = lax.broadcasted_iota(I32, (128, 128), 0)
    col = lax.broadcasted_iota(I32, (128, 128), 1)

    @pl.when(jnp.logical_and(pl.program_id(0) == 0, jq == 0))
    def _():
        d0 = (lax.broadcasted_iota(I32, (AC, 128), 1) - lax.broadcasted_iota(I32, (AC, 128), 0)).astype(F32)
        for h in range(N_HEADS):
            lanes = slice((h % LANE_HEADS) * 128, (h % LANE_HEADS + 1) * 128)
            ab_scr[h // LANE_HEADS, :, lanes] = (ALIBI_SLOPES[h] * LOG2E) * d0

    for p in range(IDX_HEADS // 2):
        pair = qi_ref[:, p * 128:(p + 1) * 128]
        zero = jnp.zeros_like(pair)
        qim_scr[p, 0:128, :] = jnp.where(col < IDX_DIM, pair, zero)
        qim_scr[p, 128:256, :] = jnp.where(col >= IDX_DIM, pair, zero)
    wis = wit_ref[...] * (IDX_DIM ** -0.5)
    key_minus_query = (lax.broadcasted_iota(I32, (KC, 128), 0) - lax.broadcasted_iota(I32, (KC, 128), 1))

    def idx_body(kc, carry):
        base = pl.multiple_of(kc * KC, KC)
        kk = ki2_ref[pl.ds(base, KC), :]
        acc = jnp.zeros((KC, 128), F32)
        for p in range(IDX_HEADS // 2):
            s = _dot_nt(kk, qim_scr[p])
            acc = (acc + jnp.maximum(s[:, :128], 0.0) * wis[2 * p:2 * p + 1, :]
                   + jnp.maximum(s[:, 128:], 0.0) * wis[2 * p + 1:2 * p + 2, :])
        inadmissible = key_minus_query > (jq * 128 - kc * KC)
        sc_scr[pl.ds(base, KC), :] = jnp.where(inadmissible, -jnp.inf, acc)
        return carry

    lax.fori_loop(0, n_kc, idx_body, 0)

    def count(pred):
        def body(kc, c):
            x = sc_scr[pl.ds(pl.multiple_of(kc * KC, KC), KC), :]
            m = jnp.where(pred(x), 1, 0).astype(I32)
            return c + jnp.sum(m.reshape(KC // 8, 8, 128), axis=0)
        c = lax.fori_loop(0, n_kc, body, jnp.zeros((8, 128), I32))
        return jnp.sum(c, axis=0, keepdims=True)

    def bit_body(i, t):
        cand = t ^ lax.shift_left(jnp.int32(1), 31 - i)
        cand_f = _key_to_float(cand)
        return jnp.where(count(lambda x: x >= cand_f) >= n_sel, cand, t)

    thr_key = lax.fori_loop(0, 32, bit_body, jnp.full((1, 128), INT_MIN, I32))
    thr = jnp.where(thr_key == jnp.int32(INT_MIN), -FLT_MAX, _key_to_float(thr_key))
    c_ge = count(lambda x: x >= thr)
    c_gt = count(lambda x: x > thr)
    has_tie_overflow = jnp.max(c_ge) > n_sel

    @pl.when(jnp.logical_not(has_tie_overflow))
    def _():
        def body(kc, carry):
            sl = pl.ds(pl.multiple_of(kc * KC, KC), KC)
            msk_scr[sl, :] = jnp.where(sc_scr[sl, :] >= thr, 0.0, -jnp.inf).astype(F32)
            return carry
        lax.fori_loop(0, n_kc, body, 0)

    @pl.when(has_tie_overflow)
    def _():
        need = (n_sel - c_gt).astype(F32)
        lstrict = jnp.where(col < row, 1.0, 0.0).astype(BF16)

        def body(kt, before):
            sl = pl.ds(pl.multiple_of(kt * 128, 128), 128)
            x = sc_scr[sl, :]
            eq = x == thr
            eqf = jnp.where(eq, 1.0, 0.0).astype(F32)
            rank = before + _dot(lstrict, eqf.astype(BF16))
            sel = jnp.logical_or(x > thr, jnp.logical_and(eq, rank < need))
            msk_scr[sl, :] = jnp.where(sel, 0.0, -jnp.inf).astype(F32)
            return before + jnp.sum(eqf, axis=0, keepdims=True)
        lax.fori_loop(0, n_kc * (KC // 128), body, jnp.zeros((1, 128), F32))

    gw = LANE_HEADS * 128
    n_lg = N_HEADS // LANE_HEADS
    lane_head = lax.broadcasted_iota(I32, (1, gw), 1) // 128
    q_grp, slope_vec = [], []
    for g in range(n_lg):
        heads = range(g * LANE_HEADS, (g + 1) * LANE_HEADS)
        q_grp.append(jnp.concatenate([q_ref[:, h * HEAD_DIM:(h + 1) * HEAD_DIM] for h in heads], axis=0))
        sv = jnp.zeros((1, gw), F32)
        for i, h in enumerate(heads):
            sv = jnp.where(lane_head == i, ALIBI_SLOPES[h] * LOG2E, sv)
        slope_vec.append(sv)
        acc_scr[g] = jnp.zeros((HEAD_DIM + DEN_ROWS, gw), F32)

    def att_body(ac, carry):
        base = pl.multiple_of(ac * AC, AC)
        off = (jq * 128 - ac * AC).astype(F32)
        mk = msk_scr[pl.ds(base, AC), :]
        mkw = jnp.concatenate([mk] * LANE_HEADS, axis=1)
        new = []
        for g in range(n_lg):
            kv = g * LANE_HEADS // GQA_GROUP
            m = carry[g]
            raw = _dot_nt(k_ref[pl.ds(base, AC), kv * HEAD_DIM:(kv + 1) * HEAD_DIM], q_grp[g])
            x = raw * (ATT_SCALE * LOG2E) - ab_scr[g] + mkw
            cvec = slope_vec[g] * off
            m_new = jnp.maximum(m, jnp.max(x, axis=0, keepdims=True) - cvec)
            alpha = jnp.exp2(m - m_new)
            p = jnp.exp2(x - (m_new + cvec)).astype(BF16)
            vt_c = jnp.concatenate(
                [jnp.concatenate([vt_ref[ac * (AC // 128) + t, kv * HEAD_DIM:(kv + 1) * HEAD_DIM, :]
                                  for t in range(AC // 128)], axis=1),
                 jnp.ones((DEN_ROWS, AC), BF16)], axis=0)
            acc_scr[g] = alpha * acc_scr[g] + _dot(vt_c, p)
            new.append(m_new)
        return tuple(new)

    lax.fori_loop(0, n_ac, att_body, (jnp.full((1, gw), NEG_BIG, F32),) * n_lg)
    for g in range(n_lg):
        out_t = acc_scr[g, :HEAD_DIM, :] / acc_scr[g, HEAD_DIM:HEAD_DIM + 1, :]
        for i in range(LANE_HEADS):
            lo = A_WIDTH + (g * LANE_HEADS + i) * HEAD_DIM
            o_ref[:, lo:lo + HEAD_DIM] = out_t[:, i * 128:(i + 1) * 128].T.astype(o_ref.dtype)

    for g in range(A_GROUPS):
        sl = slice(g * A_CH, (g + 1) * A_CH)
        sp = _dot(ws_ref[g], vn_ref[:, sl]) + bsp_ref[:, sl]
        o_ref[:, sl] = (u_ref[:, sl] * sp).astype(o_ref.dtype)


def _mix(u, vn, q, qi, wit, kb, vt, ki2, ws, bsp, *, batch, seq, n_sel):
    n = batch * seq
    nblk = seq // Q_BLOCK
    ntile = seq // 128
    row_spec = lambda w: pl.BlockSpec((Q_BLOCK, w), lambda b, j: (b * nblk + j, 0))
    in_specs = [
        row_spec(A_WIDTH), row_spec(A_WIDTH), row_spec(ATT_WIDTH), row_spec(IDX_WIDTH),
        pl.BlockSpec((IDX_HEADS, Q_BLOCK), lambda b, j: (0, b * nblk + j)),
        pl.BlockSpec((seq, KV_WIDTH), lambda b, j: (b, 0)),
        pl.BlockSpec((ntile, KV_WIDTH, 128), lambda b, j: (b, 0, 0)),
        pl.BlockSpec((seq, 2 * IDX_DIM), lambda b, j: (b, 0)),
        pl.BlockSpec((A_GROUPS, CHUNK, CHUNK), lambda b, j: (0, 0, 0)),
        pl.BlockSpec((CHUNK, A_WIDTH), lambda b, j: (0, 0)),
    ]
    return pl.pallas_call(
        functools.partial(_mix_kernel, n_sel=n_sel),
        out_shape=jax.ShapeDtypeStruct((n, A_WIDTH + ATT_WIDTH), BF16),
        grid=(batch, nblk),
        in_specs=in_specs,
        out_specs=pl.BlockSpec((Q_BLOCK, A_WIDTH + ATT_WIDTH), lambda b, j: (b * nblk + j, 0)),
        scratch_shapes=[pltpu.VMEM((seq, 128), F32),
                        pltpu.VMEM((seq, 128), F32),
                        pltpu.VMEM((IDX_HEADS // 2, 256, 128), BF16),
                        pltpu.VMEM((N_HEADS // LANE_HEADS, AC, LANE_HEADS * 128), F32),
                        pltpu.VMEM((N_HEADS // LANE_HEADS, HEAD_DIM + DEN_ROWS, LANE_HEADS * 128), F32)],
        compiler_params=_params("arbitrary", "arbitrary"),
        name="mix",
    )(u, vn, q, qi, wit, kb, vt, ki2, ws, bsp)


def _s_scores_kernel(pt_ref, qi_ref, wi_ref, kin_ref, *rest, n_pages):
    del pt_ref
    pages = rest[:n_pages]
    sc_ref, self_ref = rest[n_pages], rest[n_pages + 1]
    qi = qi_ref[...]
    w = wi_ref[...] * (IDX_DIM ** -0.5)
    kcat = jnp.concatenate([pages[p][...] for p in range(n_pages)], axis=1).astype(BF16)
    s = _dot(qi, kcat)
    sc_ref[...] = jnp.sum(jnp.maximum(s, 0.0) * w, axis=0, keepdims=True)
    kin = kin_ref[...].astype(BF16).astype(F32)
    s_self = jnp.sum(qi.astype(F32) * kin, axis=1, keepdims=True)
    v_self = jnp.sum(jnp.maximum(s_self, 0.0) * w, axis=0, keepdims=True)
    self_ref[...] = jnp.broadcast_to(v_self, self_ref.shape)


def _s_scores(page_table_flat, qi3, wi_col, ki_new3, cache_kidx_l, *, n_pages):
    bd = qi3.shape[0]

    def page_spec(p):
        return pl.BlockSpec((None, IDX_DIM, PAGE_SIZE), lambda b, pt, p=p: (pt[b * n_pages + p], 0, 0))

    past = n_pages * PAGE_SIZE
    grid_spec = pltpu.PrefetchScalarGridSpec(
        num_scalar_prefetch=1,
        grid=(bd,),
        in_specs=[pl.BlockSpec((None, IDX_HEADS, IDX_DIM), lambda b, pt: (b, 0, 0)),
                  pl.BlockSpec((None, IDX_HEADS, 1), lambda b, pt: (b, 0, 0)),
                  pl.BlockSpec((None, 1, IDX_DIM), lambda b, pt: (b, 0, 0))]
                 + [page_spec(p) for p in range(n_pages)],
        out_specs=(pl.BlockSpec((None, 1, past), lambda b, pt: (b, 0, 0)),
                   pl.BlockSpec((None, 1, 128), lambda b, pt: (b, 0, 0))),
    )
    return pl.pallas_call(
        functools.partial(_s_scores_kernel, n_pages=n_pages),
        out_shape=(jax.ShapeDtypeStruct((bd, 1, past), F32),
                   jax.ShapeDtypeStruct((bd, 1, 128), F32)),
        grid_spec=grid_spec,
        compiler_params=_params("arbitrary"),
        name="s_scores",
    )(page_table_flat, qi3, wi_col, ki_new3, *([cache_kidx_l] * n_pages))


def _s_select_kernel(sc_ref, self_ref, u_ref, vn_ref, ws0_ref, bs0_ref,
                     sel_ref, selself_ref, a_ref, *, n_sel):
    keys = sc_ref[...]
    kself = self_ref[:, 0:1]
    bd, past = keys.shape

    def count_ge(cand):
        c = jnp.sum(jnp.where(keys >= cand, 1, 0).astype(I32), axis=1, keepdims=True)
        return c + jnp.where(kself >= cand, 1, 0).astype(I32)

    def bit_body(i, t):
        cand = t ^ lax.shift_left(jnp.int32(1), 31 - i)
        return jnp.where(count_ge(_key_to_float(cand)) >= n_sel, cand, t)

    thr = _key_to_float(lax.fori_loop(0, 32, bit_body, jnp.full((bd, 1), INT_MIN, I32)))
    c_gt = (jnp.sum(jnp.where(keys > thr, 1, 0).astype(I32), axis=1, keepdims=True)
            + jnp.where(kself > thr, 1, 0).astype(I32))
    need = (n_sel - c_gt).astype(F32)
    r = lax.broadcasted_iota(I32, (128, 128), 0)
    c = lax.broadcasted_iota(I32, (128, 128), 1)
    ustrict = jnp.where(r < c, 1.0, 0.0).astype(BF16)
    before = jnp.zeros((bd, 1), F32)
    for t in range(past // 128):
        kt = keys[:, t * 128:(t + 1) * 128]
        eq = kt == thr
        eqf = jnp.where(eq, 1.0, 0.0).astype(F32)
        rank = before + _dot(eqf.astype(BF16), ustrict)
        sel = jnp.logical_or(kt > thr, jnp.logical_and(eq, rank < need))
        sel_ref[:, t * 128:(t + 1) * 128] = jnp.where(sel, 1.0, 0.0).astype(F32)
        before = before + jnp.sum(eqf, axis=1, keepdims=True)
    sel_self = jnp.logical_or(kself > thr, jnp.logical_and(kself == thr, before < need))
    selself_ref[...] = jnp.broadcast_to(jnp.where(sel_self, 1.0, 0.0).astype(F32), selself_ref.shape)
    sp = ws0_ref[...] * vn_ref[...] + bs0_ref[...]
    a_ref[...] = (u_ref[...] * sp).astype(a_ref.dtype)


def _s_select(sc, sc_self, u, vn, ws0, bs0, *, n_sel):
    bd, past = sc.shape
    full = lambda a: pl.BlockSpec(a.shape, lambda i: (0,) * a.ndim)
    args = (sc, sc_self, u, vn, ws0, bs0)
    return pl.pallas_call(
        functools.partial(_s_select_kernel, n_sel=n_sel),
        out_shape=(jax.ShapeDtypeStruct((bd, past), F32),
                   jax.ShapeDtypeStruct((bd, 128), F32),
                   jax.ShapeDtypeStruct((bd, A_WIDTH), BF16)),
        grid=(1,),
        in_specs=[full(a) for a in args],
        out_specs=(pl.BlockSpec((bd, past), lambda i: (0, 0)),
                   pl.BlockSpec((bd, 128), lambda i: (0, 0)),
                   pl.BlockSpec((bd, A_WIDTH), lambda i: (0, 0))),
        compiler_params=_params("arbitrary"),
        name="s_select",
    )(*args)


def _s_attn_kernel(pt_ref, q_ref, sel_ref, selself_ref, knew_ref, vnew_ref, *rest, n_pages, past):
    del pt_ref
    kpages = rest[:n_pages]
    vpages = rest[n_pages:2 * n_pages]
    o_ref = rest[2 * n_pages]
    q = q_ref[...]
    hrow = lax.broadcasted_iota(I32, (N_HEADS, 2 * PAGE_SIZE), 0)
    ccol = lax.broadcasted_iota(I32, (N_HEADS, 2 * PAGE_SIZE), 1)
    own_kv = (ccol & 1) == (hrow // GQA_GROUP)
    pos_in_page = ccol >> 1
    hcol = lax.broadcasted_iota(I32, (N_HEADS, 1), 0)
    slope = jnp.zeros((N_HEADS, 1), F32)
    for h in range(N_HEADS):
        slope = jnp.where(hcol == h, ALIBI_SLOPES[h], slope)

    logits = []
    for p in range(n_pages):
        kp = kpages[p][...].astype(BF16)
        lg = _dot_nt(q, kp) * ATT_SCALE
        dist = (past - (p * PAGE_SIZE + pos_in_page)).astype(F32)
        lg = lg - slope * dist
        ok = jnp.logical_and(own_kv, sel_ref[p:p + 1, :] > 0.5)
        logits.append(jnp.where(ok, lg, -jnp.inf))
    first_group = lax.broadcasted_iota(I32, (N_HEADS, HEAD_DIM), 0) < GQA_GROUP
    knew = knew_ref[...].astype(BF16).astype(F32)
    vnew = vnew_ref[...].astype(BF16).astype(F32)
    knew8 = jnp.where(first_group, knew[0:1, :], knew[1:2, :])
    vnew8 = jnp.where(first_group, vnew[0:1, :], vnew[1:2, :])
    lg_self = jnp.sum(q.astype(F32) * knew8, axis=1, keepdims=True) * ATT_SCALE
    lg_self = jnp.where(selself_ref[:, 0:1] > 0.5, lg_self, -jnp.inf)

    m = jnp.maximum(lg_self, NEG_BIG)
    for lg in logits:
        m = jnp.maximum(m, jnp.max(lg, axis=1, keepdims=True))
    p_self = jnp.exp(lg_self - m)
    l = p_self
    acc = p_self.astype(BF16).astype(F32) * vnew8
    for p in range(n_pages):
        pp = jnp.exp(logits[p] - m)
        l = l + jnp.sum(pp, axis=1, keepdims=True)
        acc = acc + _dot(pp.astype(BF16), vpages[p][...].astype(BF16))
    o_ref[...] = (acc / l).astype(o_ref.dtype)


def _s_attn(page_table_flat, q3, sel2, sel_self3, k_new3, v_new3, cache_k2, cache_v2, *, n_pages, past):
    bd = q3.shape[0]

    def page_spec(p):
        return pl.BlockSpec((None, 2 * PAGE_SIZE, HEAD_DIM), lambda b, pt, p=p: (pt[b * n_pages + p], 0, 0))

    per_b = lambda s1, s2: pl.BlockSpec((None, s1, s2), lambda b, pt: (b, 0, 0))
    grid_spec = pltpu.PrefetchScalarGridSpec(
        num_scalar_prefetch=1,
        grid=(bd,),
        in_specs=[per_b(N_HEADS, HEAD_DIM), per_b(n_pages, 2 * PAGE_SIZE), per_b(1, 128),
                  per_b(N_KV_HEADS, HEAD_DIM), per_b(N_KV_HEADS, HEAD_DIM)]
                 + [page_spec(p) for p in range(n_pages)] * 2,
        out_specs=per_b(N_HEADS, HEAD_DIM),
    )
    return pl.pallas_call(
        functools.partial(_s_attn_kernel, n_pages=n_pages, past=past),
        out_shape=jax.ShapeDtypeStruct((bd, N_HEADS, HEAD_DIM), BF16),
        grid_spec=grid_spec,
        compiler_params=_params("arbitrary"),
        name="s_attn",
    )(page_table_flat, q3, sel2, sel_self3, k_new3, v_new3, *([cache_k2] * n_pages), *([cache_v2] * n_pages))


def _outln_kernel(mix_ref, x_ref, gate_ref, w_ref, g_ref, b_ref, o_ref, *, alpha):
    mix = _dot(mix_ref[...], w_ref[...])
    y = alpha * x_ref[...] + (1.0 + gate_ref[...]) * mix
    o_ref[...] = _layer_norm(y, g_ref[...], b_ref[...])


def _outln(mixin, x, gate, w, g, b, *, tm, alpha):
    n, d = x.shape
    nb, r, _ = gate.shape
    tiles_per_mod = (n // nb) // tm
    kin = mixin.shape[1]
    return pl.pallas_call(
        functools.partial(_outln_kernel, alpha=alpha),
        out_shape=jax.ShapeDtypeStruct((n, d), F32),
        grid=(n // tm,),
        in_specs=[pl.BlockSpec((tm, kin), lambda i: (i, 0)),
                  pl.BlockSpec((tm, d), lambda i: (i, 0)),
                  pl.BlockSpec((None, r, d), lambda i: (i // tiles_per_mod, 0, 0)),
                  pl.BlockSpec((kin, d), lambda i: (0, 0), pipeline_mode=pl.Buffered(1)),
                  pl.BlockSpec((1, d), lambda i: (0, 0)),
                  pl.BlockSpec((1, d), lambda i: (0, 0))],
        out_specs=pl.BlockSpec((tm, d), lambda i: (i, 0)),
        compiler_params=_params("arbitrary"),
        name="outln",
    )(mixin, x, gate, w, g, b)


def _ffn_kernel(x_ref, shift_ref, scale_ref, gate_ref, w1_ref, b1_ref, w2_ref, b2_ref, g_ref, b_ref,
                o_ref, h_scr, *, alpha):
    f = pl.program_id(1)

    @pl.when(f == 0)
    def _():
        h_scr[...] = (x_ref[...] * (1.0 + scale_ref[...]) + shift_ref[...]).astype(BF16)
        o_ref[...] = jnp.zeros_like(o_ref)

    a = jnp.maximum(_dot(h_scr[...], w1_ref[...]) + b1_ref[...], 0.0)
    a2 = (a * a).astype(BF16)
    for c in range(o_ref.shape[1] // FFN_TN):
        cs = slice(c * FFN_TN, (c + 1) * FFN_TN)
        o_ref[:, cs] += _dot(a2, w2_ref[:, cs])

    @pl.when(f == pl.num_programs(1) - 1)
    def _():
        y = alpha * x_ref[...] + (1.0 + gate_ref[...]) * (o_ref[...] + b2_ref[...])
        o_ref[...] = _layer_norm(y, g_ref[...], b_ref[...])


def _ffn(x, shift, scale, gate, w1, b1, w2, b2, g, b, *, tm, tf, alpha):
    n, d = x.shape
    nb, r, _ = gate.shape
    dff = w1.shape[1]
    tiles_per_mod = (n // nb) // tm
    mod_spec = pl.BlockSpec((None, r, d), lambda i, f: (i // tiles_per_mod, 0, 0))
    return pl.pallas_call(
        functools.partial(_ffn_kernel, alpha=alpha),
        out_shape=jax.ShapeDtypeStruct((n, d), F32),
        grid=(n // tm, dff // tf),
        in_specs=[pl.BlockSpec((tm, d), lambda i, f: (i, 0), pipeline_mode=pl.Buffered(1)),
                  mod_spec, mod_spec, mod_spec,
                  pl.BlockSpec((d, tf), lambda i, f: (0, f)),
                  pl.BlockSpec((1, tf), lambda i, f: (0, f)),
                  pl.BlockSpec((tf, d), lambda i, f: (f, 0)),
                  pl.BlockSpec((1, d), lambda i, f: (0, 0)),
                  pl.BlockSpec((1, d), lambda i, f: (0, 0)),
                  pl.BlockSpec((1, d), lambda i, f: (0, 0))],
        out_specs=pl.BlockSpec((tm, d), lambda i, f: (i, 0)),
        scratch_shapes=[pltpu.VMEM((tm, d), BF16)],
        compiler_params=_params("arbitrary", "arbitrary"),
        name="ffn",
    )(x, shift, scale, gate, w1, b1, w2, b2, g, b)


def kernel(x_prompt, x_sample, cache_k, cache_v, cache_kidx, page_table, c_prompt, c_sample,
           w_cond, b_cond, w_in, ln_v_g, ln_v_b, w_spatial, b_spatial, w_out,
           ln1_g, ln1_b, w_ff1, b_ff1, w_ff2, b_ff2, ln2_g, ln2_b):
    batch, seq, d = x_prompt.shape
    bd, ts, _ = x_sample.shape
    depth = w_in.shape[0]
    n_pages = page_table.shape[1]
    past = n_pages * PAGE_SIZE
    n_pool = cache_k.shape[1]
    dff = w_ff1.shape[2]
    assert ts == 1 and seq % KC == 0 and d == A_WIDTH + ATT_WIDTH
    assert w_in.shape[2] == MAIN_WIDTH + IDX_DIM + IDX_HEADS
    n = batch * seq
    alpha = (2 * depth) ** 0.25
    n_sel_p = min(TOPK_MAX, seq // 4)
    n_sel_s = min(TOPK_MAX, (past + ts) // 4)
    tm_p = min(1024, seq)
    tm_o = min(512, seq)
    tf = min(1024, dff)
    pt_flat = page_table.reshape(-1).astype(I32)

    xp = x_prompt.reshape(n, d)
    xs = x_sample.reshape(bd, d)
    c_all = jnp.concatenate([c_prompt, c_sample], axis=0)

    outs = {k: [] for k in ("kp", "vp", "kip", "ks", "vs", "kis", "vc")}
    for l in range(depth):
        wm = jnp.swapaxes(w_in[l], 0, 1)
        wt = jnp.pad(wm[MAIN_WIDTH:], ((0, 2 * IDX_DIM - IDX_DIM - IDX_HEADS), (0, 0)))
        kidx_t = jnp.swapaxes(cache_kidx[l], 1, 2)
        lng = ln_v_g[l].reshape(1, A_WIDTH)
        lnb = ln_v_b[l].reshape(1, A_WIDTH)
        tril = jnp.tril(jnp.ones((CHUNK, CHUNK), dtype=bool))
        ws = jnp.where(tril[None], w_spatial[l], 0.0).astype(BF16)
        bsp = jnp.repeat(jnp.transpose(b_spatial[l]), A_CH, axis=1)
        ws0 = jnp.repeat(w_spatial[l][:, 0, 0], A_CH).reshape(1, A_WIDTH)
        bs0 = jnp.repeat(b_spatial[l][:, 0], A_CH).reshape(1, A_WIDTH)
        w_out_b = w_out[l].astype(BF16)
        w1_b = w_ff1[l].astype(BF16)
        w2_b = w_ff2[l].astype(BF16)
        b1 = b_ff1[l].reshape(1, dff)
        b2 = b_ff2[l].reshape(1, d)
        g1, be1 = ln1_g[l].reshape(1, d), ln1_b[l].reshape(1, d)
        g2, be2 = ln2_g[l].reshape(1, d), ln2_b[l].reshape(1, d)

        z = _cond(c_all, w_cond[l], b_cond[l])
        mods = [z[:, i * d:(i + 1) * d] for i in range(N_MOD)]
        mp = [m[:batch].reshape(batch, 1, d) for m in mods]
        ms = [m[batch:].reshape(1, bd, d) for m in mods]

        (u, vn, q, k, v, kb, vt, qi, ki, ki2, wit) = _proj(
            xp, mp[0], mp[1], wm, wt, lng, lnb, tm=tm_p, vn_dtype=BF16)
        mixin = _mix(u, vn, q, qi, wit, kb, vt, ki2, ws, bsp, batch=batch, seq=seq, n_sel=n_sel_p)
        x1 = _outln(mixin, xp, mp[2], w_out_b, g1, be1, tm=tm_o, alpha=alpha)
        xp = _ffn(x1, mp[3], mp[4], mp[5], w1_b, b1, w2_b, b2, g2, be2, tm=tm_p, tf=tf, alpha=alpha)
        outs["kp"].append(k.reshape(batch, seq, N_KV_HEADS, HEAD_DIM))
        outs["vp"].append(v.reshape(batch, seq, N_KV_HEADS, HEAD_DIM))
        outs["kip"].append(ki.reshape(batch, seq, IDX_DIM))

        (u, vn, q, k, v, _, _, qi, ki, _, wit) = _proj(
            xs, ms[0], ms[1], wm, wt, lng, lnb, tm=bd, vn_dtype=F32)
        sc, sc_self = _s_scores(pt_flat, qi.reshape(bd, IDX_HEADS, IDX_DIM),
                                jnp.transpose(wit).reshape(bd, IDX_HEADS, 1),
                                ki.reshape(bd, 1, IDX_DIM), kidx_t, n_pages=n_pages)
        sel, sel_self, a_out = _s_select(sc.reshape(bd, past), sc_self.reshape(bd, 128), u, vn, ws0, bs0,
                                         n_sel=n_sel_s)
        sel2 = jnp.repeat(sel.reshape(bd, n_pages, PAGE_SIZE), 2, axis=2)
        b_out = _s_attn(pt_flat, q.reshape(bd, N_HEADS, HEAD_DIM), sel2, sel_self.reshape(bd, 1, 128),
                        k.reshape(bd, N_KV_HEADS, HEAD_DIM), v.reshape(bd, N_KV_HEADS, HEAD_DIM),
                        cache_k[l].reshape(n_pool, 2 * PAGE_SIZE, HEAD_DIM),
                        cache_v[l].reshape(n_pool, 2 * PAGE_SIZE, HEAD_DIM),
                        n_pages=n_pages, past=past)
        mixin = jnp.concatenate([a_out, b_out.reshape(bd, ATT_WIDTH)], axis=1)
        x1 = _outln(mixin, xs, ms[2], w_out_b, g1, be1, tm=bd, alpha=alpha)
        xs = _ffn(x1, ms[3], ms[4], ms[5], w1_b, b1, w2_b, b2, g2, be2, tm=bd, tf=tf, alpha=alpha)
        outs["ks"].append(k.reshape(bd, ts, N_KV_HEADS, HEAD_DIM))
        outs["vs"].append(v.reshape(bd, ts, N_KV_HEADS, HEAD_DIM))
        outs["kis"].append(ki.reshape(bd, ts, IDX_DIM))
        outs["vc"].append(vn.reshape(bd, ts, A_GROUPS, A_CH))

    st = lambda name: jnp.stack(outs[name])
    return (xp.reshape(batch, seq, d), xs.reshape(bd, ts, d),
            st("kp"), st("vp"), st("kip"), st("ks"), st("vs"), st("kis"), st("vc"))
```

```python
import functools

import jax
import jax.numpy as jnp
import numpy as np
from jax import lax
from jax.experimental import pallas as pl
from jax.experimental.pallas import tpu as pltpu

F32 = jnp.float32
BF16 = jnp.bfloat16
I32 = jnp.int32

CHUNK = 128
A_GROUPS = 8
A_CH = 128
A_WIDTH = A_GROUPS * A_CH
HEAD_DIM = 128
N_HEADS = 8
N_KV_HEADS = 2
GQA_GROUP = N_HEADS // N_KV_HEADS
ATT_WIDTH = N_HEADS * HEAD_DIM
KV_WIDTH = N_KV_HEADS * HEAD_DIM
IDX_HEADS = 16
IDX_DIM = 64
IDX_WIDTH = IDX_HEADS * IDX_DIM
TOPK_MAX = 256
Q_BLOCK = 128
PAGE_SIZE = 128
N_MOD = 6
LN_EPS = 1e-5
ATT_SCALE = HEAD_DIM ** -0.5
MAIN_WIDTH = 2 * A_WIDTH + ATT_WIDTH + 2 * KV_WIDTH + IDX_WIDTH
PROJ_TN = 512
FFN_TN = 512
ALIBI_SLOPES = tuple(float(2.0 ** (-8.0 * h / N_HEADS)) for h in range(1, N_HEADS + 1))

VMEM_LIMIT_BYTES = 58 * 1024 * 1024
INT_MIN = -(2 ** 31)
NEG_BIG = -1e30
FLT_MAX = float(np.finfo(np.float32).max)
LOG2E = float(np.log2(np.e))
KC = 512
AC = 512
LANE_HEADS = 4
DEN_ROWS = 16

NT_DIMS = (((1,), (1,)), ((), ()))


def _dot(a, b):
    return jnp.dot(a, b, preferred_element_type=F32)


def _dot_nt(a, b):
    return lax.dot_general(a, b, NT_DIMS, preferred_element_type=F32)


def _layer_norm(x, g, b):
    mu = jnp.mean(x, axis=-1, keepdims=True)
    xc = x - mu
    var = jnp.mean(xc * xc, axis=-1, keepdims=True)
    return xc * lax.rsqrt(var + LN_EPS) * g + b


def _key_to_float(key):
    bits = jnp.where(key < 0, key ^ jnp.int32(0x7FFFFFFF), key)
    return pltpu.bitcast(bits, F32)


def _params(*sem):
    return pltpu.CompilerParams(dimension_semantics=sem, vmem_limit_bytes=VMEM_LIMIT_BYTES)


def _cond_kernel(c_ref, w_ref, b_ref, o_ref):
    c = c_ref[...]
    a = (c * jax.nn.sigmoid(c)).astype(BF16)
    o_ref[...] = _dot(a, w_ref[...].astype(BF16)) + b_ref[...]


def _cond(c, w, b):
    m, d = c.shape
    n = w.shape[1]
    tn = 1024
    return pl.pallas_call(
        _cond_kernel,
        out_shape=jax.ShapeDtypeStruct((m, n), F32),
        grid=(n // tn,),
        in_specs=[pl.BlockSpec((m, d), lambda j: (0, 0)),
                  pl.BlockSpec((d, tn), lambda j: (0, j)),
                  pl.BlockSpec((1, tn), lambda j: (0, j))],
        out_specs=pl.BlockSpec((m, tn), lambda j: (0, j)),
        compiler_params=_params("arbitrary"),
        name="cond",
    )(c, w, b.reshape(1, n))


_J_V, _J_Q, _J_KV, _J_QI, _J_END = 2, 4, 6, 7, 9


def _proj_kernel(x_ref, shift_ref, scale_ref, wm_ref, wt_ref, lng_ref, lnb_ref,
                 u_ref, vn_ref, q_ref, k_ref, v_ref, kb_ref, vt_ref, qi_ref, ki_ref, ki2_ref, wit_ref,
                 h_scr):
    j = pl.program_id(1)

    @pl.when(j == 0)
    def _():
        h = (x_ref[...] * (1.0 + scale_ref[...]) + shift_ref[...]).astype(BF16)
        h_scr[...] = h
        tail = _dot_nt(h, wt_ref[...].astype(BF16))
        ki_ref[...] = tail[:, :IDX_DIM]
        lane = lax.broadcasted_iota(I32, tail.shape, 1)
        ki2_ref[...] = jnp.where(lane < IDX_DIM, tail, pltpu.roll(tail, IDX_DIM, axis=1)).astype(BF16)
        wit_ref[...] = tail.T[IDX_DIM:IDX_DIM + IDX_HEADS, :] * (IDX_HEADS ** -0.5)

    z = _dot_nt(h_scr[...], wm_ref[...].astype(BF16))

    @pl.when(j < _J_V)
    def _():
        u_ref[...] = z

    @pl.when(jnp.logical_and(j >= _J_V, j < _J_Q))
    def _():
        for g in range(PROJ_TN // A_CH):
            sl = slice(g * A_CH, (g + 1) * A_CH)
            vn_ref[:, sl] = _layer_norm(z[:, sl], lng_ref[:, sl], lnb_ref[:, sl]).astype(vn_ref.dtype)

    @pl.when(jnp.logical_and(j >= _J_Q, j < _J_KV))
    def _():
        q_ref[...] = z.astype(BF16)

    @pl.when(j == _J_KV)
    def _():
        for hh in range(N_KV_HEADS):
            k_ref[:, hh, :] = z[:, hh * HEAD_DIM:(hh + 1) * HEAD_DIM]
            v_ref[:, hh, :] = z[:, KV_WIDTH + hh * HEAD_DIM:KV_WIDTH + (hh + 1) * HEAD_DIM]
        kb_ref[...] = z[:, :KV_WIDTH].astype(BF16)
        vt = z[:, KV_WIDTH:].T.astype(BF16)
        for c in range(vt_ref.shape[0]):
            vt_ref[c] = vt[:, c * 128:(c + 1) * 128]

    @pl.when(j >= _J_QI)
    def _():
        qi_ref[...] = z.astype(BF16)


def _proj(x, shift, scale, wm, wt, lng, lnb, *, tm, vn_dtype):
    n, d = x.shape
    nb, r, _ = shift.shape
    rows_per_mod = n // nb
    assert n % tm == 0 and rows_per_mod % tm == 0 and r in (1, tm)
    tiles_per_mod = rows_per_mod // tm
    tn = PROJ_TN

    def clipj(lo, cnt):
        return lambda i, j: (i, jnp.clip(j - lo, 0, cnt - 1))

    mod_spec = pl.BlockSpec((None, r, d), lambda i, j: (i // tiles_per_mod, 0, 0))
    out_shape = (
        jax.ShapeDtypeStruct((n, A_WIDTH), F32),
        jax.ShapeDtypeStruct((n, A_WIDTH), vn_dtype),
        jax.ShapeDtypeStruct((n, ATT_WIDTH), BF16),
        jax.ShapeDtypeStruct((n, N_KV_HEADS, HEAD_DIM), F32),
        jax.ShapeDtypeStruct((n, N_KV_HEADS, HEAD_DIM), F32),
        jax.ShapeDtypeStruct((n, KV_WIDTH), BF16),
        jax.ShapeDtypeStruct((n // 128, KV_WIDTH, 128), BF16),
        jax.ShapeDtypeStruct((n, IDX_WIDTH), BF16),
        jax.ShapeDtypeStruct((n, IDX_DIM), F32),
        jax.ShapeDtypeStruct((n, 2 * IDX_DIM), BF16),
        jax.ShapeDtypeStruct((IDX_HEADS, n), F32),
    )
    out_specs = (
        pl.BlockSpec((tm, tn), clipj(0, 2)),
        pl.BlockSpec((tm, tn), clipj(_J_V, 2)),
        pl.BlockSpec((tm, tn), clipj(_J_Q, 2)),
        pl.BlockSpec((tm, N_KV_HEADS, HEAD_DIM), lambda i, j: (i, 0, 0)),
        pl.BlockSpec((tm, N_KV_HEADS, HEAD_DIM), lambda i, j: (i, 0, 0)),
        pl.BlockSpec((tm, KV_WIDTH), lambda i, j: (i, 0)),
        pl.BlockSpec((tm // 128, KV_WIDTH, 128), lambda i, j: (i, 0, 0)),
        pl.BlockSpec((tm, tn), clipj(_J_QI, 2)),
        pl.BlockSpec((tm, IDX_DIM), lambda i, j: (i, 0)),
        pl.BlockSpec((tm, 2 * IDX_DIM), lambda i, j: (i, 0)),
        pl.BlockSpec((IDX_HEADS, tm), lambda i, j: (0, i)),
    )
    in_specs = [
        pl.BlockSpec((tm, d), lambda i, j: (i, 0)),
        mod_spec, mod_spec,
        pl.BlockSpec((tn, d), lambda i, j: (j, 0)),
        pl.BlockSpec((2 * IDX_DIM, d), lambda i, j: (0, 0)),
        pl.BlockSpec((1, tn), lambda i, j: (0, jnp.clip(j - _J_V, 0, 1))),
        pl.BlockSpec((1, tn), lambda i, j: (0, jnp.clip(j - _J_V, 0, 1))),
    ]
    return pl.pallas_call(
        _proj_kernel,
        out_shape=out_shape,
        grid=(n // tm, _J_END),
        in_specs=in_specs,
        out_specs=out_specs,
        scratch_shapes=[pltpu.VMEM((tm, d), BF16)],
        compiler_params=_params("arbitrary", "arbitrary"),
        name="proj",
    )(x, shift, scale, wm, wt, lng, lnb)


def _mix_kernel(u_ref, vn_ref, q_ref, qi_ref, wit_ref, k_ref, vt_ref, ki2_ref, ws_ref, bsp_ref,
                o_ref, sc_scr, msk_scr, qim_scr, ab_scr, acc_scr, *, n_sel):
    jq = pl.program_id(1)
    n_kc = jq // (KC // 128) + 1
    n_ac = jq // (AC // 128) + 1
    row = lax.broadcasted_iota(I32, (128, 128), 0)
    col = lax.broadcasted_iota(I32, (128, 128), 1)

    @pl.when(jnp.logical_and(pl.program_id(0) == 0, jq == 0))
    def _():
        d0 = (lax.broadcasted_iota(I32, (AC, 128), 1) - lax.broadcasted_iota(I32, (AC, 128), 0)).astype(F32)
        for h in range(N_HEADS):
            lanes = slice((h % LANE_HEADS) * 128, (h % LANE_HEADS + 1) * 128)
            ab_scr[h // LANE_HEADS, :, lanes] = (ALIBI_SLOPES[h] * LOG2E) * d0

    for p in range(IDX_HEADS // 2):
        pair = qi_ref[:, p * 128:(p + 1) * 128]
        zero = jnp.zeros_like(pair)
        qim_scr[p, 0:128, :] = jnp.where(col < IDX_DIM, pair, zero)
        qim_scr[p, 128:256, :] = jnp.where(col >= IDX_DIM, pair, zero)
    wis = wit_ref[...] * (IDX_DIM ** -0.5)
    key_minus_query = (lax.broadcasted_iota(I32, (KC, 128), 0) - lax.broadcasted_iota(I32, (KC, 128), 1))

    def idx_body(kc, carry):
        base = pl.multiple_of(kc * KC, KC)
        kk = ki2_ref[pl.ds(base, KC), :]
        acc = jnp.zeros((KC, 128), F32)
        for p in range(IDX_HEADS // 2):
            s = _dot_nt(kk, qim_scr[p])
            acc = (acc + jnp.maximum(s[:, :128], 0.0) * wis[2 * p:2 * p + 1, :]
                   + jnp.maximum(s[:, 128:], 0.0) * wis[2 * p + 1:2 * p + 2, :])
        inadmissible = key_minus_query > (jq * 128 - kc * KC)
        sc_scr[pl.ds(base, KC), :] = jnp.where(inadmissible, -jnp.inf, acc)
        return carry

    lax.fori_loop(0, n_kc, idx_body, 0)

    def count(pred):
        def body(kc, c):
            x = sc_scr[pl.ds(pl.multiple_of(kc * KC, KC), KC), :]
            m = jnp.where(pred(x), 1, 0).astype(I32)
            return c + jnp.sum(m.reshape(KC // 8, 8, 128), axis=0)
        c = lax.fori_loop(0, n_kc, body, jnp.zeros((8, 128), I32))
        return jnp.sum(c, axis=0, keepdims=True)

    def bit_body(i, t):
        cand = t ^ lax.shift_left(jnp.int32(1), 31 - i)
        cand_f = _key_to_float(cand)
        return jnp.where(count(lambda x: x >= cand_f) >= n_sel, cand, t)

    thr_key = lax.fori_loop(0, 32, bit_body, jnp.full((1, 128), INT_MIN, I32))
    thr = jnp.where(thr_key == jnp.int32(INT_MIN), -FLT_MAX, _key_to_float(thr_key))
    c_ge = count(lambda x: x >= thr)
    c_gt = count(lambda x: x > thr)
    has_tie_overflow = jnp.max(c_ge) > n_sel

    @pl.when(jnp.logical_not(has_tie_overflow))
    def _():
        def body(kc, carry):
            sl = pl.ds(pl.multiple_of(kc * KC, KC), KC)
            msk_scr[sl, :] = jnp.where(sc_scr[sl, :] >= thr, 0.0, -jnp.inf).astype(F32)
            return carry
        lax.fori_loop(0, n_kc, body, 0)

    @pl.when(has_tie_overflow)
    def _():
        need = (n_sel - c_gt).astype(F32)
        lstrict = jnp.where(col < row, 1.0, 0.0).astype(BF16)

        def body(kt, before):
            sl = pl.ds(pl.multiple_of(kt * 128, 128), 128)
            x = sc_scr[sl, :]
            eq = x == thr
            eqf = jnp.where(eq, 1.0, 0.0).astype(F32)
            rank = before + _dot(lstrict, eqf.astype(BF16))
            sel = jnp.logical_or(x > thr, jnp.logical_and(eq, rank < need))
            msk_scr[sl, :] = jnp.where(sel, 0.0, -jnp.inf).astype(F32)
            return before + jnp.sum(eqf, axis=0, keepdims=True)
        lax.fori_loop(0, n_kc * (KC // 128), body, jnp.zeros((1, 128), F32))

    gw = LANE_HEADS * 128
    n_lg = N_HEADS // LANE_HEADS
    lane_head = lax.broadcasted_iota(I32, (1, gw), 1) // 128
    q_grp, slope_vec = [], []
    for g in range(n_lg):
        heads = range(g * LANE_HEADS, (g + 1) * LANE_HEADS)
        q_grp.append(jnp.concatenate([q_ref[:, h * HEAD_DIM:(h + 1) * HEAD_DIM] for h in heads], axis=0))
        sv = jnp.zeros((1, gw), F32)
        for i, h in enumerate(heads):
            sv = jnp.where(lane_head == i, ALIBI_SLOPES[h] * LOG2E, sv)
        slope_vec.append(sv)
        acc_scr[g] = jnp.zeros((HEAD_DIM + DEN_ROWS, gw), F32)

    def att_body(ac, carry):
        base = pl.multiple_of(ac * AC, AC)
        off = (jq * 128 - ac * AC).astype(F32)
        mk = msk_scr[pl.ds(base, AC), :]
        mkw = jnp.concatenate([mk] * LANE_HEADS, axis=1)
        kv_of = [g * LANE_HEADS // GQA_GROUP for g in range(n_lg)]
        raw = [_dot_nt(k_ref[pl.ds(base, AC), kv_of[g] * HEAD_DIM:(kv_of[g] + 1) * HEAD_DIM], q_grp[g])
               for g in range(n_lg)]
        new = []
        for g in range(n_lg):
            kv = kv_of[g]
            m = carry[g]
            x = raw[g] * (ATT_SCALE * LOG2E) - ab_scr[g] + mkw
            cvec = slope_vec[g] * off
            m_new = jnp.maximum(m, jnp.max(x, axis=0, keepdims=True) - cvec)
            alpha = jnp.exp2(m - m_new)
            p = jnp.exp2(x - (m_new + cvec)).astype(BF16)
            vt_c = jnp.concatenate(
                [jnp.concatenate([vt_ref[ac * (AC // 128) + t, kv * HEAD_DIM:(kv + 1) * HEAD_DIM, :]
                                  for t in range(AC // 128)], axis=1),
                 jnp.ones((DEN_ROWS, AC), BF16)], axis=0)
            acc_scr[g] = alpha * acc_scr[g] + _dot(vt_c, p)
            new.append(m_new)
        return tuple(new)

    lax.fori_loop(0, n_ac, att_body, (jnp.full((1, gw), NEG_BIG, F32),) * n_lg)
    for g in range(n_lg):
        out_t = acc_scr[g, :HEAD_DIM, :] / acc_scr[g, HEAD_DIM:HEAD_DIM + 1, :]
        for i in range(LANE_HEADS):
            lo = A_WIDTH + (g * LANE_HEADS + i) * HEAD_DIM
            o_ref[:, lo:lo + HEAD_DIM] = out_t[:, i * 128:(i + 1) * 128].T.astype(o_ref.dtype)

    for g in range(A_GROUPS):
        sl = slice(g * A_CH, (g + 1) * A_CH)
        sp = _dot(ws_ref[g], vn_ref[:, sl]) + bsp_ref[:, sl]
        o_ref[:, sl] = (u_ref[:, sl] * sp).astype(o_ref.dtype)


def _mix(u, vn, q, qi, wit, kb, vt, ki2, ws, bsp, *, batch, seq, n_sel):
    n = batch * seq
    nblk = seq // Q_BLOCK
    ntile = seq // 128
    row_spec = lambda w: pl.BlockSpec((Q_BLOCK, w), lambda b, j: (b * nblk + j, 0))
    in_specs = [
        row_spec(A_WIDTH), row_spec(A_WIDTH), row_spec(ATT_WIDTH), row_spec(IDX_WIDTH),
        pl.BlockSpec((IDX_HEADS, Q_BLOCK), lambda b, j: (0, b * nblk + j)),
        pl.BlockSpec((seq, KV_WIDTH), lambda b, j: (b, 0)),
        pl.BlockSpec((ntile, KV_WIDTH, 128), lambda b, j: (b, 0, 0)),
        pl.BlockSpec((seq, 2 * IDX_DIM), lambda b, j: (b, 0)),
        pl.BlockSpec((A_GROUPS, CHUNK, CHUNK), lambda b, j: (0, 0, 0)),
        pl.BlockSpec((CHUNK, A_WIDTH), lambda b, j: (0, 0)),
    ]
    return pl.pallas_call(
        functools.partial(_mix_kernel, n_sel=n_sel),
        out_shape=jax.ShapeDtypeStruct((n, A_WIDTH + ATT_WIDTH), BF16),
        grid=(batch, nblk),
        in_specs=in_specs,
        out_specs=pl.BlockSpec((Q_BLOCK, A_WIDTH + ATT_WIDTH), lambda b, j: (b * nblk + j, 0)),
        scratch_shapes=[pltpu.VMEM((seq, 128), F32),
                        pltpu.VMEM((seq, 128), F32),
                        pltpu.VMEM((IDX_HEADS // 2, 256, 128), BF16),
                        pltpu.VMEM((N_HEADS // LANE_HEADS, AC, LANE_HEADS * 128), F32),
                        pltpu.VMEM((N_HEADS // LANE_HEADS, HEAD_DIM + DEN_ROWS, LANE_HEADS * 128), F32)],
        compiler_params=_params("arbitrary", "arbitrary"),
        name="mix",
    )(u, vn, q, qi, wit, kb, vt, ki2, ws, bsp)


def _page_copies(pt_ref, row, n_pages, slot, streams, *, for_wait):
    cps = []
    for p in range(n_pages):
        page = 0 if for_wait else pt_ref[row * n_pages + p]
        for hbm, buf, sem, place in streams:
            cps.append(pltpu.make_async_copy(hbm.at[page], buf.at[(slot,) + place(p)], sem.at[slot]))
    return cps


def _gather_pages(pt_ref, n_pages, streams):
    b = pl.program_id(0)
    slot = b % 2

    @pl.when(b == 0)
    def _():
        for c in _page_copies(pt_ref, 0, n_pages, 0, streams, for_wait=False):
            c.start()

    @pl.when(b + 1 < pl.num_programs(0))
    def _():
        for c in _page_copies(pt_ref, b + 1, n_pages, 1 - slot, streams, for_wait=False):
            c.start()

    for c in _page_copies(pt_ref, b, n_pages, slot, streams, for_wait=True):
        c.wait()
    return slot


def _s_scores_kernel(pt_ref, qi_ref, wi_ref, kin_ref, kidx_hbm, sc_ref, self_ref, kbuf, sem, *, n_pages):
    place = lambda p: (slice(None), pl.ds(p * PAGE_SIZE, PAGE_SIZE))
    slot = _gather_pages(pt_ref, n_pages, [(kidx_hbm, kbuf, sem, place)])
    qi = qi_ref[...]
    w = wi_ref[...] * (IDX_DIM ** -0.5)
    kcat = kbuf[slot].astype(BF16)
    s = _dot(qi, kcat)
    sc_ref[...] = jnp.sum(jnp.maximum(s, 0.0) * w, axis=0, keepdims=True)
    kin = kin_ref[...].astype(BF16).astype(F32)
    s_self = jnp.sum(qi.astype(F32) * kin, axis=1, keepdims=True)
    v_self = jnp.sum(jnp.maximum(s_self, 0.0) * w, axis=0, keepdims=True)
    self_ref[...] = jnp.broadcast_to(v_self, self_ref.shape)


def _s_scores(page_table_flat, qi3, wi_col, ki_new3, cache_kidx_l, *, n_pages):
    bd = qi3.shape[0]
    past = n_pages * PAGE_SIZE
    grid_spec = pltpu.PrefetchScalarGridSpec(
        num_scalar_prefetch=1,
        grid=(bd,),
        in_specs=[pl.BlockSpec((None, IDX_HEADS, IDX_DIM), lambda b, pt: (b, 0, 0)),
                  pl.BlockSpec((None, IDX_HEADS, 1), lambda b, pt: (b, 0, 0)),
                  pl.BlockSpec((None, 1, IDX_DIM), lambda b, pt: (b, 0, 0)),
                  pl.BlockSpec(memory_space=pl.ANY)],
        out_specs=(pl.BlockSpec((None, 1, past), lambda b, pt: (b, 0, 0)),
                   pl.BlockSpec((None, 1, 128), lambda b, pt: (b, 0, 0))),
        scratch_shapes=[pltpu.VMEM((2, IDX_DIM, past), F32), pltpu.SemaphoreType.DMA((2,))],
    )
    return pl.pallas_call(
        functools.partial(_s_scores_kernel, n_pages=n_pages),
        out_shape=(jax.ShapeDtypeStruct((bd, 1, past), F32),
                   jax.ShapeDtypeStruct((bd, 1, 128), F32)),
        grid_spec=grid_spec,
        compiler_params=_params("arbitrary"),
        name="s_scores",
    )(page_table_flat, qi3, wi_col, ki_new3, cache_kidx_l)


def _s_select_kernel(sc_ref, self_ref, u_ref, vn_ref, ws0_ref, bs0_ref,
                     sel_ref, selself_ref, a_ref, *, n_sel):
    keys = sc_ref[...]
    kself = self_ref[:, 0:1]
    bd, past = keys.shape

    def count_ge(cand):
        c = jnp.sum(jnp.where(keys >= cand, 1, 0).astype(I32), axis=1, keepdims=True)
        return c + jnp.where(kself >= cand, 1, 0).astype(I32)

    def bit_body(i, t):
        cand = t ^ lax.shift_left(jnp.int32(1), 31 - i)
        return jnp.where(count_ge(_key_to_float(cand)) >= n_sel, cand, t)

    thr = _key_to_float(lax.fori_loop(0, 32, bit_body, jnp.full((bd, 1), INT_MIN, I32)))
    c_gt = (jnp.sum(jnp.where(keys > thr, 1, 0).astype(I32), axis=1, keepdims=True)
            + jnp.where(kself > thr, 1, 0).astype(I32))
    need = (n_sel - c_gt).astype(F32)
    r = lax.broadcasted_iota(I32, (128, 128), 0)
    c = lax.broadcasted_iota(I32, (128, 128), 1)
    ustrict = jnp.where(r < c, 1.0, 0.0).astype(BF16)
    before = jnp.zeros((bd, 1), F32)
    for t in range(past // 128):
        kt = keys[:, t * 128:(t + 1) * 128]
        eq = kt == thr
        eqf = jnp.where(eq, 1.0, 0.0).astype(F32)
        rank = before + _dot(eqf.astype(BF16), ustrict)
        sel = jnp.logical_or(kt > thr, jnp.logical_and(eq, rank < need))
        sel_ref[:, t * 128:(t + 1) * 128] = jnp.where(sel, 1.0, 0.0).astype(F32)
        before = before + jnp.sum(eqf, axis=1, keepdims=True)
    sel_self = jnp.logical_or(kself > thr, jnp.logical_and(kself == thr, before < need))
    selself_ref[...] = jnp.broadcast_to(jnp.where(sel_self, 1.0, 0.0).astype(F32), selself_ref.shape)
    sp = ws0_ref[...] * vn_ref[...] + bs0_ref[...]
    a_ref[...] = (u_ref[...] * sp).astype(a_ref.dtype)


def _s_select(sc, sc_self, u, vn, ws0, bs0, *, n_sel):
    bd, past = sc.shape
    full = lambda a: pl.BlockSpec(a.shape, lambda i: (0,) * a.ndim)
    args = (sc, sc_self, u, vn, ws0, bs0)
    return pl.pallas_call(
        functools.partial(_s_select_kernel, n_sel=n_sel),
        out_shape=(jax.ShapeDtypeStruct((bd, past), F32),
                   jax.ShapeDtypeStruct((bd, 128), F32),
                   jax.ShapeDtypeStruct((bd, A_WIDTH), BF16)),
        grid=(1,),
        in_specs=[full(a) for a in args],
        out_specs=(pl.BlockSpec((bd, past), lambda i: (0, 0)),
                   pl.BlockSpec((bd, 128), lambda i: (0, 0)),
                   pl.BlockSpec((bd, A_WIDTH), lambda i: (0, 0))),
        compiler_params=_params("arbitrary"),
        name="s_select",
    )(*args)


def _s_attn_kernel(pt_ref, q_ref, sel_ref, selself_ref, knew_ref, vnew_ref, ck_hbm, cv_hbm, o_ref,
                   kbuf, vbuf, ksem, vsem, *, n_pages, past):
    place = lambda p: (pl.ds(p * 2 * PAGE_SIZE, 2 * PAGE_SIZE), slice(None))
    slot = _gather_pages(pt_ref, n_pages, [(ck_hbm, kbuf, ksem, place), (cv_hbm, vbuf, vsem, place)])
    q = q_ref[...]
    hrow = lax.broadcasted_iota(I32, (N_HEADS, 2 * past), 0)
    ccol = lax.broadcasted_iota(I32, (N_HEADS, 2 * past), 1)
    own_kv = (ccol & 1) == (hrow // GQA_GROUP)
    hcol = lax.broadcasted_iota(I32, (N_HEADS, 1), 0)
    slope = jnp.zeros((N_HEADS, 1), F32)
    for h in range(N_HEADS):
        slope = jnp.where(hcol == h, ALIBI_SLOPES[h], slope)

    lg = _dot_nt(q, kbuf[slot].astype(BF16)) * ATT_SCALE
    lg = lg - slope * (past - (ccol >> 1)).astype(F32)
    lg = jnp.where(jnp.logical_and(own_kv, sel_ref[...] > 0.5), lg, -jnp.inf)
    first_group = lax.broadcasted_iota(I32, (N_HEADS, HEAD_DIM), 0) < GQA_GROUP
    knew = knew_ref[...].astype(BF16).astype(F32)
    vnew = vnew_ref[...].astype(BF16).astype(F32)
    knew8 = jnp.where(first_group, knew[0:1, :], knew[1:2, :])
    vnew8 = jnp.where(first_group, vnew[0:1, :], vnew[1:2, :])
    lg_self = jnp.sum(q.astype(F32) * knew8, axis=1, keepdims=True) * ATT_SCALE
    lg_self = jnp.where(selself_ref[:, 0:1] > 0.5, lg_self, -jnp.inf)

    m = jnp.maximum(jnp.maximum(lg_self, NEG_BIG), jnp.max(lg, axis=1, keepdims=True))
    p_self = jnp.exp(lg_self - m)
    pp = jnp.exp(lg - m)
    l = p_self + jnp.sum(pp, axis=1, keepdims=True)
    acc = p_self.astype(BF16).astype(F32) * vnew8 + _dot(pp.astype(BF16), vbuf[slot].astype(BF16))
    o_ref[...] = (acc / l).astype(o_ref.dtype)


def _s_attn(page_table_flat, q3, sel2, sel_self3, k_new3, v_new3, cache_k2, cache_v2, *, n_pages, past):
    bd = q3.shape[0]
    per_b = lambda s1, s2: pl.BlockSpec((None, s1, s2), lambda b, pt: (b, 0, 0))
    grid_spec = pltpu.PrefetchScalarGridSpec(
        num_scalar_prefetch=1,
        grid=(bd,),
        in_specs=[per_b(N_HEADS, HEAD_DIM), per_b(1, 2 * past), per_b(1, 128),
                  per_b(N_KV_HEADS, HEAD_DIM), per_b(N_KV_HEADS, HEAD_DIM),
                  pl.BlockSpec(memory_space=pl.ANY), pl.BlockSpec(memory_space=pl.ANY)],
        out_specs=per_b(N_HEADS, HEAD_DIM),
        scratch_shapes=[pltpu.VMEM((2, 2 * past, HEAD_DIM), F32), pltpu.VMEM((2, 2 * past, HEAD_DIM), F32),
                        pltpu.SemaphoreType.DMA((2,)), pltpu.SemaphoreType.DMA((2,))],
    )
    return pl.pallas_call(
        functools.partial(_s_attn_kernel, n_pages=n_pages, past=past),
        out_shape=jax.ShapeDtypeStruct((bd, N_HEADS, HEAD_DIM), BF16),
        grid_spec=grid_spec,
        compiler_params=_params("arbitrary"),
        name="s_attn",
    )(page_table_flat, q3, sel2, sel_self3, k_new3, v_new3, cache_k2, cache_v2)


def _outln_kernel(mix_ref, x_ref, gate_ref, w_ref, g_ref, b_ref, o_ref, *, alpha):
    mix = _dot(mix_ref[...], w_ref[...])
    y = alpha * x_ref[...] + (1.0 + gate_ref[...]) * mix
    o_ref[...] = _layer_norm(y, g_ref[...], b_ref[...])


def _outln(mixin, x, gate, w, g, b, *, tm, alpha):
    n, d = x.shape
    nb, r, _ = gate.shape
    tiles_per_mod = (n // nb) // tm
    kin = mixin.shape[1]
    return pl.pallas_call(
        functools.partial(_outln_kernel, alpha=alpha),
        out_shape=jax.ShapeDtypeStruct((n, d), F32),
        grid=(n // tm,),
        in_specs=[pl.BlockSpec((tm, kin), lambda i: (i, 0)),
                  pl.BlockSpec((tm, d), lambda i: (i, 0)),
                  pl.BlockSpec((None, r, d), lambda i: (i // tiles_per_mod, 0, 0)),
                  pl.BlockSpec((kin, d), lambda i: (0, 0), pipeline_mode=pl.Buffered(1)),
                  pl.BlockSpec((1, d), lambda i: (0, 0)),
                  pl.BlockSpec((1, d), lambda i: (0, 0))],
        out_specs=pl.BlockSpec((tm, d), lambda i: (i, 0)),
        compiler_params=_params("arbitrary"),
        name="outln",
    )(mixin, x, gate, w, g, b)


def _ffn_kernel(x_ref, shift_ref, scale_ref, gate_ref, w1_ref, b1_ref, w2_ref, b2_ref, g_ref, b_ref,
                o_ref, h_scr, *, alpha):
    f = pl.program_id(1)

    @pl.when(f == 0)
    def _():
        h_scr[...] = (x_ref[...] * (1.0 + scale_ref[...]) + shift_ref[...]).astype(BF16)
        o_ref[...] = jnp.zeros_like(o_ref)

    a = jnp.maximum(_dot(h_scr[...], w1_ref[...]) + b1_ref[...], 0.0)
    a2 = (a * a).astype(BF16)
    for c in range(o_ref.shape[1] // FFN_TN):
        cs = slice(c * FFN_TN, (c + 1) * FFN_TN)
        o_ref[:, cs] += _dot(a2, w2_ref[:, cs])

    @pl.when(f == pl.num_programs(1) - 1)
    def _():
        y = alpha * x_ref[...] + (1.0 + gate_ref[...]) * (o_ref[...] + b2_ref[...])
        o_ref[...] = _layer_norm(y, g_ref[...], b_ref[...])


def _ffn(x, shift, scale, gate, w1, b1, w2, b2, g, b, *, tm, tf, alpha):
    n, d = x.shape
    nb, r, _ = gate.shape
    dff = w1.shape[1]
    tiles_per_mod = (n // nb) // tm
    mod_spec = pl.BlockSpec((None, r, d), lambda i, f: (i // tiles_per_mod, 0, 0))
    return pl.pallas_call(
        functools.partial(_ffn_kernel, alpha=alpha),
        out_shape=jax.ShapeDtypeStruct((n, d), F32),
        grid=(n // tm, dff // tf),
        in_specs=[pl.BlockSpec((tm, d), lambda i, f: (i, 0), pipeline_mode=pl.Buffered(1)),
                  mod_spec, mod_spec, mod_spec,
                  pl.BlockSpec((d, tf), lambda i, f: (0, f)),
                  pl.BlockSpec((1, tf), lambda i, f: (0, f)),
                  pl.BlockSpec((tf, d), lambda i, f: (f, 0)),
                  pl.BlockSpec((1, d), lambda i, f: (0, 0)),
                  pl.BlockSpec((1, d), lambda i, f: (0, 0)),
                  pl.BlockSpec((1, d), lambda i, f: (0, 0))],
        out_specs=pl.BlockSpec((tm, d), lambda i, f: (i, 0)),
        scratch_shapes=[pltpu.VMEM((tm, d), BF16)],
        compiler_params=_params("arbitrary", "arbitrary"),
        name="ffn",
    )(x, shift, scale, gate, w1, b1, w2, b2, g, b)


def kernel(x_prompt, x_sample, cache_k, cache_v, cache_kidx, page_table, c_prompt, c_sample,
           w_cond, b_cond, w_in, ln_v_g, ln_v_b, w_spatial, b_spatial, w_out,
           ln1_g, ln1_b, w_ff1, b_ff1, w_ff2, b_ff2, ln2_g, ln2_b):
    batch, seq, d = x_prompt.shape
    bd, ts, _ = x_sample.shape
    depth = w_in.shape[0]
    n_pages = page_table.shape[1]
    past = n_pages * PAGE_SIZE
    n_pool = cache_k.shape[1]
    dff = w_ff1.shape[2]
    assert ts == 1 and seq % KC == 0 and d == A_WIDTH + ATT_WIDTH
    assert w_in.shape[2] == MAIN_WIDTH + IDX_DIM + IDX_HEADS
    n = batch * seq
    alpha = (2 * depth) ** 0.25
    n_sel_p = min(TOPK_MAX, seq // 4)
    n_sel_s = min(TOPK_MAX, (past + ts) // 4)
    tm_p = min(1024, seq)
    tm_o = min(512, seq)
    tf = min(1024, dff)
    pt_flat = page_table.reshape(-1).astype(I32)

    xp = x_prompt.reshape(n, d)
    xs = x_sample.reshape(bd, d)
    c_all = jnp.concatenate([c_prompt, c_sample], axis=0)

    outs = {k: [] for k in ("kp", "vp", "kip", "ks", "vs", "kis", "vc")}
    for l in range(depth):
        wm = jnp.swapaxes(w_in[l], 0, 1)
        wt = jnp.pad(wm[MAIN_WIDTH:], ((0, 2 * IDX_DIM - IDX_DIM - IDX_HEADS), (0, 0)))
        kidx_t = jnp.swapaxes(cache_kidx[l], 1, 2)
        lng = ln_v_g[l].reshape(1, A_WIDTH)
        lnb = ln_v_b[l].reshape(1, A_WIDTH)
        tril = jnp.tril(jnp.ones((CHUNK, CHUNK), dtype=bool))
        ws = jnp.where(tril[None], w_spatial[l], 0.0).astype(BF16)
        bsp = jnp.repeat(jnp.transpose(b_spatial[l]), A_CH, axis=1)
        ws0 = jnp.repeat(w_spatial[l][:, 0, 0], A_CH).reshape(1, A_WIDTH)
        bs0 = jnp.repeat(b_spatial[l][:, 0], A_CH).reshape(1, A_WIDTH)
        w_out_b = w_out[l].astype(BF16)
        w1_b = w_ff1[l].astype(BF16)
        w2_b = w_ff2[l].astype(BF16)
        b1 = b_ff1[l].reshape(1, dff)
        b2 = b_ff2[l].reshape(1, d)
        g1, be1 = ln1_g[l].reshape(1, d), ln1_b[l].reshape(1, d)
        g2, be2 = ln2_g[l].reshape(1, d), ln2_b[l].reshape(1, d)

        z = _cond(c_all, w_cond[l], b_cond[l])
        mods = [z[:, i * d:(i + 1) * d] for i in range(N_MOD)]
        mp = [m[:batch].reshape(batch, 1, d) for m in mods]
        ms = [m[batch:].reshape(1, bd, d) for m in mods]

        (u, vn, q, k, v, kb, vt, qi, ki, ki2, wit) = _proj(
            xp, mp[0], mp[1], wm, wt, lng, lnb, tm=tm_p, vn_dtype=BF16)
        mixin = _mix(u, vn, q, qi, wit, kb, vt, ki2, ws, bsp, batch=batch, seq=seq, n_sel=n_sel_p)
        x1 = _outln(mixin, xp, mp[2], w_out_b, g1, be1, tm=tm_o, alpha=alpha)
        xp = _ffn(x1, mp[3], mp[4], mp[5], w1_b, b1, w2_b, b2, g2, be2, tm=tm_p, tf=tf, alpha=alpha)
        outs["kp"].append(k.reshape(batch, seq, N_KV_HEADS, HEAD_DIM))
        outs["vp"].append(v.reshape(batch, seq, N_KV_HEADS, HEAD_DIM))
        outs["kip"].append(ki.reshape(batch, seq, IDX_DIM))

        (u, vn, q, k, v, _, _, qi, ki, _, wit) = _proj(
            xs, ms[0], ms[1], wm, wt, lng, lnb, tm=bd, vn_dtype=F32)
        sc, sc_self = _s_scores(pt_flat, qi.reshape(bd, IDX_HEADS, IDX_DIM),
                                jnp.transpose(wit).reshape(bd, IDX_HEADS, 1),
                                ki.reshape(bd, 1, IDX_DIM), kidx_t, n_pages=n_pages)
        sel, sel_self, a_out = _s_select(sc.reshape(bd, past), sc_self.reshape(bd, 128), u, vn, ws0, bs0,
                                         n_sel=n_sel_s)
        sel2 = jnp.repeat(sel, 2, axis=1).reshape(bd, 1, 2 * past)
        b_out = _s_attn(pt_flat, q.reshape(bd, N_HEADS, HEAD_DIM), sel2, sel_self.reshape(bd, 1, 128),
                        k.reshape(bd, N_KV_HEADS, HEAD_DIM), v.reshape(bd, N_KV_HEADS, HEAD_DIM),
                        cache_k[l].reshape(n_pool, 2 * PAGE_SIZE, HEAD_DIM),
                        cache_v[l].reshape(n_pool, 2 * PAGE_SIZE, HEAD_DIM),
                        n_pages=n_pages, past=past)
        mixin = jnp.concatenate([a_out, b_out.reshape(bd, ATT_WIDTH)], axis=1)
        x1 = _outln(mixin, xs, ms[2], w_out_b, g1, be1, tm=bd, alpha=alpha)
        xs = _ffn(x1, ms[3], ms[4], ms[5], w1_b, b1, w2_b, b2, g2, be2, tm=bd, tf=tf, alpha=alpha)
        outs["ks"].append(k.reshape(bd, ts, N_KV_HEADS, HEAD_DIM))
        outs["vs"].append(v.reshape(bd, ts, N_KV_HEADS, HEAD_DIM))
        outs["kis"].append(ki.reshape(bd, ts, IDX_DIM))
        outs["vc"].append(vn.reshape(bd, ts, A_GROUPS, A_CH))

    st = lambda name: jnp.stack(outs[name])
    return (xp.reshape(batch, seq, d), xs.reshape(bd, ts, d),
            st("kp"), st("vp"), st("kip"), st("ks"), st("vs"), st("kis"), st("vc"))
```

```python
import functools

import jax
import jax.numpy as jnp
import numpy as np
from jax import lax
from jax.experimental import pallas as pl
from jax.experimental.pallas import tpu as pltpu

F32 = jnp.float32
BF16 = jnp.bfloat16
I32 = jnp.int32

CHUNK = 128
A_GROUPS = 8
A_CH = 128
A_WIDTH = A_GROUPS * A_CH
HEAD_DIM = 128
N_HEADS = 8
N_KV_HEADS = 2
GQA_GROUP = N_HEADS // N_KV_HEADS
ATT_WIDTH = N_HEADS * HEAD_DIM
KV_WIDTH = N_KV_HEADS * HEAD_DIM
IDX_HEADS = 16
IDX_DIM = 64
IDX_WIDTH = IDX_HEADS * IDX_DIM
TOPK_MAX = 256
Q_BLOCK = 128
PAGE_SIZE = 128
N_MOD = 6
LN_EPS = 1e-5
ATT_SCALE = HEAD_DIM ** -0.5
MAIN_WIDTH = 2 * A_WIDTH + ATT_WIDTH + 2 * KV_WIDTH + IDX_WIDTH
PROJ_TN = 512
FFN_TN = 512
ALIBI_SLOPES = tuple(float(2.0 ** (-8.0 * h / N_HEADS)) for h in range(1, N_HEADS + 1))

VMEM_LIMIT_BYTES = 58 * 1024 * 1024
INT_MIN = -(2 ** 31)
NEG_BIG = -1e30
FLT_MAX = float(np.finfo(np.float32).max)
LOG2E = float(np.log2(np.e))
KC = 512
AC = 512
S_SCORES_SLOTS = 4
S_ATTN_SLOTS = 3
LANE_HEADS = 4
DEN_ROWS = 16

NT_DIMS = (((1,), (1,)), ((), ()))


def _dot(a, b):
    return jnp.dot(a, b, preferred_element_type=F32)


def _dot_nt(a, b):
    return lax.dot_general(a, b, NT_DIMS, preferred_element_type=F32)


def _layer_norm(x, g, b):
    mu = jnp.mean(x, axis=-1, keepdims=True)
    xc = x - mu
    var = jnp.mean(xc * xc, axis=-1, keepdims=True)
    return xc * lax.rsqrt(var + LN_EPS) * g + b


def _key_to_float(key):
    bits = jnp.where(key < 0, key ^ jnp.int32(0x7FFFFFFF), key)
    return pltpu.bitcast(bits, F32)


def _params(*sem):
    return pltpu.CompilerParams(dimension_semantics=sem, vmem_limit_bytes=VMEM_LIMIT_BYTES)


def _cond_kernel(c_ref, w_ref, b_ref, o_ref):
    c = c_ref[...]
    a = (c * jax.nn.sigmoid(c)).astype(BF16)
    o_ref[...] = _dot(a, w_ref[...].astype(BF16)) + b_ref[...]


def _cond(c, w, b):
    m, d = c.shape
    n = w.shape[1]
    tn = 1024
    return pl.pallas_call(
        _cond_kernel,
        out_shape=jax.ShapeDtypeStruct((m, n), F32),
        grid=(n // tn,),
        in_specs=[pl.BlockSpec((m, d), lambda j: (0, 0)),
                  pl.BlockSpec((d, tn), lambda j: (0, j)),
                  pl.BlockSpec((1, tn), lambda j: (0, j))],
        out_specs=pl.BlockSpec((m, tn), lambda j: (0, j)),
        compiler_params=_params("arbitrary"),
        name="cond",
    )(c, w, b.reshape(1, n))


_J_V, _J_Q, _J_KV, _J_QI, _J_END = 2, 4, 6, 7, 9


def _proj_kernel(x_ref, shift_ref, scale_ref, wm_ref, wt_ref, lng_ref, lnb_ref,
                 u_ref, vn_ref, q_ref, k_ref, v_ref, kb_ref, vt_ref, qi_ref, ki_ref, ki2_ref, wit_ref,
                 h_scr):
    j = pl.program_id(1)

    @pl.when(j == 0)
    def _():
        h = (x_ref[...] * (1.0 + scale_ref[...]) + shift_ref[...]).astype(BF16)
        h_scr[...] = h
        tail = _dot_nt(h, wt_ref[...])
        ki_ref[...] = tail[:, :IDX_DIM]
        lane = lax.broadcasted_iota(I32, tail.shape, 1)
        ki2_ref[...] = jnp.where(lane < IDX_DIM, tail, pltpu.roll(tail, IDX_DIM, axis=1)).astype(BF16)
        wit_ref[...] = tail.T[IDX_DIM:IDX_DIM + IDX_HEADS, :] * (IDX_HEADS ** -0.5)

    z = _dot_nt(h_scr[...], wm_ref[...])

    @pl.when(j < _J_V)
    def _():
        u_ref[...] = z

    @pl.when(jnp.logical_and(j >= _J_V, j < _J_Q))
    def _():
        for g in range(PROJ_TN // A_CH):
            sl = slice(g * A_CH, (g + 1) * A_CH)
            vn_ref[:, sl] = _layer_norm(z[:, sl], lng_ref[:, sl], lnb_ref[:, sl]).astype(vn_ref.dtype)

    @pl.when(jnp.logical_and(j >= _J_Q, j < _J_KV))
    def _():
        q_ref[...] = z.astype(BF16)

    @pl.when(j == _J_KV)
    def _():
        for hh in range(N_KV_HEADS):
            k_ref[:, hh, :] = z[:, hh * HEAD_DIM:(hh + 1) * HEAD_DIM]
            v_ref[:, hh, :] = z[:, KV_WIDTH + hh * HEAD_DIM:KV_WIDTH + (hh + 1) * HEAD_DIM]
        kb_ref[...] = z[:, :KV_WIDTH].astype(BF16)
        vt = z[:, KV_WIDTH:].T.astype(BF16)
        for c in range(vt_ref.shape[0]):
            vt_ref[c] = vt[:, c * 128:(c + 1) * 128]

    @pl.when(j >= _J_QI)
    def _():
        qi_ref[...] = z.astype(BF16)


def _proj(x, shift, scale, wm, wt, lng, lnb, *, tm, vn_dtype):
    n, d = x.shape
    nb, r, _ = shift.shape
    rows_per_mod = n // nb
    assert n % tm == 0 and rows_per_mod % tm == 0 and r in (1, tm)
    tiles_per_mod = rows_per_mod // tm
    tn = PROJ_TN

    def clipj(lo, cnt):
        return lambda i, j: (i, jnp.clip(j - lo, 0, cnt - 1))

    mod_spec = pl.BlockSpec((None, r, d), lambda i, j: (i // tiles_per_mod, 0, 0))
    out_shape = (
        jax.ShapeDtypeStruct((n, A_WIDTH), F32),
        jax.ShapeDtypeStruct((n, A_WIDTH), vn_dtype),
        jax.ShapeDtypeStruct((n, ATT_WIDTH), BF16),
        jax.ShapeDtypeStruct((n, N_KV_HEADS, HEAD_DIM), F32),
        jax.ShapeDtypeStruct((n, N_KV_HEADS, HEAD_DIM), F32),
        jax.ShapeDtypeStruct((n, KV_WIDTH), BF16),
        jax.ShapeDtypeStruct((n // 128, KV_WIDTH, 128), BF16),
        jax.ShapeDtypeStruct((n, IDX_WIDTH), BF16),
        jax.ShapeDtypeStruct((n, IDX_DIM), F32),
        jax.ShapeDtypeStruct((n, 2 * IDX_DIM), BF16),
        jax.ShapeDtypeStruct((IDX_HEADS, n), F32),
    )
    out_specs = (
        pl.BlockSpec((tm, tn), clipj(0, 2)),
        pl.BlockSpec((tm, tn), clipj(_J_V, 2)),
        pl.BlockSpec((tm, tn), clipj(_J_Q, 2)),
        pl.BlockSpec((tm, N_KV_HEADS, HEAD_DIM), lambda i, j: (i, 0, 0)),
        pl.BlockSpec((tm, N_KV_HEADS, HEAD_DIM), lambda i, j: (i, 0, 0)),
        pl.BlockSpec((tm, KV_WIDTH), lambda i, j: (i, 0)),
        pl.BlockSpec((tm // 128, KV_WIDTH, 128), lambda i, j: (i, 0, 0)),
        pl.BlockSpec((tm, tn), clipj(_J_QI, 2)),
        pl.BlockSpec((tm, IDX_DIM), lambda i, j: (i, 0)),
        pl.BlockSpec((tm, 2 * IDX_DIM), lambda i, j: (i, 0)),
        pl.BlockSpec((IDX_HEADS, tm), lambda i, j: (0, i)),
    )
    in_specs = [
        pl.BlockSpec((tm, d), lambda i, j: (i, 0)),
        mod_spec, mod_spec,
        pl.BlockSpec((tn, d), lambda i, j: (j, 0)),
        pl.BlockSpec((2 * IDX_DIM, d), lambda i, j: (0, 0)),
        pl.BlockSpec((1, tn), lambda i, j: (0, jnp.clip(j - _J_V, 0, 1))),
        pl.BlockSpec((1, tn), lambda i, j: (0, jnp.clip(j - _J_V, 0, 1))),
    ]
    return pl.pallas_call(
        _proj_kernel,
        out_shape=out_shape,
        grid=(n // tm, _J_END),
        in_specs=in_specs,
        out_specs=out_specs,
        scratch_shapes=[pltpu.VMEM((tm, d), BF16)],
        compiler_params=_params("arbitrary", "arbitrary"),
        name="proj",
    )(x, shift, scale, wm, wt, lng, lnb)


def _mix_kernel(u_ref, vn_ref, q_ref, qi_ref, wit_ref, k_ref, vt_ref, ki2_ref, ws_ref, bsp_ref,
                o_ref, sc_scr, msk_scr, qim_scr, ab_scr, acc_scr, *, n_sel):
    jq = pl.program_id(1)
    n_kc = jq // (KC // 128) + 1
    n_ac = jq // (AC // 128) + 1
    row = lax.broadcasted_iota(I32, (128, 128), 0)
    col = lax.broadcasted_iota(I32, (128, 128), 1)

    @pl.when(jnp.logical_and(pl.program_id(0) == 0, jq == 0))
    def _():
        d0 = (lax.broadcasted_iota(I32, (AC, 128), 1) - lax.broadcasted_iota(I32, (AC, 128), 0)).astype(F32)
        for h in range(N_HEADS):
            lanes = slice((h % LANE_HEADS) * 128, (h % LANE_HEADS + 1) * 128)
            ab_scr[h // LANE_HEADS, :, lanes] = (ALIBI_SLOPES[h] * LOG2E) * d0

    for p in range(IDX_HEADS // 2):
        pair = qi_ref[:, p * 128:(p + 1) * 128]
        zero = jnp.zeros_like(pair)
        qim_scr[p, 0:128, :] = jnp.where(col < IDX_DIM, pair, zero)
        qim_scr[p, 128:256, :] = jnp.where(col >= IDX_DIM, pair, zero)
    wis = wit_ref[...] * (IDX_DIM ** -0.5)
    key_minus_query = (lax.broadcasted_iota(I32, (KC, 128), 0) - lax.broadcasted_iota(I32, (KC, 128), 1))

    def idx_body(kc, carry):
        base = pl.multiple_of(kc * KC, KC)
        kk = ki2_ref[pl.ds(base, KC), :]
        acc = jnp.zeros((KC, 128), F32)
        for p in range(IDX_HEADS // 2):
            s = _dot_nt(kk, qim_scr[p])
            acc = (acc + jnp.maximum(s[:, :128], 0.0) * wis[2 * p:2 * p + 1, :]
                   + jnp.maximum(s[:, 128:], 0.0) * wis[2 * p + 1:2 * p + 2, :])
        inadmissible = key_minus_query > (jq * 128 - kc * KC)
        sc_scr[pl.ds(base, KC), :] = jnp.where(inadmissible, -jnp.inf, acc)
        return carry

    lax.fori_loop(0, n_kc, idx_body, 0)

    def count(pred):
        def body(kc, c):
            x = sc_scr[pl.ds(pl.multiple_of(kc * KC, KC), KC), :]
            m = jnp.where(pred(x), 1, 0).astype(I32)
            return c + jnp.sum(m.reshape(KC // 8, 8, 128), axis=0)
        c = lax.fori_loop(0, n_kc, body, jnp.zeros((8, 128), I32))
        return jnp.sum(c, axis=0, keepdims=True)

    def bit_body(i, t):
        cand = t ^ lax.shift_left(jnp.int32(1), 31 - i)
        cand_f = _key_to_float(cand)
        return jnp.where(count(lambda x: x >= cand_f) >= n_sel, cand, t)

    thr_key = lax.fori_loop(0, 32, bit_body, jnp.full((1, 128), INT_MIN, I32))
    thr = jnp.where(thr_key == jnp.int32(INT_MIN), -FLT_MAX, _key_to_float(thr_key))
    c_ge = count(lambda x: x >= thr)
    c_gt = count(lambda x: x > thr)
    has_tie_overflow = jnp.max(c_ge) > n_sel

    @pl.when(jnp.logical_not(has_tie_overflow))
    def _():
        def body(kc, carry):
            sl = pl.ds(pl.multiple_of(kc * KC, KC), KC)
            msk_scr[sl, :] = jnp.where(sc_scr[sl, :] >= thr, 0.0, -jnp.inf).astype(F32)
            return carry
        lax.fori_loop(0, n_kc, body, 0)

    @pl.when(has_tie_overflow)
    def _():
        need = (n_sel - c_gt).astype(F32)
        lstrict = jnp.where(col < row, 1.0, 0.0).astype(BF16)

        def body(kt, before):
            sl = pl.ds(pl.multiple_of(kt * 128, 128), 128)
            x = sc_scr[sl, :]
            eq = x == thr
            eqf = jnp.where(eq, 1.0, 0.0).astype(F32)
            rank = before + _dot(lstrict, eqf.astype(BF16))
            sel = jnp.logical_or(x > thr, jnp.logical_and(eq, rank < need))
            msk_scr[sl, :] = jnp.where(sel, 0.0, -jnp.inf).astype(F32)
            return before + jnp.sum(eqf, axis=0, keepdims=True)
        lax.fori_loop(0, n_kc * (KC // 128), body, jnp.zeros((1, 128), F32))

    gw = LANE_HEADS * 128
    n_lg = N_HEADS // LANE_HEADS
    lane_head = lax.broadcasted_iota(I32, (1, gw), 1) // 128
    q_grp, slope_vec = [], []
    for g in range(n_lg):
        heads = range(g * LANE_HEADS, (g + 1) * LANE_HEADS)
        q_grp.append(jnp.concatenate([q_ref[:, h * HEAD_DIM:(h + 1) * HEAD_DIM] for h in heads], axis=0))
        sv = jnp.zeros((1, gw), F32)
        for i, h in enumerate(heads):
            sv = jnp.where(lane_head == i, ALIBI_SLOPES[h] * LOG2E, sv)
        slope_vec.append(sv)
        acc_scr[g] = jnp.zeros((HEAD_DIM + DEN_ROWS, gw), F32)

    def att_body(ac, carry):
        base = pl.multiple_of(ac * AC, AC)
        off = (jq * 128 - ac * AC).astype(F32)
        mk = msk_scr[pl.ds(base, AC), :]
        mkw = jnp.concatenate([mk] * LANE_HEADS, axis=1)
        kv_of = [g * LANE_HEADS // GQA_GROUP for g in range(n_lg)]
        raw = [_dot_nt(k_ref[pl.ds(base, AC), kv_of[g] * HEAD_DIM:(kv_of[g] + 1) * HEAD_DIM], q_grp[g])
               for g in range(n_lg)]
        new = []
        for g in range(n_lg):
            kv = kv_of[g]
            m = carry[g]
            x = raw[g] * (ATT_SCALE * LOG2E) - ab_scr[g] + mkw
            cvec = slope_vec[g] * off
            m_new = jnp.maximum(m, jnp.max(x, axis=0, keepdims=True) - cvec)
            alpha = jnp.exp2(m - m_new)
            p = jnp.exp2(x - (m_new + cvec)).astype(BF16)
            vt_c = jnp.concatenate(
                [jnp.concatenate([vt_ref[ac * (AC // 128) + t, kv * HEAD_DIM:(kv + 1) * HEAD_DIM, :]
                                  for t in range(AC // 128)], axis=1),
                 jnp.ones((DEN_ROWS, AC), BF16)], axis=0)
            acc_scr[g] = alpha * acc_scr[g] + _dot(vt_c, p)
            new.append(m_new)
        return tuple(new)

    lax.fori_loop(0, n_ac, att_body, (jnp.full((1, gw), NEG_BIG, F32),) * n_lg)
    for g in range(n_lg):
        out_t = acc_scr[g, :HEAD_DIM, :] / acc_scr[g, HEAD_DIM:HEAD_DIM + 1, :]
        for i in range(LANE_HEADS):
            lo = A_WIDTH + (g * LANE_HEADS + i) * HEAD_DIM
            o_ref[:, lo:lo + HEAD_DIM] = out_t[:, i * 128:(i + 1) * 128].T.astype(o_ref.dtype)

    for g in range(A_GROUPS):
        sl = slice(g * A_CH, (g + 1) * A_CH)
        sp = _dot(ws_ref[g], vn_ref[:, sl]) + bsp_ref[:, sl]
        o_ref[:, sl] = (u_ref[:, sl] * sp).astype(o_ref.dtype)


def _mix(u, vn, q, qi, wit, kb, vt, ki2, ws, bsp, *, batch, seq, n_sel):
    n = batch * seq
    nblk = seq // Q_BLOCK
    ntile = seq // 128
    row_spec = lambda w: pl.BlockSpec((Q_BLOCK, w), lambda b, j: (b * nblk + j, 0))
    in_specs = [
        row_spec(A_WIDTH), row_spec(A_WIDTH), row_spec(ATT_WIDTH), row_spec(IDX_WIDTH),
        pl.BlockSpec((IDX_HEADS, Q_BLOCK), lambda b, j: (0, b * nblk + j)),
        pl.BlockSpec((seq, KV_WIDTH), lambda b, j: (b, 0)),
        pl.BlockSpec((ntile, KV_WIDTH, 128), lambda b, j: (b, 0, 0)),
        pl.BlockSpec((seq, 2 * IDX_DIM), lambda b, j: (b, 0)),
        pl.BlockSpec((A_GROUPS, CHUNK, CHUNK), lambda b, j: (0, 0, 0)),
        pl.BlockSpec((CHUNK, A_WIDTH), lambda b, j: (0, 0)),
    ]
    return pl.pallas_call(
        functools.partial(_mix_kernel, n_sel=n_sel),
        out_shape=jax.ShapeDtypeStruct((n, A_WIDTH + ATT_WIDTH), BF16),
        grid=(batch, nblk),
        in_specs=in_specs,
        out_specs=pl.BlockSpec((Q_BLOCK, A_WIDTH + ATT_WIDTH), lambda b, j: (b * nblk + j, 0)),
        scratch_shapes=[pltpu.VMEM((seq, 128), F32),
                        pltpu.VMEM((seq, 128), F32),
                        pltpu.VMEM((IDX_HEADS // 2, 256, 128), BF16),
                        pltpu.VMEM((N_HEADS // LANE_HEADS, AC, LANE_HEADS * 128), F32),
                        pltpu.VMEM((N_HEADS // LANE_HEADS, HEAD_DIM + DEN_ROWS, LANE_HEADS * 128), F32)],
        compiler_params=_params("arbitrary", "arbitrary"),
        name="mix",
    )(u, vn, q, qi, wit, kb, vt, ki2, ws, bsp)


def _page_copies(pt_ref, row, n_pages, slot, streams, *, for_wait):
    cps = []
    for p in range(n_pages):
        page = 0 if for_wait else pt_ref[row * n_pages + p]
        for hbm, buf, sem, place in streams:
            cps.append(pltpu.make_async_copy(hbm.at[page], buf.at[(slot,) + place(p)], sem.at[slot]))
    return cps


def _gather_pages(pt_ref, n_pages, streams, n_slots):
    b = pl.program_id(0)
    ahead = n_slots - 1

    @pl.when(b == 0)
    def _():
        for r in range(ahead):
            for c in _page_copies(pt_ref, r, n_pages, r, streams, for_wait=False):
                c.start()

    @pl.when(b + ahead < pl.num_programs(0))
    def _():
        for c in _page_copies(pt_ref, b + ahead, n_pages, (b + ahead) % n_slots, streams, for_wait=False):
            c.start()

    slot = b % n_slots
    for c in _page_copies(pt_ref, b, n_pages, slot, streams, for_wait=True):
        c.wait()
    return slot


def _s_scores_kernel(pt_ref, qi_ref, wi_ref, kin_ref, kidx_hbm, sc_ref, self_ref, kbuf, sem, *, n_pages):
    place = lambda p: (slice(None), pl.ds(p * PAGE_SIZE, PAGE_SIZE))
    slot = _gather_pages(pt_ref, n_pages, [(kidx_hbm, kbuf, sem, place)], S_SCORES_SLOTS)
    qi = qi_ref[...]
    w = wi_ref[...] * (IDX_DIM ** -0.5)
    kcat = kbuf[slot].astype(BF16)
    s = _dot(qi, kcat)
    sc_ref[...] = jnp.sum(jnp.maximum(s, 0.0) * w, axis=0, keepdims=True)
    kin = kin_ref[...].astype(BF16).astype(F32)
    s_self = jnp.sum(qi.astype(F32) * kin, axis=1, keepdims=True)
    v_self = jnp.sum(jnp.maximum(s_self, 0.0) * w, axis=0, keepdims=True)
    self_ref[...] = jnp.broadcast_to(v_self, self_ref.shape)


def _s_scores(page_table_flat, qi3, wi_col, ki_new3, cache_kidx_l, *, n_pages):
    bd = qi3.shape[0]
    assert bd >= S_SCORES_SLOTS
    past = n_pages * PAGE_SIZE
    grid_spec = pltpu.PrefetchScalarGridSpec(
        num_scalar_prefetch=1,
        grid=(bd,),
        in_specs=[pl.BlockSpec((None, IDX_HEADS, IDX_DIM), lambda b, pt: (b, 0, 0)),
                  pl.BlockSpec((None, IDX_HEADS, 1), lambda b, pt: (b, 0, 0)),
                  pl.BlockSpec((None, 1, IDX_DIM), lambda b, pt: (b, 0, 0)),
                  pl.BlockSpec(memory_space=pl.ANY)],
        out_specs=(pl.BlockSpec((None, 1, past), lambda b, pt: (b, 0, 0)),
                   pl.BlockSpec((None, 1, 128), lambda b, pt: (b, 0, 0))),
        scratch_shapes=[pltpu.VMEM((S_SCORES_SLOTS, IDX_DIM, past), F32),
                        pltpu.SemaphoreType.DMA((S_SCORES_SLOTS,))],
    )
    return pl.pallas_call(
        functools.partial(_s_scores_kernel, n_pages=n_pages),
        out_shape=(jax.ShapeDtypeStruct((bd, 1, past), F32),
                   jax.ShapeDtypeStruct((bd, 1, 128), F32)),
        grid_spec=grid_spec,
        compiler_params=_params("arbitrary"),
        name="s_scores",
    )(page_table_flat, qi3, wi_col, ki_new3, cache_kidx_l)


def _s_select_kernel(sc_ref, self_ref, u_ref, vn_ref, ws0_ref, bs0_ref,
                     sel_ref, selself_ref, a_ref, *, n_sel):
    keys = sc_ref[...]
    kself = self_ref[:, 0:1]
    bd, past = keys.shape

    def count_ge(cand):
        c = jnp.sum(jnp.where(keys >= cand, 1, 0).astype(I32), axis=1, keepdims=True)
        return c + jnp.where(kself >= cand, 1, 0).astype(I32)

    def bit_body(i, t):
        cand = t ^ lax.shift_left(jnp.int32(1), 31 - i)
        return jnp.where(count_ge(_key_to_float(cand)) >= n_sel, cand, t)

    thr = _key_to_float(lax.fori_loop(0, 32, bit_body, jnp.full((bd, 1), INT_MIN, I32)))
    c_gt = (jnp.sum(jnp.where(keys > thr, 1, 0).astype(I32), axis=1, keepdims=True)
            + jnp.where(kself > thr, 1, 0).astype(I32))
    need = (n_sel - c_gt).astype(F32)
    r = lax.broadcasted_iota(I32, (128, 128), 0)
    c = lax.broadcasted_iota(I32, (128, 128), 1)
    ustrict = jnp.where(r < c, 1.0, 0.0).astype(BF16)
    before = jnp.zeros((bd, 1), F32)
    for t in range(past // 128):
        kt = keys[:, t * 128:(t + 1) * 128]
        eq = kt == thr
        eqf = jnp.where(eq, 1.0, 0.0).astype(F32)
        rank = before + _dot(eqf.astype(BF16), ustrict)
        sel = jnp.logical_or(kt > thr, jnp.logical_and(eq, rank < need))
        sel_ref[:, t * 128:(t + 1) * 128] = jnp.where(sel, 1.0, 0.0).astype(F32)
        before = before + jnp.sum(eqf, axis=1, keepdims=True)
    sel_self = jnp.logical_or(kself > thr, jnp.logical_and(kself == thr, before < need))
    selself_ref[...] = jnp.broadcast_to(jnp.where(sel_self, 1.0, 0.0).astype(F32), selself_ref.shape)
    sp = ws0_ref[...] * vn_ref[...] + bs0_ref[...]
    a_ref[...] = (u_ref[...] * sp).astype(a_ref.dtype)


def _s_select(sc, sc_self, u, vn, ws0, bs0, *, n_sel):
    bd, past = sc.shape
    full = lambda a: pl.BlockSpec(a.shape, lambda i: (0,) * a.ndim)
    args = (sc, sc_self, u, vn, ws0, bs0)
    return pl.pallas_call(
        functools.partial(_s_select_kernel, n_sel=n_sel),
        out_shape=(jax.ShapeDtypeStruct((bd, past), F32),
                   jax.ShapeDtypeStruct((bd, 128), F32),
                   jax.ShapeDtypeStruct((bd, A_WIDTH), BF16)),
        grid=(1,),
        in_specs=[full(a) for a in args],
        out_specs=(pl.BlockSpec((bd, past), lambda i: (0, 0)),
                   pl.BlockSpec((bd, 128), lambda i: (0, 0)),
                   pl.BlockSpec((bd, A_WIDTH), lambda i: (0, 0))),
        compiler_params=_params("arbitrary"),
        name="s_select",
    )(*args)


def _s_attn_kernel(pt_ref, q_ref, sel_ref, selself_ref, knew_ref, vnew_ref, ck_hbm, cv_hbm, o_ref,
                   kbuf, vbuf, ksem, vsem, *, n_pages, past):
    place = lambda p: (pl.ds(p * 2 * PAGE_SIZE, 2 * PAGE_SIZE), slice(None))
    slot = _gather_pages(pt_ref, n_pages, [(ck_hbm, kbuf, ksem, place), (cv_hbm, vbuf, vsem, place)], S_ATTN_SLOTS)
    q = q_ref[...]
    hrow = lax.broadcasted_iota(I32, (N_HEADS, 2 * past), 0)
    ccol = lax.broadcasted_iota(I32, (N_HEADS, 2 * past), 1)
    own_kv = (ccol & 1) == (hrow // GQA_GROUP)
    hcol = lax.broadcasted_iota(I32, (N_HEADS, 1), 0)
    slope = jnp.zeros((N_HEADS, 1), F32)
    for h in range(N_HEADS):
        slope = jnp.where(hcol == h, ALIBI_SLOPES[h], slope)

    lg = _dot_nt(q, kbuf[slot].astype(BF16)) * ATT_SCALE
    lg = lg - slope * (past - (ccol >> 1)).astype(F32)
    lg = jnp.where(jnp.logical_and(own_kv, sel_ref[...] > 0.5), lg, -jnp.inf)
    first_group = lax.broadcasted_iota(I32, (N_HEADS, HEAD_DIM), 0) < GQA_GROUP
    knew = knew_ref[...].astype(BF16).astype(F32)
    vnew = vnew_ref[...].astype(BF16).astype(F32)
    knew8 = jnp.where(first_group, knew[0:1, :], knew[1:2, :])
    vnew8 = jnp.where(first_group, vnew[0:1, :], vnew[1:2, :])
    lg_self = jnp.sum(q.astype(F32) * knew8, axis=1, keepdims=True) * ATT_SCALE
    lg_self = jnp.where(selself_ref[:, 0:1] > 0.5, lg_self, -jnp.inf)

    m = jnp.maximum(jnp.maximum(lg_self, NEG_BIG), jnp.max(lg, axis=1, keepdims=True))
    p_self = jnp.exp(lg_self - m)
    pp = jnp.exp(lg - m)
    l = p_self + jnp.sum(pp, axis=1, keepdims=True)
    acc = p_self.astype(BF16).astype(F32) * vnew8 + _dot(pp.astype(BF16), vbuf[slot].astype(BF16))
    o_ref[...] = (acc / l).astype(o_ref.dtype)


def _s_attn(page_table_flat, q3, sel2, sel_self3, k_new3, v_new3, cache_k2, cache_v2, *, n_pages, past):
    bd = q3.shape[0]
    assert bd >= S_ATTN_SLOTS
    per_b = lambda s1, s2: pl.BlockSpec((None, s1, s2), lambda b, pt: (b, 0, 0))
    grid_spec = pltpu.PrefetchScalarGridSpec(
        num_scalar_prefetch=1,
        grid=(bd,),
        in_specs=[per_b(N_HEADS, HEAD_DIM), per_b(1, 2 * past), per_b(1, 128),
                  per_b(N_KV_HEADS, HEAD_DIM), per_b(N_KV_HEADS, HEAD_DIM),
                  pl.BlockSpec(memory_space=pl.ANY), pl.BlockSpec(memory_space=pl.ANY)],
        out_specs=per_b(N_HEADS, HEAD_DIM),
        scratch_shapes=[pltpu.VMEM((S_ATTN_SLOTS, 2 * past, HEAD_DIM), F32),
                        pltpu.VMEM((S_ATTN_SLOTS, 2 * past, HEAD_DIM), F32),
                        pltpu.SemaphoreType.DMA((S_ATTN_SLOTS,)), pltpu.SemaphoreType.DMA((S_ATTN_SLOTS,))],
    )
    return pl.pallas_call(
        functools.partial(_s_attn_kernel, n_pages=n_pages, past=past),
        out_shape=jax.ShapeDtypeStruct((bd, N_HEADS, HEAD_DIM), BF16),
        grid_spec=grid_spec,
        compiler_params=_params("arbitrary"),
        name="s_attn",
    )(page_table_flat, q3, sel2, sel_self3, k_new3, v_new3, cache_k2, cache_v2)


def _outln_kernel(mix_ref, x_ref, gate_ref, w_ref, g_ref, b_ref, o_ref, *, alpha):
    mix = _dot(mix_ref[...], w_ref[...])
    y = alpha * x_ref[...] + (1.0 + gate_ref[...]) * mix
    o_ref[...] = _layer_norm(y, g_ref[...], b_ref[...])


def _outln(mixin, x, gate, w, g, b, *, tm, alpha):
    n, d = x.shape
    nb, r, _ = gate.shape
    tiles_per_mod = (n // nb) // tm
    kin = mixin.shape[1]
    return pl.pallas_call(
        functools.partial(_outln_kernel, alpha=alpha),
        out_shape=jax.ShapeDtypeStruct((n, d), F32),
        grid=(n // tm,),
        in_specs=[pl.BlockSpec((tm, kin), lambda i: (i, 0)),
                  pl.BlockSpec((tm, d), lambda i: (i, 0)),
                  pl.BlockSpec((None, r, d), lambda i: (i // tiles_per_mod, 0, 0)),
                  pl.BlockSpec((kin, d), lambda i: (0, 0), pipeline_mode=pl.Buffered(1)),
                  pl.BlockSpec((1, d), lambda i: (0, 0)),
                  pl.BlockSpec((1, d), lambda i: (0, 0))],
        out_specs=pl.BlockSpec((tm, d), lambda i: (i, 0)),
        compiler_params=_params("arbitrary"),
        name="outln",
    )(mixin, x, gate, w, g, b)


def _ffn_kernel(x_ref, shift_ref, scale_ref, gate_ref, w1_ref, b1_ref, w2_ref, b2_ref, g_ref, b_ref,
                o_ref, h_scr, *, alpha):
    f = pl.program_id(1)

    @pl.when(f == 0)
    def _():
        h_scr[...] = (x_ref[...] * (1.0 + scale_ref[...]) + shift_ref[...]).astype(BF16)
        o_ref[...] = jnp.zeros_like(o_ref)

    a = jnp.maximum(_dot(h_scr[...], w1_ref[...]) + b1_ref[...], 0.0)
    a2 = (a * a).astype(BF16)
    for c in range(o_ref.shape[1] // FFN_TN):
        cs = slice(c * FFN_TN, (c + 1) * FFN_TN)
        o_ref[:, cs] += _dot(a2, w2_ref[:, cs])

    @pl.when(f == pl.num_programs(1) - 1)
    def _():
        y = alpha * x_ref[...] + (1.0 + gate_ref[...]) * (o_ref[...] + b2_ref[...])
        o_ref[...] = _layer_norm(y, g_ref[...], b_ref[...])


def _ffn(x, shift, scale, gate, w1, b1, w2, b2, g, b, *, tm, tf, alpha):
    n, d = x.shape
    nb, r, _ = gate.shape
    dff = w1.shape[1]
    tiles_per_mod = (n // nb) // tm
    mod_spec = pl.BlockSpec((None, r, d), lambda i, f: (i // tiles_per_mod, 0, 0))
    return pl.pallas_call(
        functools.partial(_ffn_kernel, alpha=alpha),
        out_shape=jax.ShapeDtypeStruct((n, d), F32),
        grid=(n // tm, dff // tf),
        in_specs=[pl.BlockSpec((tm, d), lambda i, f: (i, 0), pipeline_mode=pl.Buffered(1)),
                  mod_spec, mod_spec, mod_spec,
                  pl.BlockSpec((d, tf), lambda i, f: (0, f)),
                  pl.BlockSpec((1, tf), lambda i, f: (0, f)),
                  pl.BlockSpec((tf, d), lambda i, f: (f, 0)),
                  pl.BlockSpec((1, d), lambda i, f: (0, 0)),
                  pl.BlockSpec((1, d), lambda i, f: (0, 0)),
                  pl.BlockSpec((1, d), lambda i, f: (0, 0))],
        out_specs=pl.BlockSpec((tm, d), lambda i, f: (i, 0)),
        scratch_shapes=[pltpu.VMEM((tm, d), BF16)],
        compiler_params=_params("arbitrary", "arbitrary"),
        name="ffn",
    )(x, shift, scale, gate, w1, b1, w2, b2, g, b)


def kernel(x_prompt, x_sample, cache_k, cache_v, cache_kidx, page_table, c_prompt, c_sample,
           w_cond, b_cond, w_in, ln_v_g, ln_v_b, w_spatial, b_spatial, w_out,
           ln1_g, ln1_b, w_ff1, b_ff1, w_ff2, b_ff2, ln2_g, ln2_b):
    batch, seq, d = x_prompt.shape
    bd, ts, _ = x_sample.shape
    depth = w_in.shape[0]
    n_pages = page_table.shape[1]
    past = n_pages * PAGE_SIZE
    n_pool = cache_k.shape[1]
    dff = w_ff1.shape[2]
    assert ts == 1 and seq % KC == 0 and d == A_WIDTH + ATT_WIDTH
    assert w_in.shape[2] == MAIN_WIDTH + IDX_DIM + IDX_HEADS
    n = batch * seq
    alpha = (2 * depth) ** 0.25
    n_sel_p = min(TOPK_MAX, seq // 4)
    n_sel_s = min(TOPK_MAX, (past + ts) // 4)
    tm_p = min(1024, seq)
    tm_o = min(512, seq)
    tf = min(1024, dff)
    pt_flat = page_table.reshape(-1).astype(I32)

    xp = x_prompt.reshape(n, d)
    xs = x_sample.reshape(bd, d)
    c_all = jnp.concatenate([c_prompt, c_sample], axis=0)

    outs = {k: [] for k in ("kp", "vp", "kip", "ks", "vs", "kis", "vc")}
    for l in range(depth):
        wm = jnp.swapaxes(w_in[l], 0, 1).astype(BF16)
        wt = jnp.pad(wm[MAIN_WIDTH:], ((0, 2 * IDX_DIM - IDX_DIM - IDX_HEADS), (0, 0)))
        kidx_t = jnp.swapaxes(cache_kidx[l], 1, 2)
        lng = ln_v_g[l].reshape(1, A_WIDTH)
        lnb = ln_v_b[l].reshape(1, A_WIDTH)
        tril = jnp.tril(jnp.ones((CHUNK, CHUNK), dtype=bool))
        ws = jnp.where(tril[None], w_spatial[l], 0.0).astype(BF16)
        bsp = jnp.repeat(jnp.transpose(b_spatial[l]), A_CH, axis=1)
        ws0 = jnp.repeat(w_spatial[l][:, 0, 0], A_CH).reshape(1, A_WIDTH)
        bs0 = jnp.repeat(b_spatial[l][:, 0], A_CH).reshape(1, A_WIDTH)
        w_out_b = w_out[l].astype(BF16)
        w1_b = w_ff1[l].astype(BF16)
        w2_b = w_ff2[l].astype(BF16)
        b1 = b_ff1[l].reshape(1, dff)
        b2 = b_ff2[l].reshape(1, d)
        g1, be1 = ln1_g[l].reshape(1, d), ln1_b[l].reshape(1, d)
        g2, be2 = ln2_g[l].reshape(1, d), ln2_b[l].reshape(1, d)

        z = _cond(c_all, w_cond[l], b_cond[l])
        mods = [z[:, i * d:(i + 1) * d] for i in range(N_MOD)]
        mp = [m[:batch].reshape(batch, 1, d) for m in mods]
        ms = [m[batch:].reshape(1, bd, d) for m in mods]

        (u, vn, q, k, v, kb, vt, qi, ki, ki2, wit) = _proj(
            xp, mp[0], mp[1], wm, wt, lng, lnb, tm=tm_p, vn_dtype=BF16)
        mixin = _mix(u, vn, q, qi, wit, kb, vt, ki2, ws, bsp, batch=batch, seq=seq, n_sel=n_sel_p)
        x1 = _outln(mixin, xp, mp[2], w_out_b, g1, be1, tm=tm_o, alpha=alpha)
        xp = _ffn(x1, mp[3], mp[4], mp[5], w1_b, b1, w2_b, b2, g2, be2, tm=tm_p, tf=tf, alpha=alpha)
        outs["kp"].append(k.reshape(batch, seq, N_KV_HEADS, HEAD_DIM))
        outs["vp"].append(v.reshape(batch, seq, N_KV_HEADS, HEAD_DIM))
        outs["kip"].append(ki.reshape(batch, seq, IDX_DIM))

        (u, vn, q, k, v, _, _, qi, ki, _, wit) = _proj(
            xs, ms[0], ms[1], wm, wt, lng, lnb, tm=bd, vn_dtype=F32)
        sc, sc_self = _s_scores(pt_flat, qi.reshape(bd, IDX_HEADS, IDX_DIM),
                                jnp.transpose(wit).reshape(bd, IDX_HEADS, 1),
                                ki.reshape(bd, 1, IDX_DIM), kidx_t, n_pages=n_pages)
        sel, sel_self, a_out = _s_select(sc.reshape(bd, past), sc_self.reshape(bd, 128), u, vn, ws0, bs0,
                                         n_sel=n_sel_s)
        sel2 = jnp.repeat(sel, 2, axis=1).reshape(bd, 1, 2 * past)
        b_out = _s_attn(pt_flat, q.reshape(bd, N_HEADS, HEAD_DIM), sel2, sel_self.reshape(bd, 1, 128),
                        k.reshape(bd, N_KV_HEADS, HEAD_DIM), v.reshape(bd, N_KV_HEADS, HEAD_DIM),
                        cache_k[l].reshape(n_pool, 2 * PAGE_SIZE, HEAD_DIM),
                        cache_v[l].reshape(n_pool, 2 * PAGE_SIZE, HEAD_DIM),
                        n_pages=n_pages, past=past)
        mixin = jnp.concatenate([a_out, b_out.reshape(bd, ATT_WIDTH)], axis=1)
        x1 = _outln(mixin, xs, ms[2], w_out_b, g1, be1, tm=bd, alpha=alpha)
        xs = _ffn(x1, ms[3], ms[4], ms[5], w1_b, b1, w2_b, b2, g2, be2, tm=bd, tf=tf, alpha=alpha)
        outs["ks"].append(k.reshape(bd, ts, N_KV_HEADS, HEAD_DIM))
        outs["vs"].append(v.reshape(bd, ts, N_KV_HEADS, HEAD_DIM))
        outs["kis"].append(ki.reshape(bd, ts, IDX_DIM))
        outs["vc"].append(vn.reshape(bd, ts, A_GROUPS, A_CH))

    st = lambda name: jnp.stack(outs[name])
    return (xp.reshape(batch, seq, d), xs.reshape(bd, ts, d),
            st("kp"), st("vp"), st("kip"), st("ks"), st("vs"), st("kis"), st("vc"))
```

```python
import functools

import jax
import jax.numpy as jnp
import numpy as np
from jax import lax
from jax.experimental import pallas as pl
from jax.experimental.pallas import tpu as pltpu

F32 = jnp.float32
BF16 = jnp.bfloat16
I32 = jnp.int32

CHUNK = 128
A_GROUPS = 8
A_CH = 128
A_WIDTH = A_GROUPS * A_CH
HEAD_DIM = 128
N_HEADS = 8
N_KV_HEADS = 2
GQA_GROUP = N_HEADS // N_KV_HEADS
ATT_WIDTH = N_HEADS * HEAD_DIM
KV_WIDTH = N_KV_HEADS * HEAD_DIM
IDX_HEADS = 16
IDX_DIM = 64
IDX_WIDTH = IDX_HEADS * IDX_DIM
TOPK_MAX = 256
Q_BLOCK = 128
PAGE_SIZE = 128
N_MOD = 6
LN_EPS = 1e-5
ATT_SCALE = HEAD_DIM ** -0.5
MAIN_WIDTH = 2 * A_WIDTH + ATT_WIDTH + 2 * KV_WIDTH + IDX_WIDTH
PROJ_TN = 512
FFN_TN = 512
ALIBI_SLOPES = tuple(float(2.0 ** (-8.0 * h / N_HEADS)) for h in range(1, N_HEADS + 1))

VMEM_LIMIT_BYTES = 58 * 1024 * 1024
INT_MIN = -(2 ** 31)
NEG_BIG = -1e30
FLT_MAX = float(np.finfo(np.float32).max)
LOG2E = float(np.log2(np.e))
KC = 512
AC = 512
S_SCORES_SLOTS = 4
S_ATTN_SLOTS = 3
LANE_HEADS = 4
DEN_ROWS = 16

NT_DIMS = (((1,), (1,)), ((), ()))


def _dot(a, b):
    return jnp.dot(a, b, preferred_element_type=F32)


def _dot_nt(a, b):
    return lax.dot_general(a, b, NT_DIMS, preferred_element_type=F32)


def _layer_norm(x, g, b):
    mu = jnp.mean(x, axis=-1, keepdims=True)
    xc = x - mu
    var = jnp.mean(xc * xc, axis=-1, keepdims=True)
    return xc * lax.rsqrt(var + LN_EPS) * g + b


def _key_to_float(key):
    bits = jnp.where(key < 0, key ^ jnp.int32(0x7FFFFFFF), key)
    return pltpu.bitcast(bits, F32)


def _params(*sem):
    return pltpu.CompilerParams(dimension_semantics=sem, vmem_limit_bytes=VMEM_LIMIT_BYTES)


def _cond_kernel(c_ref, w_ref, b_ref, o_ref):
    c = c_ref[...]
    a = (c * jax.nn.sigmoid(c)).astype(BF16)
    o_ref[...] = _dot(a, w_ref[...].astype(BF16)) + b_ref[...]


def _cond(c, w, b):
    m, d = c.shape
    n = w.shape[1]
    tn = 1024
    return pl.pallas_call(
        _cond_kernel,
        out_shape=jax.ShapeDtypeStruct((m, n), F32),
        grid=(n // tn,),
        in_specs=[pl.BlockSpec((m, d), lambda j: (0, 0)),
                  pl.BlockSpec((d, tn), lambda j: (0, j)),
                  pl.BlockSpec((1, tn), lambda j: (0, j))],
        out_specs=pl.BlockSpec((m, tn), lambda j: (0, j)),
        compiler_params=_params("arbitrary"),
        name="cond",
    )(c, w, b.reshape(1, n))


_J_V, _J_Q, _J_KV, _J_QI, _J_END = 2, 4, 6, 7, 9


def _proj_kernel(x_ref, shift_ref, scale_ref, wm_ref, wt_ref, lng_ref, lnb_ref,
                 u_ref, vn_ref, q_ref, k_ref, v_ref, kb_ref, vt_ref, qi_ref, ki_ref, ki2_ref, wit_ref,
                 h_scr):
    j = pl.program_id(1)

    @pl.when(j == 0)
    def _():
        h = (x_ref[...] * (1.0 + scale_ref[...]) + shift_ref[...]).astype(BF16)
        h_scr[...] = h
        tail = _dot_nt(h, wt_ref[...])
        ki_ref[...] = tail[:, :IDX_DIM]
        lane = lax.broadcasted_iota(I32, tail.shape, 1)
        ki2_ref[...] = jnp.where(lane < IDX_DIM, tail, pltpu.roll(tail, IDX_DIM, axis=1)).astype(BF16)
        wit_ref[...] = tail.T[IDX_DIM:IDX_DIM + IDX_HEADS, :] * (IDX_HEADS ** -0.5)

    z = _dot_nt(h_scr[...], wm_ref[...])

    @pl.when(j < _J_V)
    def _():
        u_ref[...] = z

    @pl.when(jnp.logical_and(j >= _J_V, j < _J_Q))
    def _():
        for g in range(PROJ_TN // A_CH):
            sl = slice(g * A_CH, (g + 1) * A_CH)
            vn_ref[:, sl] = _layer_norm(z[:, sl], lng_ref[:, sl], lnb_ref[:, sl]).astype(vn_ref.dtype)

    @pl.when(jnp.logical_and(j >= _J_Q, j < _J_KV))
    def _():
        q_ref[...] = z.astype(BF16)

    @pl.when(j == _J_KV)
    def _():
        for hh in range(N_KV_HEADS):
            k_ref[:, hh, :] = z[:, hh * HEAD_DIM:(hh + 1) * HEAD_DIM]
            v_ref[:, hh, :] = z[:, KV_WIDTH + hh * HEAD_DIM:KV_WIDTH + (hh + 1) * HEAD_DIM]
        kb_ref[...] = z[:, :KV_WIDTH].astype(BF16)
        vt = z[:, KV_WIDTH:].T.astype(BF16)
        for c in range(vt_ref.shape[0]):
            vt_ref[c] = vt[:, c * 128:(c + 1) * 128]

    @pl.when(j >= _J_QI)
    def _():
        qi_ref[...] = z.astype(BF16)


def _proj(x, shift, scale, wm, wt, lng, lnb, *, tm, vn_dtype):
    n, d = x.shape
    nb, r, _ = shift.shape
    rows_per_mod = n // nb
    assert n % tm == 0 and rows_per_mod % tm == 0 and r in (1, tm)
    tiles_per_mod = rows_per_mod // tm
    tn = PROJ_TN

    def clipj(lo, cnt):
        return lambda i, j: (i, jnp.clip(j - lo, 0, cnt - 1))

    mod_spec = pl.BlockSpec((None, r, d), lambda i, j: (i // tiles_per_mod, 0, 0))
    out_shape = (
        jax.ShapeDtypeStruct((n, A_WIDTH), F32),
        jax.ShapeDtypeStruct((n, A_WIDTH), vn_dtype),
        jax.ShapeDtypeStruct((n, ATT_WIDTH), BF16),
        jax.ShapeDtypeStruct((n, N_KV_HEADS, HEAD_DIM), F32),
        jax.ShapeDtypeStruct((n, N_KV_HEADS, HEAD_DIM), F32),
        jax.ShapeDtypeStruct((n, KV_WIDTH), BF16),
        jax.ShapeDtypeStruct((n // 128, KV_WIDTH, 128), BF16),
        jax.ShapeDtypeStruct((n, IDX_WIDTH), BF16),
        jax.ShapeDtypeStruct((n, IDX_DIM), F32),
        jax.ShapeDtypeStruct((n, 2 * IDX_DIM), BF16),
        jax.ShapeDtypeStruct((IDX_HEADS, n), F32),
    )
    out_specs = (
        pl.BlockSpec((tm, tn), clipj(0, 2)),
        pl.BlockSpec((tm, tn), clipj(_J_V, 2)),
        pl.BlockSpec((tm, tn), clipj(_J_Q, 2)),
        pl.BlockSpec((tm, N_KV_HEADS, HEAD_DIM), lambda i, j: (i, 0, 0)),
        pl.BlockSpec((tm, N_KV_HEADS, HEAD_DIM), lambda i, j: (i, 0, 0)),
        pl.BlockSpec((tm, KV_WIDTH), lambda i, j: (i, 0)),
        pl.BlockSpec((tm // 128, KV_WIDTH, 128), lambda i, j: (i, 0, 0)),
        pl.BlockSpec((tm, tn), clipj(_J_QI, 2)),
        pl.BlockSpec((tm, IDX_DIM), lambda i, j: (i, 0)),
        pl.BlockSpec((tm, 2 * IDX_DIM), lambda i, j: (i, 0)),
        pl.BlockSpec((IDX_HEADS, tm), lambda i, j: (0, i)),
    )
    in_specs = [
        pl.BlockSpec((tm, d), lambda i, j: (i, 0)),
        mod_spec, mod_spec,
        pl.BlockSpec((tn, d), lambda i, j: (j, 0)),
        pl.BlockSpec((2 * IDX_DIM, d), lambda i, j: (0, 0)),
        pl.BlockSpec((1, tn), lambda i, j: (0, jnp.clip(j - _J_V, 0, 1))),
        pl.BlockSpec((1, tn), lambda i, j: (0, jnp.clip(j - _J_V, 0, 1))),
    ]
    return pl.pallas_call(
        _proj_kernel,
        out_shape=out_shape,
        grid=(n // tm, _J_END),
        in_specs=in_specs,
        out_specs=out_specs,
        scratch_shapes=[pltpu.VMEM((tm, d), BF16)],
        compiler_params=_params("arbitrary", "arbitrary"),
        name="proj",
    )(x, shift, scale, wm, wt, lng, lnb)


def _mix_kernel(u_ref, vn_ref, q_ref, qi_ref, wit_ref, k_ref, vt_ref, ki2_ref, ws_ref, bsp_ref,
                o_ref, sc_scr, msk_scr, qim_scr, ab_scr, acc_scr, *, n_sel):
    jq = pl.program_id(1)
    n_kc = jq // (KC // 128) + 1
    n_ac = jq // (AC // 128) + 1
    row = lax.broadcasted_iota(I32, (128, 128), 0)
    col = lax.broadcasted_iota(I32, (128, 128), 1)

    @pl.when(jnp.logical_and(pl.program_id(0) == 0, jq == 0))
    def _():
        d0 = (lax.broadcasted_iota(I32, (AC, 128), 1) - lax.broadcasted_iota(I32, (AC, 128), 0)).astype(F32)
        for h in range(N_HEADS):
            lanes = slice((h % LANE_HEADS) * 128, (h % LANE_HEADS + 1) * 128)
            ab_scr[h // LANE_HEADS, :, lanes] = (ALIBI_SLOPES[h] * LOG2E) * d0

    for p in range(IDX_HEADS // 2):
        pair = qi_ref[:, p * 128:(p + 1) * 128]
        zero = jnp.zeros_like(pair)
        qim_scr[p, 0:128, :] = jnp.where(col < IDX_DIM, pair, zero)
        qim_scr[p, 128:256, :] = jnp.where(col >= IDX_DIM, pair, zero)
    wis = wit_ref[...] * (IDX_DIM ** -0.5)
    key_minus_query = (lax.broadcasted_iota(I32, (KC, 128), 0) - lax.broadcasted_iota(I32, (KC, 128), 1))

    def idx_body(kc, carry):
        base = pl.multiple_of(kc * KC, KC)
        kk = ki2_ref[pl.ds(base, KC), :]
        acc = jnp.zeros((KC, 128), F32)
        for p in range(IDX_HEADS // 2):
            s = _dot_nt(kk, qim_scr[p])
            acc = (acc + jnp.maximum(s[:, :128], 0.0) * wis[2 * p:2 * p + 1, :]
                   + jnp.maximum(s[:, 128:], 0.0) * wis[2 * p + 1:2 * p + 2, :])
        inadmissible = key_minus_query > (jq * 128 - kc * KC)
        sc_scr[pl.ds(base, KC), :] = jnp.where(inadmissible, -jnp.inf, acc)
        return carry

    lax.fori_loop(0, n_kc, idx_body, 0)

    def count(pred):
        def body(kc, c):
            x = sc_scr[pl.ds(pl.multiple_of(kc * KC, KC), KC), :]
            m = jnp.where(pred(x), 1, 0).astype(I32)
            return c + jnp.sum(m.reshape(KC // 8, 8, 128), axis=0)
        c = lax.fori_loop(0, n_kc, body, jnp.zeros((8, 128), I32))
        return jnp.sum(c, axis=0, keepdims=True)

    def bit_body(i, t):
        cand = t ^ lax.shift_left(jnp.int32(1), 31 - i)
        cand_f = _key_to_float(cand)
        return jnp.where(count(lambda x: x >= cand_f) >= n_sel, cand, t)

    thr_key = lax.fori_loop(0, 32, bit_body, jnp.full((1, 128), INT_MIN, I32))
    thr = jnp.where(thr_key == jnp.int32(INT_MIN), -FLT_MAX, _key_to_float(thr_key))
    c_ge = count(lambda x: x >= thr)
    c_gt = count(lambda x: x > thr)
    has_tie_overflow = jnp.max(c_ge) > n_sel

    @pl.when(jnp.logical_not(has_tie_overflow))
    def _():
        def body(kc, carry):
            sl = pl.ds(pl.multiple_of(kc * KC, KC), KC)
            msk_scr[sl, :] = jnp.where(sc_scr[sl, :] >= thr, 0.0, -jnp.inf).astype(F32)
            return carry
        lax.fori_loop(0, n_kc, body, 0)

    @pl.when(has_tie_overflow)
    def _():
        need = (n_sel - c_gt).astype(F32)
        lstrict = jnp.where(col < row, 1.0, 0.0).astype(BF16)

        def body(kt, before):
            sl = pl.ds(pl.multiple_of(kt * 128, 128), 128)
            x = sc_scr[sl, :]
            eq = x == thr
            eqf = jnp.where(eq, 1.0, 0.0).astype(F32)
            rank = before + _dot(lstrict, eqf.astype(BF16))
            sel = jnp.logical_or(x > thr, jnp.logical_and(eq, rank < need))
            msk_scr[sl, :] = jnp.where(sel, 0.0, -jnp.inf).astype(F32)
            return before + jnp.sum(eqf, axis=0, keepdims=True)
        lax.fori_loop(0, n_kc * (KC // 128), body, jnp.zeros((1, 128), F32))

    gw = LANE_HEADS * 128
    n_lg = N_HEADS // LANE_HEADS
    lane_head = lax.broadcasted_iota(I32, (1, gw), 1) // 128
    q_grp, slope_vec = [], []
    for g in range(n_lg):
        heads = range(g * LANE_HEADS, (g + 1) * LANE_HEADS)
        q_grp.append(jnp.concatenate([q_ref[:, h * HEAD_DIM:(h + 1) * HEAD_DIM] for h in heads], axis=0))
        sv = jnp.zeros((1, gw), F32)
        for i, h in enumerate(heads):
            sv = jnp.where(lane_head == i, ALIBI_SLOPES[h] * LOG2E, sv)
        slope_vec.append(sv)
        acc_scr[g] = jnp.zeros((HEAD_DIM + DEN_ROWS, gw), F32)

    def att_body(ac, carry):
        base = pl.multiple_of(ac * AC, AC)
        off = (jq * 128 - ac * AC).astype(F32)
        mk = msk_scr[pl.ds(base, AC), :]
        mkw = jnp.concatenate([mk] * LANE_HEADS, axis=1)
        kv_of = [g * LANE_HEADS // GQA_GROUP for g in range(n_lg)]
        raw = [_dot_nt(k_ref[pl.ds(base, AC), kv_of[g] * HEAD_DIM:(kv_of[g] + 1) * HEAD_DIM], q_grp[g])
               for g in range(n_lg)]
        new = []
        for g in range(n_lg):
            kv = kv_of[g]
            m = carry[g]
            x = raw[g] * (ATT_SCALE * LOG2E) - ab_scr[g] + mkw
            cvec = slope_vec[g] * off
            m_new = jnp.maximum(m, jnp.max(x, axis=0, keepdims=True) - cvec)
            alpha = jnp.exp2(m - m_new)
            p = jnp.exp2(x - (m_new + cvec)).astype(BF16)
            vt_c = jnp.concatenate(
                [jnp.concatenate([vt_ref[ac * (AC // 128) + t, kv * HEAD_DIM:(kv + 1) * HEAD_DIM, :]
                                  for t in range(AC // 128)], axis=1),
                 jnp.ones((DEN_ROWS, AC), BF16)], axis=0)
            acc_scr[g] = alpha * acc_scr[g] + _dot(vt_c, p)
            new.append(m_new)
        return tuple(new)

    lax.fori_loop(0, n_ac, att_body, (jnp.full((1, gw), NEG_BIG, F32),) * n_lg)
    for g in range(n_lg):
        out_t = acc_scr[g, :HEAD_DIM, :] / acc_scr[g, HEAD_DIM:HEAD_DIM + 1, :]
        for i in range(LANE_HEADS):
            lo = A_WIDTH + (g * LANE_HEADS + i) * HEAD_DIM
            o_ref[:, lo:lo + HEAD_DIM] = out_t[:, i * 128:(i + 1) * 128].T.astype(o_ref.dtype)

    for g in range(A_GROUPS):
        sl = slice(g * A_CH, (g + 1) * A_CH)
        sp = _dot(ws_ref[g], vn_ref[:, sl]) + bsp_ref[:, sl]
        o_ref[:, sl] = (u_ref[:, sl] * sp).astype(o_ref.dtype)


def _mix(u, vn, q, qi, wit, kb, vt, ki2, ws, bsp, *, batch, seq, n_sel):
    n = batch * seq
    nblk = seq // Q_BLOCK
    ntile = seq // 128
    row_spec = lambda w: pl.BlockSpec((Q_BLOCK, w), lambda b, j: (b * nblk + j, 0))
    in_specs = [
        row_spec(A_WIDTH), row_spec(A_WIDTH), row_spec(ATT_WIDTH), row_spec(IDX_WIDTH),
        pl.BlockSpec((IDX_HEADS, Q_BLOCK), lambda b, j: (0, b * nblk + j)),
        pl.BlockSpec((seq, KV_WIDTH), lambda b, j: (b, 0)),
        pl.BlockSpec((ntile, KV_WIDTH, 128), lambda b, j: (b, 0, 0)),
        pl.BlockSpec((seq, 2 * IDX_DIM), lambda b, j: (b, 0)),
        pl.BlockSpec((A_GROUPS, CHUNK, CHUNK), lambda b, j: (0, 0, 0)),
        pl.BlockSpec((CHUNK, A_WIDTH), lambda b, j: (0, 0)),
    ]
    return pl.pallas_call(
        functools.partial(_mix_kernel, n_sel=n_sel),
        out_shape=jax.ShapeDtypeStruct((n, A_WIDTH + ATT_WIDTH), BF16),
        grid=(batch, nblk),
        in_specs=in_specs,
        out_specs=pl.BlockSpec((Q_BLOCK, A_WIDTH + ATT_WIDTH), lambda b, j: (b * nblk + j, 0)),
        scratch_shapes=[pltpu.VMEM((seq, 128), F32),
                        pltpu.VMEM((seq, 128), F32),
                        pltpu.VMEM((IDX_HEADS // 2, 256, 128), BF16),
                        pltpu.VMEM((N_HEADS // LANE_HEADS, AC, LANE_HEADS * 128), F32),
                        pltpu.VMEM((N_HEADS // LANE_HEADS, HEAD_DIM + DEN_ROWS, LANE_HEADS * 128), F32)],
        compiler_params=_params("arbitrary", "arbitrary"),
        name="mix",
    )(u, vn, q, qi, wit, kb, vt, ki2, ws, bsp)


def _page_copies(pt_ref, row, n_pages, slot, streams, *, for_wait):
    cps = []
    for p in range(n_pages):
        page = 0 if for_wait else pt_ref[row * n_pages + p]
        for hbm, buf, sem, place in streams:
            cps.append(pltpu.make_async_copy(hbm.at[page], buf.at[(slot,) + place(p)], sem.at[slot]))
    return cps


def _gather_pages(pt_ref, n_pages, streams, n_slots):
    b = pl.program_id(0)
    ahead = n_slots - 1

    @pl.when(b == 0)
    def _():
        for r in range(ahead):
            for c in _page_copies(pt_ref, r, n_pages, r, streams, for_wait=False):
                c.start()

    @pl.when(b + ahead < pl.num_programs(0))
    def _():
        for c in _page_copies(pt_ref, b + ahead, n_pages, (b + ahead) % n_slots, streams, for_wait=False):
            c.start()

    slot = b % n_slots
    for c in _page_copies(pt_ref, b, n_pages, slot, streams, for_wait=True):
        c.wait()
    return slot


def _s_scores_kernel(pt_ref, qi_ref, wi_ref, kin_ref, kidx_hbm, sc_ref, self_ref, kbuf, sem, *, n_pages):
    place = lambda p: (slice(None), pl.ds(p * PAGE_SIZE, PAGE_SIZE))
    slot = _gather_pages(pt_ref, n_pages, [(kidx_hbm, kbuf, sem, place)], S_SCORES_SLOTS)
    qi = qi_ref[...]
    w = wi_ref[...] * (IDX_DIM ** -0.5)
    kcat = kbuf[slot].astype(BF16)
    s = _dot(qi, kcat)
    sc_ref[...] = jnp.sum(jnp.maximum(s, 0.0) * w, axis=0, keepdims=True)
    kin = kin_ref[...].astype(BF16).astype(F32)
    s_self = jnp.sum(qi.astype(F32) * kin, axis=1, keepdims=True)
    v_self = jnp.sum(jnp.maximum(s_self, 0.0) * w, axis=0, keepdims=True)
    self_ref[...] = jnp.broadcast_to(v_self, self_ref.shape)


def _s_scores(page_table_flat, qi3, wi_col, ki_new3, cache_kidx_l, *, n_pages):
    bd = qi3.shape[0]
    assert bd >= S_SCORES_SLOTS
    past = n_pages * PAGE_SIZE
    grid_spec = pltpu.PrefetchScalarGridSpec(
        num_scalar_prefetch=1,
        grid=(bd,),
        in_specs=[pl.BlockSpec((None, IDX_HEADS, IDX_DIM), lambda b, pt: (b, 0, 0)),
                  pl.BlockSpec((None, IDX_HEADS, 1), lambda b, pt: (b, 0, 0)),
                  pl.BlockSpec((None, 1, IDX_DIM), lambda b, pt: (b, 0, 0)),
                  pl.BlockSpec(memory_space=pl.ANY)],
        out_specs=(pl.BlockSpec((None, 1, past), lambda b, pt: (b, 0, 0)),
                   pl.BlockSpec((None, 1, 128), lambda b, pt: (b, 0, 0))),
        scratch_shapes=[pltpu.VMEM((S_SCORES_SLOTS, IDX_DIM, past), F32),
                        pltpu.SemaphoreType.DMA((S_SCORES_SLOTS,))],
    )
    return pl.pallas_call(
        functools.partial(_s_scores_kernel, n_pages=n_pages),
        out_shape=(jax.ShapeDtypeStruct((bd, 1, past), F32),
                   jax.ShapeDtypeStruct((bd, 1, 128), F32)),
        grid_spec=grid_spec,
        compiler_params=_params("arbitrary"),
        name="s_scores",
    )(page_table_flat, qi3, wi_col, ki_new3, cache_kidx_l)


def _s_select_kernel(sc_ref, self_ref, u_ref, vn_ref, ws0_ref, bs0_ref,
                     sel_ref, selself_ref, a_ref, *, n_sel):
    keys = sc_ref[...]
    kself = self_ref[:, 0:1]
    bd, past = keys.shape

    def count_ge(cand):
        c = jnp.sum(jnp.where(keys >= cand, 1, 0).astype(I32), axis=1, keepdims=True)
        return c + jnp.where(kself >= cand, 1, 0).astype(I32)

    def bit_body(i, t):
        cand = t ^ lax.shift_left(jnp.int32(1), 31 - i)
        return jnp.where(count_ge(_key_to_float(cand)) >= n_sel, cand, t)

    thr = _key_to_float(lax.fori_loop(0, 32, bit_body, jnp.full((bd, 1), INT_MIN, I32)))
    c_gt = (jnp.sum(jnp.where(keys > thr, 1, 0).astype(I32), axis=1, keepdims=True)
            + jnp.where(kself > thr, 1, 0).astype(I32))
    need = (n_sel - c_gt).astype(F32)
    r = lax.broadcasted_iota(I32, (128, 128), 0)
    c = lax.broadcasted_iota(I32, (128, 128), 1)
    ustrict = jnp.where(r < c, 1.0, 0.0).astype(BF16)
    before = jnp.zeros((bd, 1), F32)
    for t in range(past // 128):
        kt = keys[:, t * 128:(t + 1) * 128]
        eq = kt == thr
        eqf = jnp.where(eq, 1.0, 0.0).astype(F32)
        rank = before + _dot(eqf.astype(BF16), ustrict)
        sel = jnp.logical_or(kt > thr, jnp.logical_and(eq, rank < need))
        sel_ref[:, t * 128:(t + 1) * 128] = jnp.where(sel, 1.0, 0.0).astype(F32)
        before = before + jnp.sum(eqf, axis=1, keepdims=True)
    sel_self = jnp.logical_or(kself > thr, jnp.logical_and(kself == thr, before < need))
    selself_ref[...] = jnp.broadcast_to(jnp.where(sel_self, 1.0, 0.0).astype(F32), selself_ref.shape)
    sp = ws0_ref[...] * vn_ref[...] + bs0_ref[...]
    a_ref[...] = (u_ref[...] * sp).astype(a_ref.dtype)


def _s_select(sc, sc_self, u, vn, ws0, bs0, *, n_sel):
    bd, past = sc.shape
    full = lambda a: pl.BlockSpec(a.shape, lambda i: (0,) * a.ndim)
    args = (sc, sc_self, u, vn, ws0, bs0)
    return pl.pallas_call(
        functools.partial(_s_select_kernel, n_sel=n_sel),
        out_shape=(jax.ShapeDtypeStruct((bd, past), F32),
                   jax.ShapeDtypeStruct((bd, 128), F32),
                   jax.ShapeDtypeStruct((bd, A_WIDTH), BF16)),
        grid=(1,),
        in_specs=[full(a) for a in args],
        out_specs=(pl.BlockSpec((bd, past), lambda i: (0, 0)),
                   pl.BlockSpec((bd, 128), lambda i: (0, 0)),
                   pl.BlockSpec((bd, A_WIDTH), lambda i: (0, 0))),
        compiler_params=_params("arbitrary"),
        name="s_select",
    )(*args)


def _s_attn_kernel(pt_ref, q_ref, sel_ref, selself_ref, knew_ref, vnew_ref, ck_hbm, cv_hbm, o_ref,
                   kbuf, vbuf, ksem, vsem, *, n_pages, past):
    place = lambda p: (pl.ds(p * 2 * PAGE_SIZE, 2 * PAGE_SIZE), slice(None))
    slot = _gather_pages(pt_ref, n_pages, [(ck_hbm, kbuf, ksem, place), (cv_hbm, vbuf, vsem, place)], S_ATTN_SLOTS)
    q = q_ref[...]
    hrow = lax.broadcasted_iota(I32, (N_HEADS, 2 * past), 0)
    ccol = lax.broadcasted_iota(I32, (N_HEADS, 2 * past), 1)
    own_kv = (ccol & 1) == (hrow // GQA_GROUP)
    hcol = lax.broadcasted_iota(I32, (N_HEADS, 1), 0)
    slope = jnp.zeros((N_HEADS, 1), F32)
    for h in range(N_HEADS):
        slope = jnp.where(hcol == h, ALIBI_SLOPES[h], slope)

    lg = _dot_nt(q, kbuf[slot].astype(BF16)) * ATT_SCALE
    lg = lg - slope * (past - (ccol >> 1)).astype(F32)
    lg = jnp.where(jnp.logical_and(own_kv, sel_ref[...] > 0.5), lg, -jnp.inf)
    first_group = lax.broadcasted_iota(I32, (N_HEADS, HEAD_DIM), 0) < GQA_GROUP
    knew = knew_ref[...].astype(BF16).astype(F32)
    vnew = vnew_ref[...].astype(BF16).astype(F32)
    knew8 = jnp.where(first_group, knew[0:1, :], knew[1:2, :])
    vnew8 = jnp.where(first_group, vnew[0:1, :], vnew[1:2, :])
    lg_self = jnp.sum(q.astype(F32) * knew8, axis=1, keepdims=True) * ATT_SCALE
    lg_self = jnp.where(selself_ref[:, 0:1] > 0.5, lg_self, -jnp.inf)

    m = jnp.maximum(jnp.maximum(lg_self, NEG_BIG), jnp.max(lg, axis=1, keepdims=True))
    p_self = jnp.exp(lg_self - m)
    pp = jnp.exp(lg - m)
    l = p_self + jnp.sum(pp, axis=1, keepdims=True)
    acc = p_self.astype(BF16).astype(F32) * vnew8 + _dot(pp.astype(BF16), vbuf[slot].astype(BF16))
    o_ref[...] = (acc / l).astype(o_ref.dtype)


def _s_attn(page_table_flat, q3, sel2, sel_self3, k_new3, v_new3, cache_k2, cache_v2, *, n_pages, past):
    bd = q3.shape[0]
    assert bd >= S_ATTN_SLOTS
    per_b = lambda s1, s2: pl.BlockSpec((None, s1, s2), lambda b, pt: (b, 0, 0))
    grid_spec = pltpu.PrefetchScalarGridSpec(
        num_scalar_prefetch=1,
        grid=(bd,),
        in_specs=[per_b(N_HEADS, HEAD_DIM), per_b(1, 2 * past), per_b(1, 128),
                  per_b(N_KV_HEADS, HEAD_DIM), per_b(N_KV_HEADS, HEAD_DIM),
                  pl.BlockSpec(memory_space=pl.ANY), pl.BlockSpec(memory_space=pl.ANY)],
        out_specs=per_b(N_HEADS, HEAD_DIM),
        scratch_shapes=[pltpu.VMEM((S_ATTN_SLOTS, 2 * past, HEAD_DIM), F32),
                        pltpu.VMEM((S_ATTN_SLOTS, 2 * past, HEAD_DIM), F32),
                        pltpu.SemaphoreType.DMA((S_ATTN_SLOTS,)), pltpu.SemaphoreType.DMA((S_ATTN_SLOTS,))],
    )
    return pl.pallas_call(
        functools.partial(_s_attn_kernel, n_pages=n_pages, past=past),
        out_shape=jax.ShapeDtypeStruct((bd, N_HEADS, HEAD_DIM), BF16),
        grid_spec=grid_spec,
        compiler_params=_params("arbitrary"),
        name="s_attn",
    )(page_table_flat, q3, sel2, sel_self3, k_new3, v_new3, cache_k2, cache_v2)


def _outln_kernel(mix_ref, x_ref, gate_ref, w_ref, g_ref, b_ref, o_ref, *wb_ref, alpha):
    w = w_ref[...]
    if wb_ref:
        w = w.astype(BF16)
        wb_ref[0][...] = w
    mix = _dot(mix_ref[...], w)
    y = alpha * x_ref[...] + (1.0 + gate_ref[...]) * mix
    o_ref[...] = _layer_norm(y, g_ref[...], b_ref[...])


def _outln(mixin, x, gate, w, g, b, *, tm, alpha):
    n, d = x.shape
    nb, r, _ = gate.shape
    tiles_per_mod = (n // nb) // tm
    kin = mixin.shape[1]
    emit = w.dtype != BF16
    assert not emit or n == tm
    out_shape = [jax.ShapeDtypeStruct((n, d), F32)]
    out_specs = [pl.BlockSpec((tm, d), lambda i: (i, 0))]
    if emit:
        out_shape.append(jax.ShapeDtypeStruct((kin, d), BF16))
        out_specs.append(pl.BlockSpec((kin, d), lambda i: (0, 0)))
    res = pl.pallas_call(
        functools.partial(_outln_kernel, alpha=alpha),
        out_shape=out_shape,
        grid=(n // tm,),
        in_specs=[pl.BlockSpec((tm, kin), lambda i: (i, 0)),
                  pl.BlockSpec((tm, d), lambda i: (i, 0)),
                  pl.BlockSpec((None, r, d), lambda i: (i // tiles_per_mod, 0, 0)),
                  pl.BlockSpec((kin, d), lambda i: (0, 0), pipeline_mode=pl.Buffered(1)),
                  pl.BlockSpec((1, d), lambda i: (0, 0)),
                  pl.BlockSpec((1, d), lambda i: (0, 0))],
        out_specs=out_specs,
        compiler_params=_params("arbitrary"),
        name="outln",
    )(mixin, x, gate, w, g, b)
    return res if emit else res[0]


def _ffn_kernel(x_ref, shift_ref, scale_ref, gate_ref, w1_ref, b1_ref, w2_ref, b2_ref, g_ref, b_ref,
                o_ref, *rest, alpha):
    *wb_refs, h_scr = rest
    f = pl.program_id(1)

    @pl.when(f == 0)
    def _():
        h_scr[...] = (x_ref[...] * (1.0 + scale_ref[...]) + shift_ref[...]).astype(BF16)
        o_ref[...] = jnp.zeros_like(o_ref)

    if wb_refs:
        wb_refs[0][...] = w1_ref[...].astype(BF16)
        wb_refs[1][...] = w2_ref[...].astype(BF16)
        w1_ref, w2_ref = wb_refs
    a = jnp.maximum(_dot(h_scr[...], w1_ref[...]) + b1_ref[...], 0.0)
    a2 = (a * a).astype(BF16)
    for c in range(o_ref.shape[1] // FFN_TN):
        cs = slice(c * FFN_TN, (c + 1) * FFN_TN)
        o_ref[:, cs] += _dot(a2, w2_ref[:, cs])

    @pl.when(f == pl.num_programs(1) - 1)
    def _():
        y = alpha * x_ref[...] + (1.0 + gate_ref[...]) * (o_ref[...] + b2_ref[...])
        o_ref[...] = _layer_norm(y, g_ref[...], b_ref[...])


def _ffn(x, shift, scale, gate, w1, b1, w2, b2, g, b, *, tm, tf, alpha):
    n, d = x.shape
    nb, r, _ = gate.shape
    dff = w1.shape[1]
    tiles_per_mod = (n // nb) // tm
    mod_spec = pl.BlockSpec((None, r, d), lambda i, f: (i // tiles_per_mod, 0, 0))
    emit = w1.dtype != BF16
    assert not emit or n == tm
    out_shape = [jax.ShapeDtypeStruct((n, d), F32)]
    out_specs = [pl.BlockSpec((tm, d), lambda i, f: (i, 0))]
    if emit:
        out_shape += [jax.ShapeDtypeStruct((d, dff), BF16), jax.ShapeDtypeStruct((dff, d), BF16)]
        out_specs += [pl.BlockSpec((d, tf), lambda i, f: (0, f)), pl.BlockSpec((tf, d), lambda i, f: (f, 0))]
    res = pl.pallas_call(
        functools.partial(_ffn_kernel, alpha=alpha),
        out_shape=out_shape,
        grid=(n // tm, dff // tf),
        in_specs=[pl.BlockSpec((tm, d), lambda i, f: (i, 0), pipeline_mode=pl.Buffered(1)),
                  mod_spec, mod_spec, mod_spec,
                  pl.BlockSpec((d, tf), lambda i, f: (0, f)),
                  pl.BlockSpec((1, tf), lambda i, f: (0, f)),
                  pl.BlockSpec((tf, d), lambda i, f: (f, 0)),
                  pl.BlockSpec((1, d), lambda i, f: (0, 0)),
                  pl.BlockSpec((1, d), lambda i, f: (0, 0)),
                  pl.BlockSpec((1, d), lambda i, f: (0, 0))],
        out_specs=out_specs,
        scratch_shapes=[pltpu.VMEM((tm, d), BF16)],
        compiler_params=_params("arbitrary", "arbitrary"),
        name="ffn",
    )(x, shift, scale, gate, w1, b1, w2, b2, g, b)
    return res if emit else res[0]


def kernel(x_prompt, x_sample, cache_k, cache_v, cache_kidx, page_table, c_prompt, c_sample,
           w_cond, b_cond, w_in, ln_v_g, ln_v_b, w_spatial, b_spatial, w_out,
           ln1_g, ln1_b, w_ff1, b_ff1, w_ff2, b_ff2, ln2_g, ln2_b):
    batch, seq, d = x_prompt.shape
    bd, ts, _ = x_sample.shape
    depth = w_in.shape[0]
    n_pages = page_table.shape[1]
    past = n_pages * PAGE_SIZE
    n_pool = cache_k.shape[1]
    dff = w_ff1.shape[2]
    assert ts == 1 and seq % KC == 0 and d == A_WIDTH + ATT_WIDTH
    assert w_in.shape[2] == MAIN_WIDTH + IDX_DIM + IDX_HEADS
    n = batch * seq
    alpha = (2 * depth) ** 0.25
    n_sel_p = min(TOPK_MAX, seq // 4)
    n_sel_s = min(TOPK_MAX, (past + ts) // 4)
    tm_p = min(1024, seq)
    tm_o = min(512, seq)
    tf = min(1024, dff)
    pt_flat = page_table.reshape(-1).astype(I32)

    xp = x_prompt.reshape(n, d)
    xs = x_sample.reshape(bd, d)
    c_all = jnp.concatenate([c_prompt, c_sample], axis=0)

    outs = {k: [] for k in ("kp", "vp", "kip", "ks", "vs", "kis", "vc")}
    for l in range(depth):
        wm = jnp.swapaxes(w_in[l], 0, 1).astype(BF16)
        wt = jnp.pad(wm[MAIN_WIDTH:], ((0, 2 * IDX_DIM - IDX_DIM - IDX_HEADS), (0, 0)))
        kidx_t = jnp.swapaxes(cache_kidx[l], 1, 2)
        lng = ln_v_g[l].reshape(1, A_WIDTH)
        lnb = ln_v_b[l].reshape(1, A_WIDTH)
        tril = jnp.tril(jnp.ones((CHUNK, CHUNK), dtype=bool))
        ws = jnp.where(tril[None], w_spatial[l], 0.0).astype(BF16)
        bsp = jnp.repeat(jnp.transpose(b_spatial[l]), A_CH, axis=1)
        ws0 = jnp.repeat(w_spatial[l][:, 0, 0], A_CH).reshape(1, A_WIDTH)
        bs0 = jnp.repeat(b_spatial[l][:, 0], A_CH).reshape(1, A_WIDTH)
        b1 = b_ff1[l].reshape(1, dff)
        b2 = b_ff2[l].reshape(1, d)
        g1, be1 = ln1_g[l].reshape(1, d), ln1_b[l].reshape(1, d)
        g2, be2 = ln2_g[l].reshape(1, d), ln2_b[l].reshape(1, d)

        z = _cond(c_all, w_cond[l], b_cond[l])
        mods = [z[:, i * d:(i + 1) * d] for i in range(N_MOD)]
        mp = [m[:batch].reshape(batch, 1, d) for m in mods]
        ms = [m[batch:].reshape(1, bd, d) for m in mods]

        (u, vn, q, k, v, _, _, qi, ki, _, wit) = _proj(
            xs, ms[0], ms[1], wm, wt, lng, lnb, tm=bd, vn_dtype=F32)
        sc, sc_self = _s_scores(pt_flat, qi.reshape(bd, IDX_HEADS, IDX_DIM),
                                jnp.transpose(wit).reshape(bd, IDX_HEADS, 1),
                                ki.reshape(bd, 1, IDX_DIM), kidx_t, n_pages=n_pages)
        sel, sel_self, a_out = _s_select(sc.reshape(bd, past), sc_self.reshape(bd, 128), u, vn, ws0, bs0,
                                         n_sel=n_sel_s)
        sel2 = jnp.repeat(sel, 2, axis=1).reshape(bd, 1, 2 * past)
        b_out = _s_attn(pt_flat, q.reshape(bd, N_HEADS, HEAD_DIM), sel2, sel_self.reshape(bd, 1, 128),
                        k.reshape(bd, N_KV_HEADS, HEAD_DIM), v.reshape(bd, N_KV_HEADS, HEAD_DIM),
                        cache_k[l].reshape(n_pool, 2 * PAGE_SIZE, HEAD_DIM),
                        cache_v[l].reshape(n_pool, 2 * PAGE_SIZE, HEAD_DIM),
                        n_pages=n_pages, past=past)
        mixin = jnp.concatenate([a_out, b_out.reshape(bd, ATT_WIDTH)], axis=1)
        x1, w_out_b = _outln(mixin, xs, ms[2], w_out[l], g1, be1, tm=bd, alpha=alpha)
        xs, w1_b, w2_b = _ffn(x1, ms[3], ms[4], ms[5], w_ff1[l], b1, w_ff2[l], b2, g2, be2, tm=bd, tf=tf, alpha=alpha)
        outs["ks"].append(k.reshape(bd, ts, N_KV_HEADS, HEAD_DIM))
        outs["vs"].append(v.reshape(bd, ts, N_KV_HEADS, HEAD_DIM))
        outs["kis"].append(ki.reshape(bd, ts, IDX_DIM))
        outs["vc"].append(vn.reshape(bd, ts, A_GROUPS, A_CH))

        (u, vn, q, k, v, kb, vt, qi, ki, ki2, wit) = _proj(
            xp, mp[0], mp[1], wm, wt, lng, lnb, tm=tm_p, vn_dtype=BF16)
        mixin = _mix(u, vn, q, qi, wit, kb, vt, ki2, ws, bsp, batch=batch, seq=seq, n_sel=n_sel_p)
        x1 = _outln(mixin, xp, mp[2], w_out_b, g1, be1, tm=tm_o, alpha=alpha)
        xp = _ffn(x1, mp[3], mp[4], mp[5], w1_b, b1, w2_b, b2, g2, be2, tm=tm_p, tf=tf, alpha=alpha)
        outs["kp"].append(k.reshape(batch, seq, N_KV_HEADS, HEAD_DIM))
        outs["vp"].append(v.reshape(batch, seq, N_KV_HEADS, HEAD_DIM))
        outs["kip"].append(ki.reshape(batch, seq, IDX_DIM))

    st = lambda name: jnp.stack(outs[name])
    return (xp.reshape(batch, seq, d), xs.reshape(bd, ts, d),
            st("kp"), st("vp"), st("kip"), st("ks"), st("vs"), st("kis"), st("vc"))
```

```python
import functools

import jax
import jax.numpy as jnp
import numpy as np
from jax import lax
from jax.experimental import pallas as pl
from jax.experimental.pallas import tpu as pltpu

F32 = jnp.float32
BF16 = jnp.bfloat16
I32 = jnp.int32

CHUNK = 128
A_GROUPS = 8
A_CH = 128
A_WIDTH = A_GROUPS * A_CH
HEAD_DIM = 128
N_HEADS = 8
N_KV_HEADS = 2
GQA_GROUP = N_HEADS // N_KV_HEADS
ATT_WIDTH = N_HEADS * HEAD_DIM
KV_WIDTH = N_KV_HEADS * HEAD_DIM
IDX_HEADS = 16
IDX_DIM = 64
IDX_WIDTH = IDX_HEADS * IDX_DIM
TOPK_MAX = 256
Q_BLOCK = 128
PAGE_SIZE = 128
N_MOD = 6
LN_EPS = 1e-5
ATT_SCALE = HEAD_DIM ** -0.5
MAIN_WIDTH = 2 * A_WIDTH + ATT_WIDTH + 2 * KV_WIDTH + IDX_WIDTH
PROJ_TN = 512
FFN_TN = 512
ALIBI_SLOPES = tuple(float(2.0 ** (-8.0 * h / N_HEADS)) for h in range(1, N_HEADS + 1))

VMEM_LIMIT_BYTES = 58 * 1024 * 1024
INT_MIN = -(2 ** 31)
NEG_BIG = -1e30
FLT_MAX = float(np.finfo(np.float32).max)
LOG2E = float(np.log2(np.e))
KC = 512
AC = 512
S_SCORES_SLOTS = 4
S_ATTN_SLOTS = 3
LANE_HEADS = 4
DEN_ROWS = 16

NT_DIMS = (((1,), (1,)), ((), ()))


def _dot(a, b):
    return jnp.dot(a, b, preferred_element_type=F32)


def _dot_nt(a, b):
    return lax.dot_general(a, b, NT_DIMS, preferred_element_type=F32)


def _layer_norm(x, g, b):
    mu = jnp.mean(x, axis=-1, keepdims=True)
    xc = x - mu
    var = jnp.mean(xc * xc, axis=-1, keepdims=True)
    return xc * lax.rsqrt(var + LN_EPS) * g + b


def _key_to_float(key):
    bits = jnp.where(key < 0, key ^ jnp.int32(0x7FFFFFFF), key)
    return pltpu.bitcast(bits, F32)


def _params(*sem):
    return pltpu.CompilerParams(dimension_semantics=sem, vmem_limit_bytes=VMEM_LIMIT_BYTES)


def _cond_kernel(c_ref, w_ref, b_ref, o_ref):
    c = c_ref[...]
    a = (c * jax.nn.sigmoid(c)).astype(BF16)
    o_ref[...] = _dot(a, w_ref[...].astype(BF16)) + b_ref[...]


def _cond(c, w, b):
    m, d = c.shape
    n = w.shape[1]
    tn = 1024
    return pl.pallas_call(
        _cond_kernel,
        out_shape=jax.ShapeDtypeStruct((m, n), F32),
        grid=(n // tn,),
        in_specs=[pl.BlockSpec((m, d), lambda j: (0, 0)),
                  pl.BlockSpec((d, tn), lambda j: (0, j)),
                  pl.BlockSpec((1, tn), lambda j: (0, j))],
        out_specs=pl.BlockSpec((m, tn), lambda j: (0, j)),
        compiler_params=_params("arbitrary"),
        name="cond",
    )(c, w, b.reshape(1, n))


_J_V, _J_Q, _J_KV, _J_QI, _J_END = 2, 4, 6, 7, 9


def _proj_kernel(x_ref, shift_ref, scale_ref, wm_ref, wt_ref, lng_ref, lnb_ref,
                 u_ref, vn_ref, q_ref, k_ref, v_ref, kb_ref, vt_ref, qi_ref, ki_ref, ki2_ref, wit_ref,
                 h_scr):
    j = pl.program_id(1)

    @pl.when(j == 0)
    def _():
        h = (x_ref[...] * (1.0 + scale_ref[...]) + shift_ref[...]).astype(BF16)
        h_scr[...] = h
        tail = _dot_nt(h, wt_ref[...])
        ki_ref[...] = tail[:, :IDX_DIM]
        lane = lax.broadcasted_iota(I32, tail.shape, 1)
        ki2_ref[...] = jnp.where(lane < IDX_DIM, tail, pltpu.roll(tail, IDX_DIM, axis=1)).astype(BF16)
        wit_ref[...] = tail.T[IDX_DIM:IDX_DIM + IDX_HEADS, :] * (IDX_HEADS ** -0.5)

    def tile():
        return _dot_nt(h_scr[...], wm_ref[...])

    @pl.when(j < _J_V)
    def _():
        u_ref[...] = tile()

    @pl.when(jnp.logical_and(j >= _J_V, j < _J_Q))
    def _():
        z = tile()
        for g in range(PROJ_TN // A_CH):
            sl = slice(g * A_CH, (g + 1) * A_CH)
            vn_ref[:, sl] = _layer_norm(z[:, sl], lng_ref[:, sl], lnb_ref[:, sl]).astype(vn_ref.dtype)

    @pl.when(jnp.logical_and(j >= _J_Q, j < _J_KV))
    def _():
        q_ref[...] = tile().astype(BF16)

    @pl.when(j == _J_KV)
    def _():
        z = tile()
        for hh in range(N_KV_HEADS):
            k_ref[:, hh, :] = z[:, hh * HEAD_DIM:(hh + 1) * HEAD_DIM]
            v_ref[:, hh, :] = z[:, KV_WIDTH + hh * HEAD_DIM:KV_WIDTH + (hh + 1) * HEAD_DIM]
        kb_ref[...] = z[:, :KV_WIDTH].astype(BF16)
        vt = z[:, KV_WIDTH:].T.astype(BF16)
        for c in range(vt_ref.shape[0]):
            vt_ref[c] = vt[:, c * 128:(c + 1) * 128]

    @pl.when(j >= _J_QI)
    def _():
        qi_ref[...] = tile().astype(BF16)


def _proj(x, shift, scale, wm, wt, lng, lnb, *, tm, vn_dtype):
    n, d = x.shape
    nb, r, _ = shift.shape
    rows_per_mod = n // nb
    assert n % tm == 0 and rows_per_mod % tm == 0 and r in (1, tm)
    tiles_per_mod = rows_per_mod // tm
    tn = PROJ_TN

    def clipj(lo, cnt):
        return lambda i, j: (i, jnp.clip(j - lo, 0, cnt - 1))

    mod_spec = pl.BlockSpec((None, r, d), lambda i, j: (i // tiles_per_mod, 0, 0))
    out_shape = (
        jax.ShapeDtypeStruct((n, A_WIDTH), F32),
        jax.ShapeDtypeStruct((n, A_WIDTH), vn_dtype),
        jax.ShapeDtypeStruct((n, ATT_WIDTH), BF16),
        jax.ShapeDtypeStruct((n, N_KV_HEADS, HEAD_DIM), F32),
        jax.ShapeDtypeStruct((n, N_KV_HEADS, HEAD_DIM), F32),
        jax.ShapeDtypeStruct((n, KV_WIDTH), BF16),
        jax.ShapeDtypeStruct((n // 128, KV_WIDTH, 128), BF16),
        jax.ShapeDtypeStruct((n, IDX_WIDTH), BF16),
        jax.ShapeDtypeStruct((n, IDX_DIM), F32),
        jax.ShapeDtypeStruct((n, 2 * IDX_DIM), BF16),
        jax.ShapeDtypeStruct((IDX_HEADS, n), F32),
    )
    out_specs = (
        pl.BlockSpec((tm, tn), clipj(0, 2)),
        pl.BlockSpec((tm, tn), clipj(_J_V, 2)),
        pl.BlockSpec((tm, tn), clipj(_J_Q, 2)),
        pl.BlockSpec((tm, N_KV_HEADS, HEAD_DIM), lambda i, j: (i, 0, 0)),
        pl.BlockSpec((tm, N_KV_HEADS, HEAD_DIM), lambda i, j: (i, 0, 0)),
        pl.BlockSpec((tm, KV_WIDTH), lambda i, j: (i, 0)),
        pl.BlockSpec((tm // 128, KV_WIDTH, 128), lambda i, j: (i, 0, 0)),
        pl.BlockSpec((tm, tn), clipj(_J_QI, 2)),
        pl.BlockSpec((tm, IDX_DIM), lambda i, j: (i, 0)),
        pl.BlockSpec((tm, 2 * IDX_DIM), lambda i, j: (i, 0)),
        pl.BlockSpec((IDX_HEADS, tm), lambda i, j: (0, i)),
    )
    in_specs = [
        pl.BlockSpec((tm, d), lambda i, j: (i, 0)),
        mod_spec, mod_spec,
        pl.BlockSpec((tn, d), lambda i, j: (j, 0)),
        pl.BlockSpec((2 * IDX_DIM, d), lambda i, j: (0, 0)),
        pl.BlockSpec((1, tn), lambda i, j: (0, jnp.clip(j - _J_V, 0, 1))),
        pl.BlockSpec((1, tn), lambda i, j: (0, jnp.clip(j - _J_V, 0, 1))),
    ]
    return pl.pallas_call(
        _proj_kernel,
        out_shape=out_shape,
        grid=(n // tm, _J_END),
        in_specs=in_specs,
        out_specs=out_specs,
        scratch_shapes=[pltpu.VMEM((tm, d), BF16)],
        compiler_params=_params("arbitrary", "arbitrary"),
        name="proj",
    )(x, shift, scale, wm, wt, lng, lnb)


def _mix_kernel(u_ref, vn_ref, q_ref, qi_ref, wit_ref, k_ref, vt_ref, ki2_ref, ws_ref, bsp_ref,
                o_ref, sc_scr, msk_scr, qim_scr, ab_scr, acc_scr, *, n_sel):
    jq = pl.program_id(1)
    n_kc = jq // (KC // 128) + 1
    n_ac = jq // (AC // 128) + 1
    row = lax.broadcasted_iota(I32, (128, 128), 0)
    col = lax.broadcasted_iota(I32, (128, 128), 1)

    @pl.when(jnp.logical_and(pl.program_id(0) == 0, jq == 0))
    def _():
        d0 = (lax.broadcasted_iota(I32, (AC, 128), 1) - lax.broadcasted_iota(I32, (AC, 128), 0)).astype(F32)
        for h in range(N_HEADS):
            lanes = slice((h % LANE_HEADS) * 128, (h % LANE_HEADS + 1) * 128)
            ab_scr[h // LANE_HEADS, :, lanes] = (ALIBI_SLOPES[h] * LOG2E) * d0

    for p in range(IDX_HEADS // 2):
        pair = qi_ref[:, p * 128:(p + 1) * 128]
        zero = jnp.zeros_like(pair)
        qim_scr[p, 0:128, :] = jnp.where(col < IDX_DIM, pair, zero)
        qim_scr[p, 128:256, :] = jnp.where(col >= IDX_DIM, pair, zero)
    wis = wit_ref[...] * (IDX_DIM ** -0.5)
    key_minus_query = (lax.broadcasted_iota(I32, (KC, 128), 0) - lax.broadcasted_iota(I32, (KC, 128), 1))

    def idx_body(kc, carry):
        base = pl.multiple_of(kc * KC, KC)
        kk = ki2_ref[pl.ds(base, KC), :]
        acc = jnp.zeros((KC, 128), F32)
        for p in range(IDX_HEADS // 2):
            s = _dot_nt(kk, qim_scr[p])
            acc = (acc + jnp.maximum(s[:, :128], 0.0) * wis[2 * p:2 * p + 1, :]
                   + jnp.maximum(s[:, 128:], 0.0) * wis[2 * p + 1:2 * p + 2, :])
        inadmissible = key_minus_query > (jq * 128 - kc * KC)
        sc_scr[pl.ds(base, KC), :] = jnp.where(inadmissible, -jnp.inf, acc)
        return carry

    lax.fori_loop(0, n_kc, idx_body, 0)

    def count(pred):
        def body(kc, c):
            x = sc_scr[pl.ds(pl.multiple_of(kc * KC, KC), KC), :]
            m = jnp.where(pred(x), 1, 0).astype(I32)
            return c + jnp.sum(m.reshape(KC // 8, 8, 128), axis=0)
        c = lax.fori_loop(0, n_kc, body, jnp.zeros((8, 128), I32))
        return jnp.sum(c, axis=0, keepdims=True)

    def bit_body(i, t):
        cand = t ^ lax.shift_left(jnp.int32(1), 31 - i)
        cand_f = _key_to_float(cand)
        return jnp.where(count(lambda x: x >= cand_f) >= n_sel, cand, t)

    thr_key = lax.fori_loop(0, 32, bit_body, jnp.full((1, 128), INT_MIN, I32))
    thr = jnp.where(thr_key == jnp.int32(INT_MIN), -FLT_MAX, _key_to_float(thr_key))
    c_ge = count(lambda x: x >= thr)
    c_gt = count(lambda x: x > thr)
    has_tie_overflow = jnp.max(c_ge) > n_sel

    @pl.when(jnp.logical_not(has_tie_overflow))
    def _():
        def body(kc, carry):
            sl = pl.ds(pl.multiple_of(kc * KC, KC), KC)
            msk_scr[sl, :] = jnp.where(sc_scr[sl, :] >= thr, 0.0, -jnp.inf).astype(F32)
            return carry
        lax.fori_loop(0, n_kc, body, 0)

    @pl.when(has_tie_overflow)
    def _():
        need = (n_sel - c_gt).astype(F32)
        lstrict = jnp.where(col < row, 1.0, 0.0).astype(BF16)

        def body(kt, before):
            sl = pl.ds(pl.multiple_of(kt * 128, 128), 128)
            x = sc_scr[sl, :]
            eq = x == thr
            eqf = jnp.where(eq, 1.0, 0.0).astype(F32)
            rank = before + _dot(lstrict, eqf.astype(BF16))
            sel = jnp.logical_or(x > thr, jnp.logical_and(eq, rank < need))
            msk_scr[sl, :] = jnp.where(sel, 0.0, -jnp.inf).astype(F32)
            return before + jnp.sum(eqf, axis=0, keepdims=True)
        lax.fori_loop(0, n_kc * (KC // 128), body, jnp.zeros((1, 128), F32))

    gw = LANE_HEADS * 128
    n_lg = N_HEADS // LANE_HEADS
    lane_head = lax.broadcasted_iota(I32, (1, gw), 1) // 128
    q_grp, slope_vec = [], []
    for g in range(n_lg):
        heads = range(g * LANE_HEADS, (g + 1) * LANE_HEADS)
        q_grp.append(jnp.concatenate([q_ref[:, h * HEAD_DIM:(h + 1) * HEAD_DIM] for h in heads], axis=0))
        sv = jnp.zeros((1, gw), F32)
        for i, h in enumerate(heads):
            sv = jnp.where(lane_head == i, ALIBI_SLOPES[h] * LOG2E, sv)
        slope_vec.append(sv)
        acc_scr[g] = jnp.zeros((HEAD_DIM + DEN_ROWS, gw), F32)

    def att_body(ac, carry):
        base = pl.multiple_of(ac * AC, AC)
        off = (jq * 128 - ac * AC).astype(F32)
        mk = msk_scr[pl.ds(base, AC), :]
        mkw = jnp.concatenate([mk] * LANE_HEADS, axis=1)
        kv_of = [g * LANE_HEADS // GQA_GROUP for g in range(n_lg)]
        raw = [_dot_nt(k_ref[pl.ds(base, AC), kv_of[g] * HEAD_DIM:(kv_of[g] + 1) * HEAD_DIM], q_grp[g])
               for g in range(n_lg)]
        new = []
        for g in range(n_lg):
            kv = kv_of[g]
            m = carry[g]
            x = raw[g] * (ATT_SCALE * LOG2E) - ab_scr[g] + mkw
            cvec = slope_vec[g] * off
            m_new = jnp.maximum(m, jnp.max(x, axis=0, keepdims=True) - cvec)
            alpha = jnp.exp2(m - m_new)
            p = jnp.exp2(x - (m_new + cvec)).astype(BF16)
            vt_c = jnp.concatenate(
                [jnp.concatenate([vt_ref[ac * (AC // 128) + t, kv * HEAD_DIM:(kv + 1) * HEAD_DIM, :]
                                  for t in range(AC // 128)], axis=1),
                 jnp.ones((DEN_ROWS, AC), BF16)], axis=0)
            acc_scr[g] = alpha * acc_scr[g] + _dot(vt_c, p)
            new.append(m_new)
        return tuple(new)

    lax.fori_loop(0, n_ac, att_body, (jnp.full((1, gw), NEG_BIG, F32),) * n_lg)
    for g in range(n_lg):
        out_t = acc_scr[g, :HEAD_DIM, :] / acc_scr[g, HEAD_DIM:HEAD_DIM + 1, :]
        for i in range(LANE_HEADS):
            lo = A_WIDTH + (g * LANE_HEADS + i) * HEAD_DIM
            o_ref[:, lo:lo + HEAD_DIM] = out_t[:, i * 128:(i + 1) * 128].T.astype(o_ref.dtype)

    for g in range(A_GROUPS):
        sl = slice(g * A_CH, (g + 1) * A_CH)
        sp = _dot(ws_ref[g], vn_ref[:, sl]) + bsp_ref[:, sl]
        o_ref[:, sl] = (u_ref[:, sl] * sp).astype(o_ref.dtype)


def _mix(u, vn, q, qi, wit, kb, vt, ki2, ws, bsp, *, batch, seq, n_sel):
    n = batch * seq
    nblk = seq // Q_BLOCK
    ntile = seq // 128
    row_spec = lambda w: pl.BlockSpec((Q_BLOCK, w), lambda b, j: (b * nblk + j, 0))
    in_specs = [
        row_spec(A_WIDTH), row_spec(A_WIDTH), row_spec(ATT_WIDTH), row_spec(IDX_WIDTH),
        pl.BlockSpec((IDX_HEADS, Q_BLOCK), lambda b, j: (0, b * nblk + j)),
        pl.BlockSpec((seq, KV_WIDTH), lambda b, j: (b, 0)),
        pl.BlockSpec((ntile, KV_WIDTH, 128), lambda b, j: (b, 0, 0)),
        pl.BlockSpec((seq, 2 * IDX_DIM), lambda b, j: (b, 0)),
        pl.BlockSpec((A_GROUPS, CHUNK, CHUNK), lambda b, j: (0, 0, 0)),
        pl.BlockSpec((CHUNK, A_WIDTH), lambda b, j: (0, 0)),
    ]
    return pl.pallas_call(
        functools.partial(_mix_kernel, n_sel=n_sel),
        out_shape=jax.ShapeDtypeStruct((n, A_WIDTH + ATT_WIDTH), BF16),
        grid=(batch, nblk),
        in_specs=in_specs,
        out_specs=pl.BlockSpec((Q_BLOCK, A_WIDTH + ATT_WIDTH), lambda b, j: (b * nblk + j, 0)),
        scratch_shapes=[pltpu.VMEM((seq, 128), F32),
                        pltpu.VMEM((seq, 128), F32),
                        pltpu.VMEM((IDX_HEADS // 2, 256, 128), BF16),
                        pltpu.VMEM((N_HEADS // LANE_HEADS, AC, LANE_HEADS * 128), F32),
                        pltpu.VMEM((N_HEADS // LANE_HEADS, HEAD_DIM + DEN_ROWS, LANE_HEADS * 128), F32)],
        compiler_params=_params("arbitrary", "arbitrary"),
        name="mix",
    )(u, vn, q, qi, wit, kb, vt, ki2, ws, bsp)


def _page_copies(pt_ref, row, n_pages, slot, streams, *, for_wait):
    cps = []
    for p in range(n_pages):
        page = 0 if for_wait else pt_ref[row * n_pages + p]
        for hbm, buf, sem, place in streams:
            cps.append(pltpu.make_async_copy(hbm.at[page], buf.at[(slot,) + place(p)], sem.at[slot]))
    return cps


def _gather_pages(pt_ref, n_pages, streams, n_slots):
    b = pl.program_id(0)
    ahead = n_slots - 1

    @pl.when(b == 0)
    def _():
        for r in range(ahead):
            for c in _page_copies(pt_ref, r, n_pages, r, streams, for_wait=False):
                c.start()

    @pl.when(b + ahead < pl.num_programs(0))
    def _():
        for c in _page_copies(pt_ref, b + ahead, n_pages, (b + ahead) % n_slots, streams, for_wait=False):
            c.start()

    slot = b % n_slots
    for c in _page_copies(pt_ref, b, n_pages, slot, streams, for_wait=True):
        c.wait()
    return slot


def _s_scores_kernel(pt_ref, qi_ref, wi_ref, kin_ref, kidx_hbm, sc_ref, self_ref, kbuf, sem, *, n_pages):
    place = lambda p: (slice(None), pl.ds(p * PAGE_SIZE, PAGE_SIZE))
    slot = _gather_pages(pt_ref, n_pages, [(kidx_hbm, kbuf, sem, place)], S_SCORES_SLOTS)
    qi = qi_ref[...]
    w = wi_ref[...] * (IDX_DIM ** -0.5)
    kcat = kbuf[slot].astype(BF16)
    s = _dot(qi, kcat)
    sc_ref[...] = jnp.sum(jnp.maximum(s, 0.0) * w, axis=0, keepdims=True)
    kin = kin_ref[...].astype(BF16).astype(F32)
    s_self = jnp.sum(qi.astype(F32) * kin, axis=1, keepdims=True)
    v_self = jnp.sum(jnp.maximum(s_self, 0.0) * w, axis=0, keepdims=True)
    self_ref[...] = jnp.broadcast_to(v_self, self_ref.shape)


def _s_scores(page_table_flat, qi3, wi_col, ki_new3, cache_kidx_l, *, n_pages):
    bd = qi3.shape[0]
    assert bd >= S_SCORES_SLOTS
    past = n_pages * PAGE_SIZE
    grid_spec = pltpu.PrefetchScalarGridSpec(
        num_scalar_prefetch=1,
        grid=(bd,),
        in_specs=[pl.BlockSpec((None, IDX_HEADS, IDX_DIM), lambda b, pt: (b, 0, 0)),
                  pl.BlockSpec((None, IDX_HEADS, 1), lambda b, pt: (b, 0, 0)),
                  pl.BlockSpec((None, 1, IDX_DIM), lambda b, pt: (b, 0, 0)),
                  pl.BlockSpec(memory_space=pl.ANY)],
        out_specs=(pl.BlockSpec((None, 1, past), lambda b, pt: (b, 0, 0)),
                   pl.BlockSpec((None, 1, 128), lambda b, pt: (b, 0, 0))),
        scratch_shapes=[pltpu.VMEM((S_SCORES_SLOTS, IDX_DIM, past), F32),
                        pltpu.SemaphoreType.DMA((S_SCORES_SLOTS,))],
    )
    return pl.pallas_call(
        functools.partial(_s_scores_kernel, n_pages=n_pages),
        out_shape=(jax.ShapeDtypeStruct((bd, 1, past), F32),
                   jax.ShapeDtypeStruct((bd, 1, 128), F32)),
        grid_spec=grid_spec,
        compiler_params=_params("arbitrary"),
        name="s_scores",
    )(page_table_flat, qi3, wi_col, ki_new3, cache_kidx_l)


def _s_select_kernel(sc_ref, self_ref, u_ref, vn_ref, ws0_ref, bs0_ref,
                     sel_ref, selself_ref, a_ref, *, n_sel):
    keys = sc_ref[...]
    kself = self_ref[:, 0:1]
    bd, past = keys.shape

    def count_ge(cand):
        c = jnp.sum(jnp.where(keys >= cand, 1, 0).astype(I32), axis=1, keepdims=True)
        return c + jnp.where(kself >= cand, 1, 0).astype(I32)

    def bit_body(i, t):
        cand = t ^ lax.shift_left(jnp.int32(1), 31 - i)
        return jnp.where(count_ge(_key_to_float(cand)) >= n_sel, cand, t)

    thr = _key_to_float(lax.fori_loop(0, 32, bit_body, jnp.full((bd, 1), INT_MIN, I32)))
    c_gt = (jnp.sum(jnp.where(keys > thr, 1, 0).astype(I32), axis=1, keepdims=True)
            + jnp.where(kself > thr, 1, 0).astype(I32))
    need = (n_sel - c_gt).astype(F32)
    r = lax.broadcasted_iota(I32, (128, 128), 0)
    c = lax.broadcasted_iota(I32, (128, 128), 1)
    ustrict = jnp.where(r < c, 1.0, 0.0).astype(BF16)
    before = jnp.zeros((bd, 1), F32)
    for t in range(past // 128):
        kt = keys[:, t * 128:(t + 1) * 128]
        eq = kt == thr
        eqf = jnp.where(eq, 1.0, 0.0).astype(F32)
        rank = before + _dot(eqf.astype(BF16), ustrict)
        sel = jnp.logical_or(kt > thr, jnp.logical_and(eq, rank < need))
        sel_ref[:, t * 128:(t + 1) * 128] = jnp.where(sel, 1.0, 0.0).astype(F32)
        before = before + jnp.sum(eqf, axis=1, keepdims=True)
    sel_self = jnp.logical_or(kself > thr, jnp.logical_and(kself == thr, before < need))
    selself_ref[...] = jnp.broadcast_to(jnp.where(sel_self, 1.0, 0.0).astype(F32), selself_ref.shape)
    sp = ws0_ref[...] * vn_ref[...] + bs0_ref[...]
    a_ref[...] = (u_ref[...] * sp).astype(a_ref.dtype)


def _s_select(sc, sc_self, u, vn, ws0, bs0, *, n_sel):
    bd, past = sc.shape
    full = lambda a: pl.BlockSpec(a.shape, lambda i: (0,) * a.ndim)
    args = (sc, sc_self, u, vn, ws0, bs0)
    return pl.pallas_call(
        functools.partial(_s_select_kernel, n_sel=n_sel),
        out_shape=(jax.ShapeDtypeStruct((bd, past), F32),
                   jax.ShapeDtypeStruct((bd, 128), F32),
                   jax.ShapeDtypeStruct((bd, A_WIDTH), BF16)),
        grid=(1,),
        in_specs=[full(a) for a in args],
        out_specs=(pl.BlockSpec((bd, past), lambda i: (0, 0)),
                   pl.BlockSpec((bd, 128), lambda i: (0, 0)),
                   pl.BlockSpec((bd, A_WIDTH), lambda i: (0, 0))),
        compiler_params=_params("arbitrary"),
        name="s_select",
    )(*args)


def _s_attn_kernel(pt_ref, q_ref, sel_ref, selself_ref, knew_ref, vnew_ref, ck_hbm, cv_hbm, o_ref,
                   kbuf, vbuf, ksem, vsem, *, n_pages, past):
    place = lambda p: (pl.ds(p * 2 * PAGE_SIZE, 2 * PAGE_SIZE), slice(None))
    slot = _gather_pages(pt_ref, n_pages, [(ck_hbm, kbuf, ksem, place), (cv_hbm, vbuf, vsem, place)], S_ATTN_SLOTS)
    q = q_ref[...]
    hrow = lax.broadcasted_iota(I32, (N_HEADS, 2 * past), 0)
    ccol = lax.broadcasted_iota(I32, (N_HEADS, 2 * past), 1)
    own_kv = (ccol & 1) == (hrow // GQA_GROUP)
    hcol = lax.broadcasted_iota(I32, (N_HEADS, 1), 0)
    slope = jnp.zeros((N_HEADS, 1), F32)
    for h in range(N_HEADS):
        slope = jnp.where(hcol == h, ALIBI_SLOPES[h], slope)

    lg = _dot_nt(q, kbuf[slot].astype(BF16)) * ATT_SCALE
    lg = lg - slope * (past - (ccol >> 1)).astype(F32)
    lg = jnp.where(jnp.logical_and(own_kv, sel_ref[...] > 0.5), lg, -jnp.inf)
    first_group = lax.broadcasted_iota(I32, (N_HEADS, HEAD_DIM), 0) < GQA_GROUP
    knew = knew_ref[...].astype(BF16).astype(F32)
    vnew = vnew_ref[...].astype(BF16).astype(F32)
    knew8 = jnp.where(first_group, knew[0:1, :], knew[1:2, :])
    vnew8 = jnp.where(first_group, vnew[0:1, :], vnew[1:2, :])
    lg_self = jnp.sum(q.astype(F32) * knew8, axis=1, keepdims=True) * ATT_SCALE
    lg_self = jnp.where(selself_ref[:, 0:1] > 0.5, lg_self, -jnp.inf)

    m = jnp.maximum(jnp.maximum(lg_self, NEG_BIG), jnp.max(lg, axis=1, keepdims=True))
    p_self = jnp.exp(lg_self - m)
    pp = jnp.exp(lg - m)
    l = p_self + jnp.sum(pp, axis=1, keepdims=True)
    acc = p_self.astype(BF16).astype(F32) * vnew8 + _dot(pp.astype(BF16), vbuf[slot].astype(BF16))
    o_ref[...] = (acc / l).astype(o_ref.dtype)


def _s_attn(page_table_flat, q3, sel2, sel_self3, k_new3, v_new3, cache_k2, cache_v2, *, n_pages, past):
    bd = q3.shape[0]
    assert bd >= S_ATTN_SLOTS
    per_b = lambda s1, s2: pl.BlockSpec((None, s1, s2), lambda b, pt: (b, 0, 0))
    grid_spec = pltpu.PrefetchScalarGridSpec(
        num_scalar_prefetch=1,
        grid=(bd,),
        in_specs=[per_b(N_HEADS, HEAD_DIM), per_b(1, 2 * past), per_b(1, 128),
                  per_b(N_KV_HEADS, HEAD_DIM), per_b(N_KV_HEADS, HEAD_DIM),
                  pl.BlockSpec(memory_space=pl.ANY), pl.BlockSpec(memory_space=pl.ANY)],
        out_specs=per_b(N_HEADS, HEAD_DIM),
        scratch_shapes=[pltpu.VMEM((S_ATTN_SLOTS, 2 * past, HEAD_DIM), F32),
                        pltpu.VMEM((S_ATTN_SLOTS, 2 * past, HEAD_DIM), F32),
                        pltpu.SemaphoreType.DMA((S_ATTN_SLOTS,)), pltpu.SemaphoreType.DMA((S_ATTN_SLOTS,))],
    )
    return pl.pallas_call(
        functools.partial(_s_attn_kernel, n_pages=n_pages, past=past),
        out_shape=jax.ShapeDtypeStruct((bd, N_HEADS, HEAD_DIM), BF16),
        grid_spec=grid_spec,
        compiler_params=_params("arbitrary"),
        name="s_attn",
    )(page_table_flat, q3, sel2, sel_self3, k_new3, v_new3, cache_k2, cache_v2)


def _outln_kernel(mix_ref, x_ref, gate_ref, w_ref, g_ref, b_ref, o_ref, *wb_ref, alpha):
    w = w_ref[...]
    if wb_ref:
        w = w.astype(BF16)
        wb_ref[0][...] = w
    mix = _dot(mix_ref[...], w)
    y = alpha * x_ref[...] + (1.0 + gate_ref[...]) * mix
    o_ref[...] = _layer_norm(y, g_ref[...], b_ref[...])


def _outln(mixin, x, gate, w, g, b, *, tm, alpha):
    n, d = x.shape
    nb, r, _ = gate.shape
    tiles_per_mod = (n // nb) // tm
    kin = mixin.shape[1]
    emit = w.dtype != BF16
    assert not emit or n == tm
    out_shape = [jax.ShapeDtypeStruct((n, d), F32)]
    out_specs = [pl.BlockSpec((tm, d), lambda i: (i, 0))]
    if emit:
        out_shape.append(jax.ShapeDtypeStruct((kin, d), BF16))
        out_specs.append(pl.BlockSpec((kin, d), lambda i: (0, 0)))
    res = pl.pallas_call(
        functools.partial(_outln_kernel, alpha=alpha),
        out_shape=out_shape,
        grid=(n // tm,),
        in_specs=[pl.BlockSpec((tm, kin), lambda i: (i, 0)),
                  pl.BlockSpec((tm, d), lambda i: (i, 0)),
                  pl.BlockSpec((None, r, d), lambda i: (i // tiles_per_mod, 0, 0)),
                  pl.BlockSpec((kin, d), lambda i: (0, 0), pipeline_mode=pl.Buffered(1)),
                  pl.BlockSpec((1, d), lambda i: (0, 0)),
                  pl.BlockSpec((1, d), lambda i: (0, 0))],
        out_specs=out_specs,
        compiler_params=_params("arbitrary"),
        name="outln",
    )(mixin, x, gate, w, g, b)
    return res if emit else res[0]


def _ffn_kernel(x_ref, shift_ref, scale_ref, gate_ref, w1_ref, b1_ref, w2_ref, b2_ref, g_ref, b_ref,
                o_ref, *rest, alpha):
    *wb_refs, h_scr = rest
    f = pl.program_id(1)

    @pl.when(f == 0)
    def _():
        h_scr[...] = (x_ref[...] * (1.0 + scale_ref[...]) + shift_ref[...]).astype(BF16)
        o_ref[...] = jnp.zeros_like(o_ref)

    if wb_refs:
        wb_refs[0][...] = w1_ref[...].astype(BF16)
        wb_refs[1][...] = w2_ref[...].astype(BF16)
        w1_ref, w2_ref = wb_refs
    a = jnp.maximum(_dot(h_scr[...], w1_ref[...]) + b1_ref[...], 0.0)
    a2 = (a * a).astype(BF16)
    for c in range(o_ref.shape[1] // FFN_TN):
        cs = slice(c * FFN_TN, (c + 1) * FFN_TN)
        o_ref[:, cs] += _dot(a2, w2_ref[:, cs])

    @pl.when(f == pl.num_programs(1) - 1)
    def _():
        y = alpha * x_ref[...] + (1.0 + gate_ref[...]) * (o_ref[...] + b2_ref[...])
        o_ref[...] = _layer_norm(y, g_ref[...], b_ref[...])


def _ffn(x, shift, scale, gate, w1, b1, w2, b2, g, b, *, tm, tf, alpha):
    n, d = x.shape
    nb, r, _ = gate.shape
    dff = w1.shape[1]
    tiles_per_mod = (n // nb) // tm
    mod_spec = pl.BlockSpec((None, r, d), lambda i, f: (i // tiles_per_mod, 0, 0))
    emit = w1.dtype != BF16
    assert not emit or n == tm
    out_shape = [jax.ShapeDtypeStruct((n, d), F32)]
    out_specs = [pl.BlockSpec((tm, d), lambda i, f: (i, 0))]
    if emit:
        out_shape += [jax.ShapeDtypeStruct((d, dff), BF16), jax.ShapeDtypeStruct((dff, d), BF16)]
        out_specs += [pl.BlockSpec((d, tf), lambda i, f: (0, f)), pl.BlockSpec((tf, d), lambda i, f: (f, 0))]
    res = pl.pallas_call(
        functools.partial(_ffn_kernel, alpha=alpha),
        out_shape=out_shape,
        grid=(n // tm, dff // tf),
        in_specs=[pl.BlockSpec((tm, d), lambda i, f: (i, 0), pipeline_mode=pl.Buffered(1)),
                  mod_spec, mod_spec, mod_spec,
                  pl.BlockSpec((d, tf), lambda i, f: (0, f)),
                  pl.BlockSpec((1, tf), lambda i, f: (0, f)),
                  pl.BlockSpec((tf, d), lambda i, f: (f, 0)),
                  pl.BlockSpec((1, d), lambda i, f: (0, 0)),
                  pl.BlockSpec((1, d), lambda i, f: (0, 0)),
                  pl.BlockSpec((1, d), lambda i, f: (0, 0))],
        out_specs=out_specs,
        scratch_shapes=[pltpu.VMEM((tm, d), BF16)],
        compiler_params=_params("arbitrary", "arbitrary"),
        name="ffn",
    )(x, shift, scale, gate, w1, b1, w2, b2, g, b)
    return res if emit else res[0]


def kernel(x_prompt, x_sample, cache_k, cache_v, cache_kidx, page_table, c_prompt, c_sample,
           w_cond, b_cond, w_in, ln_v_g, ln_v_b, w_spatial, b_spatial, w_out,
           ln1_g, ln1_b, w_ff1, b_ff1, w_ff2, b_ff2, ln2_g, ln2_b):
    batch, seq, d = x_prompt.shape
    bd, ts, _ = x_sample.shape
    depth = w_in.shape[0]
    n_pages = page_table.shape[1]
    past = n_pages * PAGE_SIZE
    n_pool = cache_k.shape[1]
    dff = w_ff1.shape[2]
    assert ts == 1 and seq % KC == 0 and d == A_WIDTH + ATT_WIDTH
    assert w_in.shape[2] == MAIN_WIDTH + IDX_DIM + IDX_HEADS
    n = batch * seq
    alpha = (2 * depth) ** 0.25
    n_sel_p = min(TOPK_MAX, seq // 4)
    n_sel_s = min(TOPK_MAX, (past + ts) // 4)
    tm_p = min(1024, seq)
    tm_o = min(512, seq)
    tf = min(1024, dff)
    pt_flat = page_table.reshape(-1).astype(I32)

    xp = x_prompt.reshape(n, d)
    xs = x_sample.reshape(bd, d)
    c_all = jnp.concatenate([c_prompt, c_sample], axis=0)

    outs = {k: [] for k in ("kp", "vp", "kip", "ks", "vs", "kis", "vc")}
    for l in range(depth):
        wm = jnp.swapaxes(w_in[l], 0, 1).astype(BF16)
        wt = jnp.pad(wm[MAIN_WIDTH:], ((0, 2 * IDX_DIM - IDX_DIM - IDX_HEADS), (0, 0)))
        kidx_t = jnp.swapaxes(cache_kidx[l], 1, 2)
        lng = ln_v_g[l].reshape(1, A_WIDTH)
        lnb = ln_v_b[l].reshape(1, A_WIDTH)
        tril = jnp.tril(jnp.ones((CHUNK, CHUNK), dtype=bool))
        ws = jnp.where(tril[None], w_spatial[l], 0.0).astype(BF16)
        bsp = jnp.repeat(jnp.transpose(b_spatial[l]), A_CH, axis=1)
        ws0 = jnp.repeat(w_spatial[l][:, 0, 0], A_CH).reshape(1, A_WIDTH)
        bs0 = jnp.repeat(b_spatial[l][:, 0], A_CH).reshape(1, A_WIDTH)
        b1 = b_ff1[l].reshape(1, dff)
        b2 = b_ff2[l].reshape(1, d)
        g1, be1 = ln1_g[l].reshape(1, d), ln1_b[l].reshape(1, d)
        g2, be2 = ln2_g[l].reshape(1, d), ln2_b[l].reshape(1, d)

        z = _cond(c_all, w_cond[l], b_cond[l])
        mods = [z[:, i * d:(i + 1) * d] for i in range(N_MOD)]
        mp = [m[:batch].reshape(batch, 1, d) for m in mods]
        ms = [m[batch:].reshape(1, bd, d) for m in mods]

        (u, vn, q, k, v, _, _, qi, ki, _, wit) = _proj(
            xs, ms[0], ms[1], wm, wt, lng, lnb, tm=bd, vn_dtype=F32)
        sc, sc_self = _s_scores(pt_flat, qi.reshape(bd, IDX_HEADS, IDX_DIM),
                                jnp.transpose(wit).reshape(bd, IDX_HEADS, 1),
                                ki.reshape(bd, 1, IDX_DIM), kidx_t, n_pages=n_pages)
        sel, sel_self, a_out = _s_select(sc.reshape(bd, past), sc_self.reshape(bd, 128), u, vn, ws0, bs0,
                                         n_sel=n_sel_s)
        sel2 = jnp.repeat(sel, 2, axis=1).reshape(bd, 1, 2 * past)
        b_out = _s_attn(pt_flat, q.reshape(bd, N_HEADS, HEAD_DIM), sel2, sel_self.reshape(bd, 1, 128),
                        k.reshape(bd, N_KV_HEADS, HEAD_DIM), v.reshape(bd, N_KV_HEADS, HEAD_DIM),
                        cache_k[l].reshape(n_pool, 2 * PAGE_SIZE, HEAD_DIM),
                        cache_v[l].reshape(n_pool, 2 * PAGE_SIZE, HEAD_DIM),
                        n_pages=n_pages, past=past)
        mixin = jnp.concatenate([a_out, b_out.reshape(bd, ATT_WIDTH)], axis=1)
        x1, w_out_b = _outln(mixin, xs, ms[2], w_out[l], g1, be1, tm=bd, alpha=alpha)
        xs, w1_b, w2_b = _ffn(x1, ms[3], ms[4], ms[5], w_ff1[l], b1, w_ff2[l], b2, g2, be2, tm=bd, tf=tf, alpha=alpha)
        outs["ks"].append(k.reshape(bd, ts, N_KV_HEADS, HEAD_DIM))
        outs["vs"].append(v.reshape(bd, ts, N_KV_HEADS, HEAD_DIM))
        outs["kis"].append(ki.reshape(bd, ts, IDX_DIM))
        outs["vc"].append(vn.reshape(bd, ts, A_GROUPS, A_CH))

        (u, vn, q, k, v, kb, vt, qi, ki, ki2, wit) = _proj(
            xp, mp[0], mp[1], wm, wt, lng, lnb, tm=tm_p, vn_dtype=BF16)
        mixin = _mix(u, vn, q, qi, wit, kb, vt, ki2, ws, bsp, batch=batch, seq=seq, n_sel=n_sel_p)
        x1 = _outln(mixin, xp, mp[2], w_out_b, g1, be1, tm=tm_o, alpha=alpha)
        xp = _ffn(x1, mp[3], mp[4], mp[5], w1_b, b1, w2_b, b2, g2, be2, tm=tm_p, tf=tf, alpha=alpha)
        outs["kp"].append(k.reshape(batch, seq, N_KV_HEADS, HEAD_DIM))
        outs["vp"].append(v.reshape(batch, seq, N_KV_HEADS, HEAD_DIM))
        outs["kip"].append(ki.reshape(batch, seq, IDX_DIM))

    st = lambda name: jnp.stack(outs[name])
    return (xp.reshape(batch, seq, d), xs.reshape(bd, ts, d),
            st("kp"), st("vp"), st("kip"), st("ks"), st("vs"), st("kis"), st("vc"))
```

```python
import functools

import jax
import jax.numpy as jnp
import numpy as np
from jax import lax
from jax.experimental import pallas as pl
from jax.experimental.pallas import tpu as pltpu

F32 = jnp.float32
BF16 = jnp.bfloat16
I32 = jnp.int32

CHUNK = 128
A_GROUPS = 8
A_CH = 128
A_WIDTH = A_GROUPS * A_CH
HEAD_DIM = 128
N_HEADS = 8
N_KV_HEADS = 2
GQA_GROUP = N_HEADS // N_KV_HEADS
ATT_WIDTH = N_HEADS * HEAD_DIM
KV_WIDTH = N_KV_HEADS * HEAD_DIM
IDX_HEADS = 16
IDX_DIM = 64
IDX_WIDTH = IDX_HEADS * IDX_DIM
TOPK_MAX = 256
Q_BLOCK = 128
PAGE_SIZE = 128
N_MOD = 6
LN_EPS = 1e-5
ATT_SCALE = HEAD_DIM ** -0.5
MAIN_WIDTH = 2 * A_WIDTH + ATT_WIDTH + 2 * KV_WIDTH + IDX_WIDTH
PROJ_TN = 512
FFN_TN = 512
OUTLN_SLAB = 256
ALIBI_SLOPES = tuple(float(2.0 ** (-8.0 * h / N_HEADS)) for h in range(1, N_HEADS + 1))

VMEM_LIMIT_BYTES = 58 * 1024 * 1024
INT_MIN = -(2 ** 31)
NEG_BIG = -1e30
FLT_MAX = float(np.finfo(np.float32).max)
LOG2E = float(np.log2(np.e))
KC = 512
AC = 512
S_SCORES_SLOTS = 4
S_ATTN_SLOTS = 3
LANE_HEADS = 4
DEN_ROWS = 16

NT_DIMS = (((1,), (1,)), ((), ()))


def _dot(a, b):
    return jnp.dot(a, b, preferred_element_type=F32)


def _dot_nt(a, b):
    return lax.dot_general(a, b, NT_DIMS, preferred_element_type=F32)


def _layer_norm(x, g, b):
    mu = jnp.mean(x, axis=-1, keepdims=True)
    xc = x - mu
    var = jnp.mean(xc * xc, axis=-1, keepdims=True)
    return xc * lax.rsqrt(var + LN_EPS) * g + b


def _key_to_float(key):
    bits = jnp.where(key < 0, key ^ jnp.int32(0x7FFFFFFF), key)
    return pltpu.bitcast(bits, F32)


def _params(*sem):
    return pltpu.CompilerParams(dimension_semantics=sem, vmem_limit_bytes=VMEM_LIMIT_BYTES)


def _cond_kernel(c_ref, w_ref, b_ref, o_ref):
    c = c_ref[...]
    a = (c * jax.nn.sigmoid(c)).astype(BF16)
    o_ref[...] = _dot(a, w_ref[...].astype(BF16)) + b_ref[...]


def _cond(c, w, b):
    m, d = c.shape
    n = w.shape[1]
    tn = 1024
    return pl.pallas_call(
        _cond_kernel,
        out_shape=jax.ShapeDtypeStruct((m, n), F32),
        grid=(n // tn,),
        in_specs=[pl.BlockSpec((m, d), lambda j: (0, 0)),
                  pl.BlockSpec((d, tn), lambda j: (0, j)),
                  pl.BlockSpec((1, tn), lambda j: (0, j))],
        out_specs=pl.BlockSpec((m, tn), lambda j: (0, j)),
        compiler_params=_params("arbitrary"),
        name="cond",
    )(c, w, b.reshape(1, n))


_J_V, _J_Q, _J_KV, _J_QI, _J_END = 2, 4, 6, 7, 9


def _proj_kernel(x_ref, shift_ref, scale_ref, wm_ref, wt_ref, lng_ref, lnb_ref,
                 u_ref, vn_ref, q_ref, k_ref, v_ref, kb_ref, vt_ref, qi_ref, ki_ref, ki2_ref, wit_ref,
                 h_scr):
    j = pl.program_id(1)

    @pl.when(j == 0)
    def _():
        h = (x_ref[...] * (1.0 + scale_ref[...]) + shift_ref[...]).astype(BF16)
        h_scr[...] = h
        tail = _dot_nt(h, wt_ref[...])
        ki_ref[...] = tail[:, :IDX_DIM]
        lane = lax.broadcasted_iota(I32, tail.shape, 1)
        ki2_ref[...] = jnp.where(lane < IDX_DIM, tail, pltpu.roll(tail, IDX_DIM, axis=1)).astype(BF16)
        wit_ref[...] = tail.T[IDX_DIM:IDX_DIM + IDX_HEADS, :] * (IDX_HEADS ** -0.5)

    def tile():
        return _dot_nt(h_scr[...], wm_ref[...])

    @pl.when(j < _J_V)
    def _():
        u_ref[...] = tile()

    @pl.when(jnp.logical_and(j >= _J_V, j < _J_Q))
    def _():
        z = tile()
        for g in range(PROJ_TN // A_CH):
            sl = slice(g * A_CH, (g + 1) * A_CH)
            vn_ref[:, sl] = _layer_norm(z[:, sl], lng_ref[:, sl], lnb_ref[:, sl]).astype(vn_ref.dtype)

    @pl.when(jnp.logical_and(j >= _J_Q, j < _J_KV))
    def _():
        q_ref[...] = tile().astype(BF16)

    @pl.when(j == _J_KV)
    def _():
        z = tile()
        for hh in range(N_KV_HEADS):
            k_ref[:, hh, :] = z[:, hh * HEAD_DIM:(hh + 1) * HEAD_DIM]
            v_ref[:, hh, :] = z[:, KV_WIDTH + hh * HEAD_DIM:KV_WIDTH + (hh + 1) * HEAD_DIM]
        kb_ref[...] = z[:, :KV_WIDTH].astype(BF16)
        vt = z[:, KV_WIDTH:].T.astype(BF16)
        for c in range(vt_ref.shape[0]):
            vt_ref[c] = vt[:, c * 128:(c + 1) * 128]

    @pl.when(j >= _J_QI)
    def _():
        qi_ref[...] = tile().astype(BF16)


def _proj(x, shift, scale, wm, wt, lng, lnb, *, tm, vn_dtype):
    n, d = x.shape
    nb, r, _ = shift.shape
    rows_per_mod = n // nb
    assert n % tm == 0 and rows_per_mod % tm == 0 and r in (1, tm)
    tiles_per_mod = rows_per_mod // tm
    tn = PROJ_TN

    def clipj(lo, cnt):
        return lambda i, j: (i, jnp.clip(j - lo, 0, cnt - 1))

    mod_spec = pl.BlockSpec((None, r, d), lambda i, j: (i // tiles_per_mod, 0, 0))
    out_shape = (
        jax.ShapeDtypeStruct((n, A_WIDTH), F32),
        jax.ShapeDtypeStruct((n, A_WIDTH), vn_dtype),
        jax.ShapeDtypeStruct((n, ATT_WIDTH), BF16),
        jax.ShapeDtypeStruct((n, N_KV_HEADS, HEAD_DIM), F32),
        jax.ShapeDtypeStruct((n, N_KV_HEADS, HEAD_DIM), F32),
        jax.ShapeDtypeStruct((n, KV_WIDTH), BF16),
        jax.ShapeDtypeStruct((n // 128, KV_WIDTH, 128), BF16),
        jax.ShapeDtypeStruct((n, IDX_WIDTH), BF16),
        jax.ShapeDtypeStruct((n, IDX_DIM), F32),
        jax.ShapeDtypeStruct((n, 2 * IDX_DIM), BF16),
        jax.ShapeDtypeStruct((IDX_HEADS, n), F32),
    )
    out_specs = (
        pl.BlockSpec((tm, tn), clipj(0, 2)),
        pl.BlockSpec((tm, tn), clipj(_J_V, 2)),
        pl.BlockSpec((tm, tn), clipj(_J_Q, 2)),
        pl.BlockSpec((tm, N_KV_HEADS, HEAD_DIM), lambda i, j: (i, 0, 0)),
        pl.BlockSpec((tm, N_KV_HEADS, HEAD_DIM), lambda i, j: (i, 0, 0)),
        pl.BlockSpec((tm, KV_WIDTH), lambda i, j: (i, 0)),
        pl.BlockSpec((tm // 128, KV_WIDTH, 128), lambda i, j: (i, 0, 0)),
        pl.BlockSpec((tm, tn), clipj(_J_QI, 2)),
        pl.BlockSpec((tm, IDX_DIM), lambda i, j: (i, 0)),
        pl.BlockSpec((tm, 2 * IDX_DIM), lambda i, j: (i, 0)),
        pl.BlockSpec((IDX_HEADS, tm), lambda i, j: (0, i)),
    )
    in_specs = [
        pl.BlockSpec((tm, d), lambda i, j: (i, 0)),
        mod_spec, mod_spec,
        pl.BlockSpec((tn, d), lambda i, j: (j, 0)),
        pl.BlockSpec((2 * IDX_DIM, d), lambda i, j: (0, 0)),
        pl.BlockSpec((1, tn), lambda i, j: (0, jnp.clip(j - _J_V, 0, 1))),
        pl.BlockSpec((1, tn), lambda i, j: (0, jnp.clip(j - _J_V, 0, 1))),
    ]
    return pl.pallas_call(
        _proj_kernel,
        out_shape=out_shape,
        grid=(n // tm, _J_END),
        in_specs=in_specs,
        out_specs=out_specs,
        scratch_shapes=[pltpu.VMEM((tm, d), BF16)],
        compiler_params=_params("arbitrary", "arbitrary"),
        name="proj",
    )(x, shift, scale, wm, wt, lng, lnb)


def _mix_kernel(u_ref, vn_ref, q_ref, qi_ref, wit_ref, k_ref, vt_ref, ki2_ref, ws_ref, bsp_ref,
                o_ref, sc_scr, msk_scr, qim_scr, ab_scr, acc_scr, thr_scr, *, n_sel):
    jq = pl.program_id(1)
    n_kc = jq // (KC // 128) + 1
    n_ac = jq // (AC // 128) + 1
    row = lax.broadcasted_iota(I32, (128, 128), 0)
    col = lax.broadcasted_iota(I32, (128, 128), 1)

    @pl.when(jnp.logical_and(pl.program_id(0) == 0, jq == 0))
    def _():
        d0 = (lax.broadcasted_iota(I32, (AC, 128), 1) - lax.broadcasted_iota(I32, (AC, 128), 0)).astype(F32)
        for h in range(N_HEADS):
            lanes = slice((h % LANE_HEADS) * 128, (h % LANE_HEADS + 1) * 128)
            ab_scr[h // LANE_HEADS, :, lanes] = (ALIBI_SLOPES[h] * LOG2E) * d0

    for p in range(IDX_HEADS // 2):
        pair = qi_ref[:, p * 128:(p + 1) * 128]
        zero = jnp.zeros_like(pair)
        qim_scr[p, 0:128, :] = jnp.where(col < IDX_DIM, pair, zero)
        qim_scr[p, 128:256, :] = jnp.where(col >= IDX_DIM, pair, zero)
    wis = wit_ref[...] * (IDX_DIM ** -0.5)
    key_minus_query = (lax.broadcasted_iota(I32, (KC, 128), 0) - lax.broadcasted_iota(I32, (KC, 128), 1))

    def idx_body(kc, carry):
        base = pl.multiple_of(kc * KC, KC)
        kk = ki2_ref[pl.ds(base, KC), :]
        acc = jnp.zeros((KC, 128), F32)
        for p in range(IDX_HEADS // 2):
            s = _dot_nt(kk, qim_scr[p])
            acc = (acc + jnp.maximum(s[:, :128], 0.0) * wis[2 * p:2 * p + 1, :]
                   + jnp.maximum(s[:, 128:], 0.0) * wis[2 * p + 1:2 * p + 2, :])
        inadmissible = key_minus_query > (jq * 128 - kc * KC)
        sc_scr[pl.ds(base, KC), :] = jnp.where(inadmissible, -jnp.inf, acc)
        return carry

    lax.fori_loop(0, n_kc, idx_body, 0)

    def count(pred):
        def body(kc, c):
            x = sc_scr[pl.ds(pl.multiple_of(kc * KC, KC), KC), :]
            m = jnp.where(pred(x), 1, 0).astype(I32)
            return c + jnp.sum(m.reshape(KC // 8, 8, 128), axis=0)
        c = lax.fori_loop(0, n_kc, body, jnp.zeros((8, 128), I32))
        return jnp.sum(c, axis=0, keepdims=True)

    def search(k):
        def bit_body(i, t):
            cand = t ^ lax.shift_left(jnp.int32(1), 31 - i)
            cand_f = _key_to_float(cand)
            c = jnp.zeros((8, 128), I32)
            for kc in range(k):
                m = jnp.where(sc_scr[kc * KC:(kc + 1) * KC, :] >= cand_f, 1, 0).astype(I32)
                c = c + jnp.sum(m.reshape(KC // 8, 8, 128), axis=0)
            return jnp.where(jnp.sum(c, axis=0, keepdims=True) >= n_sel, cand, t)

        t = lax.fori_loop(0, 32, bit_body, jnp.full((1, 128), INT_MIN, I32))
        thr_scr[...] = jnp.broadcast_to(t, thr_scr.shape)

    for k in range(1, sc_scr.shape[0] // KC + 1):
        pl.when(n_kc == k)(functools.partial(search, k))
    thr_key = thr_scr[0:1, :]
    thr = jnp.where(thr_key == jnp.int32(INT_MIN), -FLT_MAX, _key_to_float(thr_key))
    c_ge = count(lambda x: x >= thr)
    c_gt = count(lambda x: x > thr)
    has_tie_overflow = jnp.max(c_ge) > n_sel

    @pl.when(jnp.logical_not(has_tie_overflow))
    def _():
        def body(kc, carry):
            sl = pl.ds(pl.multiple_of(kc * KC, KC), KC)
            msk_scr[sl, :] = jnp.where(sc_scr[sl, :] >= thr, 0.0, -jnp.inf).astype(F32)
            return carry
        lax.fori_loop(0, n_kc, body, 0)

    @pl.when(has_tie_overflow)
    def _():
        need = (n_sel - c_gt).astype(F32)
        lstrict = jnp.where(col < row, 1.0, 0.0).astype(BF16)

        def body(kt, before):
            sl = pl.ds(pl.multiple_of(kt * 128, 128), 128)
            x = sc_scr[sl, :]
            eq = x == thr
            eqf = jnp.where(eq, 1.0, 0.0).astype(F32)
            rank = before + _dot(lstrict, eqf.astype(BF16))
            sel = jnp.logical_or(x > thr, jnp.logical_and(eq, rank < need))
            msk_scr[sl, :] = jnp.where(sel, 0.0, -jnp.inf).astype(F32)
            return before + jnp.sum(eqf, axis=0, keepdims=True)
        lax.fori_loop(0, n_kc * (KC // 128), body, jnp.zeros((1, 128), F32))

    gw = LANE_HEADS * 128
    n_lg = N_HEADS // LANE_HEADS
    lane_head = lax.broadcasted_iota(I32, (1, gw), 1) // 128
    q_grp, slope_vec = [], []
    for g in range(n_lg):
        heads = range(g * LANE_HEADS, (g + 1) * LANE_HEADS)
        q_grp.append(jnp.concatenate([q_ref[:, h * HEAD_DIM:(h + 1) * HEAD_DIM] for h in heads], axis=0))
        sv = jnp.zeros((1, gw), F32)
        for i, h in enumerate(heads):
            sv = jnp.where(lane_head == i, ALIBI_SLOPES[h] * LOG2E, sv)
        slope_vec.append(sv)
        acc_scr[g] = jnp.zeros((HEAD_DIM + DEN_ROWS, gw), F32)

    def att_body(ac, carry):
        base = pl.multiple_of(ac * AC, AC)
        off = (jq * 128 - ac * AC).astype(F32)
        mk = msk_scr[pl.ds(base, AC), :]
        mkw = jnp.concatenate([mk] * LANE_HEADS, axis=1)
        kv_of = [g * LANE_HEADS // GQA_GROUP for g in range(n_lg)]
        raw = [_dot_nt(k_ref[pl.ds(base, AC), kv_of[g] * HEAD_DIM:(kv_of[g] + 1) * HEAD_DIM], q_grp[g])
               for g in range(n_lg)]
        new = []
        for g in range(n_lg):
            kv = kv_of[g]
            m = carry[g]
            x = raw[g] * (ATT_SCALE * LOG2E) - ab_scr[g] + mkw
            cvec = slope_vec[g] * off
            m_new = jnp.maximum(m, jnp.max(x, axis=0, keepdims=True) - cvec)
            alpha = jnp.exp2(m - m_new)
            p = jnp.exp2(x - (m_new + cvec)).astype(BF16)
            vt_c = jnp.concatenate(
                [jnp.concatenate([vt_ref[ac * (AC // 128) + t, kv * HEAD_DIM:(kv + 1) * HEAD_DIM, :]
                                  for t in range(AC // 128)], axis=1),
                 jnp.ones((DEN_ROWS, AC), BF16)], axis=0)
            acc_scr[g] = alpha * acc_scr[g] + _dot(vt_c, p)
            new.append(m_new)
        return tuple(new)

    lax.fori_loop(0, n_ac, att_body, (jnp.full((1, gw), NEG_BIG, F32),) * n_lg)
    for g in range(n_lg):
        out_t = acc_scr[g, :HEAD_DIM, :] / acc_scr[g, HEAD_DIM:HEAD_DIM + 1, :]
        for i in range(LANE_HEADS):
            lo = A_WIDTH + (g * LANE_HEADS + i) * HEAD_DIM
            o_ref[:, lo:lo + HEAD_DIM] = out_t[:, i * 128:(i + 1) * 128].T.astype(o_ref.dtype)

    for g in range(A_GROUPS):
        sl = slice(g * A_CH, (g + 1) * A_CH)
        sp = _dot(ws_ref[g], vn_ref[:, sl]) + bsp_ref[:, sl]
        o_ref[:, sl] = (u_ref[:, sl] * sp).astype(o_ref.dtype)


def _mix(u, vn, q, qi, wit, kb, vt, ki2, ws, bsp, *, batch, seq, n_sel):
    n = batch * seq
    nblk = seq // Q_BLOCK
    ntile = seq // 128
    row_spec = lambda w: pl.BlockSpec((Q_BLOCK, w), lambda b, j: (b * nblk + j, 0))
    in_specs = [
        row_spec(A_WIDTH), row_spec(A_WIDTH), row_spec(ATT_WIDTH), row_spec(IDX_WIDTH),
        pl.BlockSpec((IDX_HEADS, Q_BLOCK), lambda b, j: (0, b * nblk + j)),
        pl.BlockSpec((seq, KV_WIDTH), lambda b, j: (b, 0)),
        pl.BlockSpec((ntile, KV_WIDTH, 128), lambda b, j: (b, 0, 0)),
        pl.BlockSpec((seq, 2 * IDX_DIM), lambda b, j: (b, 0)),
        pl.BlockSpec((A_GROUPS, CHUNK, CHUNK), lambda b, j: (0, 0, 0)),
        pl.BlockSpec((CHUNK, A_WIDTH), lambda b, j: (0, 0)),
    ]
    return pl.pallas_call(
        functools.partial(_mix_kernel, n_sel=n_sel),
        out_shape=jax.ShapeDtypeStruct((n, A_WIDTH + ATT_WIDTH), BF16),
        grid=(batch, nblk),
        in_specs=in_specs,
        out_specs=pl.BlockSpec((Q_BLOCK, A_WIDTH + ATT_WIDTH), lambda b, j: (b * nblk + j, 0)),
        scratch_shapes=[pltpu.VMEM((seq, 128), F32),
                        pltpu.VMEM((seq, 128), F32),
                        pltpu.VMEM((IDX_HEADS // 2, 256, 128), BF16),
                        pltpu.VMEM((N_HEADS // LANE_HEADS, AC, LANE_HEADS * 128), F32),
                        pltpu.VMEM((N_HEADS // LANE_HEADS, HEAD_DIM + DEN_ROWS, LANE_HEADS * 128), F32),
                        pltpu.VMEM((8, 128), I32)],
        compiler_params=_params("arbitrary", "arbitrary"),
        name="mix",
    )(u, vn, q, qi, wit, kb, vt, ki2, ws, bsp)


def _page_copies(pt_ref, row, n_pages, slot, streams, *, for_wait):
    cps = []
    for p in range(n_pages):
        page = 0 if for_wait else pt_ref[row * n_pages + p]
        for hbm, buf, sem, place in streams:
            cps.append(pltpu.make_async_copy(hbm.at[page], buf.at[(slot,) + place(p)], sem.at[slot]))
    return cps


def _gather_pages(pt_ref, n_pages, streams, n_slots):
    b = pl.program_id(0)
    ahead = n_slots - 1

    @pl.when(b == 0)
    def _():
        for r in range(ahead):
            for c in _page_copies(pt_ref, r, n_pages, r, streams, for_wait=False):
                c.start()

    @pl.when(b + ahead < pl.num_programs(0))
    def _():
        for c in _page_copies(pt_ref, b + ahead, n_pages, (b + ahead) % n_slots, streams, for_wait=False):
            c.start()

    slot = b % n_slots
    for c in _page_copies(pt_ref, b, n_pages, slot, streams, for_wait=True):
        c.wait()
    return slot


def _s_scores_kernel(pt_ref, qi_ref, wi_ref, kin_ref, kidx_hbm, sc_ref, self_ref, kbuf, sem, *, n_pages):
    place = lambda p: (slice(None), pl.ds(p * PAGE_SIZE, PAGE_SIZE))
    slot = _gather_pages(pt_ref, n_pages, [(kidx_hbm, kbuf, sem, place)], S_SCORES_SLOTS)
    qi = qi_ref[...]
    w = wi_ref[...] * (IDX_DIM ** -0.5)
    kcat = kbuf[slot].astype(BF16)
    s = _dot(qi, kcat)
    sc_ref[...] = jnp.sum(jnp.maximum(s, 0.0) * w, axis=0, keepdims=True)
    kin = kin_ref[...].astype(BF16).astype(F32)
    s_self = jnp.sum(qi.astype(F32) * kin, axis=1, keepdims=True)
    v_self = jnp.sum(jnp.maximum(s_self, 0.0) * w, axis=0, keepdims=True)
    self_ref[...] = jnp.broadcast_to(v_self, self_ref.shape)


def _s_scores(page_table_flat, qi3, wi_col, ki_new3, cache_kidx_l, *, n_pages):
    bd = qi3.shape[0]
    assert bd >= S_SCORES_SLOTS
    past = n_pages * PAGE_SIZE
    grid_spec = pltpu.PrefetchScalarGridSpec(
        num_scalar_prefetch=1,
        grid=(bd,),
        in_specs=[pl.BlockSpec((None, IDX_HEADS, IDX_DIM), lambda b, pt: (b, 0, 0)),
                  pl.BlockSpec((None, IDX_HEADS, 1), lambda b, pt: (b, 0, 0)),
                  pl.BlockSpec((None, 1, IDX_DIM), lambda b, pt: (b, 0, 0)),
                  pl.BlockSpec(memory_space=pl.ANY)],
        out_specs=(pl.BlockSpec((None, 1, past), lambda b, pt: (b, 0, 0)),
                   pl.BlockSpec((None, 1, 128), lambda b, pt: (b, 0, 0))),
        scratch_shapes=[pltpu.VMEM((S_SCORES_SLOTS, IDX_DIM, past), F32),
                        pltpu.SemaphoreType.DMA((S_SCORES_SLOTS,))],
    )
    return pl.pallas_call(
        functools.partial(_s_scores_kernel, n_pages=n_pages),
        out_shape=(jax.ShapeDtypeStruct((bd, 1, past), F32),
                   jax.ShapeDtypeStruct((bd, 1, 128), F32)),
        grid_spec=grid_spec,
        compiler_params=_params("arbitrary"),
        name="s_scores",
    )(page_table_flat, qi3, wi_col, ki_new3, cache_kidx_l)


def _s_select_kernel(sc_ref, self_ref, u_ref, vn_ref, ws0_ref, bs0_ref,
                     sel_ref, selself_ref, a_ref, *, n_sel):
    keys = sc_ref[...]
    kself = self_ref[:, 0:1]
    bd, past = keys.shape

    def count_ge(cand):
        c = jnp.sum(jnp.where(keys >= cand, 1, 0).astype(I32), axis=1, keepdims=True)
        return c + jnp.where(kself >= cand, 1, 0).astype(I32)

    def bit_body(i, t):
        cand = t ^ lax.shift_left(jnp.int32(1), 31 - i)
        return jnp.where(count_ge(_key_to_float(cand)) >= n_sel, cand, t)

    thr = _key_to_float(lax.fori_loop(0, 32, bit_body, jnp.full((bd, 1), INT_MIN, I32)))
    c_gt = (jnp.sum(jnp.where(keys > thr, 1, 0).astype(I32), axis=1, keepdims=True)
            + jnp.where(kself > thr, 1, 0).astype(I32))
    need = (n_sel - c_gt).astype(F32)
    r = lax.broadcasted_iota(I32, (128, 128), 0)
    c = lax.broadcasted_iota(I32, (128, 128), 1)
    ustrict = jnp.where(r < c, 1.0, 0.0).astype(BF16)
    before = jnp.zeros((bd, 1), F32)
    for t in range(past // 128):
        kt = keys[:, t * 128:(t + 1) * 128]
        eq = kt == thr
        eqf = jnp.where(eq, 1.0, 0.0).astype(F32)
        rank = before + _dot(eqf.astype(BF16), ustrict)
        sel = jnp.logical_or(kt > thr, jnp.logical_and(eq, rank < need))
        sel_ref[:, t * 128:(t + 1) * 128] = jnp.where(sel, 1.0, 0.0).astype(F32)
        before = before + jnp.sum(eqf, axis=1, keepdims=True)
    sel_self = jnp.logical_or(kself > thr, jnp.logical_and(kself == thr, before < need))
    selself_ref[...] = jnp.broadcast_to(jnp.where(sel_self, 1.0, 0.0).astype(F32), selself_ref.shape)
    sp = ws0_ref[...] * vn_ref[...] + bs0_ref[...]
    a_ref[...] = (u_ref[...] * sp).astype(a_ref.dtype)


def _s_select(sc, sc_self, u, vn, ws0, bs0, *, n_sel):
    bd, past = sc.shape
    full = lambda a: pl.BlockSpec(a.shape, lambda i: (0,) * a.ndim)
    args = (sc, sc_self, u, vn, ws0, bs0)
    return pl.pallas_call(
        functools.partial(_s_select_kernel, n_sel=n_sel),
        out_shape=(jax.ShapeDtypeStruct((bd, past), F32),
                   jax.ShapeDtypeStruct((bd, 128), F32),
                   jax.ShapeDtypeStruct((bd, A_WIDTH), BF16)),
        grid=(1,),
        in_specs=[full(a) for a in args],
        out_specs=(pl.BlockSpec((bd, past), lambda i: (0, 0)),
                   pl.BlockSpec((bd, 128), lambda i: (0, 0)),
                   pl.BlockSpec((bd, A_WIDTH), lambda i: (0, 0))),
        compiler_params=_params("arbitrary"),
        name="s_select",
    )(*args)


def _s_attn_kernel(pt_ref, q_ref, sel_ref, selself_ref, knew_ref, vnew_ref, ck_hbm, cv_hbm, o_ref,
                   kbuf, vbuf, ksem, vsem, *, n_pages, past):
    place = lambda p: (pl.ds(p * 2 * PAGE_SIZE, 2 * PAGE_SIZE), slice(None))
    slot = _gather_pages(pt_ref, n_pages, [(ck_hbm, kbuf, ksem, place), (cv_hbm, vbuf, vsem, place)], S_ATTN_SLOTS)
    q = q_ref[...]
    hrow = lax.broadcasted_iota(I32, (N_HEADS, 2 * past), 0)
    ccol = lax.broadcasted_iota(I32, (N_HEADS, 2 * past), 1)
    own_kv = (ccol & 1) == (hrow // GQA_GROUP)
    hcol = lax.broadcasted_iota(I32, (N_HEADS, 1), 0)
    slope = jnp.zeros((N_HEADS, 1), F32)
    for h in range(N_HEADS):
        slope = jnp.where(hcol == h, ALIBI_SLOPES[h], slope)

    lg = _dot_nt(q, kbuf[slot].astype(BF16)) * ATT_SCALE
    lg = lg - slope * (past - (ccol >> 1)).astype(F32)
    lg = jnp.where(jnp.logical_and(own_kv, sel_ref[...] > 0.5), lg, -jnp.inf)
    first_group = lax.broadcasted_iota(I32, (N_HEADS, HEAD_DIM), 0) < GQA_GROUP
    knew = knew_ref[...].astype(BF16).astype(F32)
    vnew = vnew_ref[...].astype(BF16).astype(F32)
    knew8 = jnp.where(first_group, knew[0:1, :], knew[1:2, :])
    vnew8 = jnp.where(first_group, vnew[0:1, :], vnew[1:2, :])
    lg_self = jnp.sum(q.astype(F32) * knew8, axis=1, keepdims=True) * ATT_SCALE
    lg_self = jnp.where(selself_ref[:, 0:1] > 0.5, lg_self, -jnp.inf)

    m = jnp.maximum(jnp.maximum(lg_self, NEG_BIG), jnp.max(lg, axis=1, keepdims=True))
    p_self = jnp.exp(lg_self - m)
    pp = jnp.exp(lg - m)
    l = p_self + jnp.sum(pp, axis=1, keepdims=True)
    acc = p_self.astype(BF16).astype(F32) * vnew8 + _dot(pp.astype(BF16), vbuf[slot].astype(BF16))
    o_ref[...] = (acc / l).astype(o_ref.dtype)


def _s_attn(page_table_flat, q3, sel2, sel_self3, k_new3, v_new3, cache_k2, cache_v2, *, n_pages, past):
    bd = q3.shape[0]
    assert bd >= S_ATTN_SLOTS
    per_b = lambda s1, s2: pl.BlockSpec((None, s1, s2), lambda b, pt: (b, 0, 0))
    grid_spec = pltpu.PrefetchScalarGridSpec(
        num_scalar_prefetch=1,
        grid=(bd,),
        in_specs=[per_b(N_HEADS, HEAD_DIM), per_b(1, 2 * past), per_b(1, 128),
                  per_b(N_KV_HEADS, HEAD_DIM), per_b(N_KV_HEADS, HEAD_DIM),
                  pl.BlockSpec(memory_space=pl.ANY), pl.BlockSpec(memory_space=pl.ANY)],
        out_specs=per_b(N_HEADS, HEAD_DIM),
        scratch_shapes=[pltpu.VMEM((S_ATTN_SLOTS, 2 * past, HEAD_DIM), F32),
                        pltpu.VMEM((S_ATTN_SLOTS, 2 * past, HEAD_DIM), F32),
                        pltpu.SemaphoreType.DMA((S_ATTN_SLOTS,)), pltpu.SemaphoreType.DMA((S_ATTN_SLOTS,))],
    )
    return pl.pallas_call(
        functools.partial(_s_attn_kernel, n_pages=n_pages, past=past),
        out_shape=jax.ShapeDtypeStruct((bd, N_HEADS, HEAD_DIM), BF16),
        grid_spec=grid_spec,
        compiler_params=_params("arbitrary"),
        name="s_attn",
    )(page_table_flat, q3, sel2, sel_self3, k_new3, v_new3, cache_k2, cache_v2)


def _outln_kernel(mix_ref, x_ref, gate_ref, w_ref, g_ref, b_ref, o_ref, *wb_ref, alpha):
    w = w_ref[...]
    if wb_ref:
        w = w.astype(BF16)
        wb_ref[0][...] = w
    tm = o_ref.shape[0]
    slab = OUTLN_SLAB if tm % OUTLN_SLAB == 0 else tm
    gate = gate_ref[...]
    for lo in range(0, tm, slab):
        rows = slice(lo, lo + slab)
        mix = _dot(mix_ref[rows, :], w)
        y = alpha * x_ref[rows, :] + (1.0 + (gate if gate.shape[0] == 1 else gate[rows, :])) * mix
        o_ref[rows, :] = _layer_norm(y, g_ref[...], b_ref[...])


def _outln(mixin, x, gate, w, g, b, *, tm, alpha):
    n, d = x.shape
    nb, r, _ = gate.shape
    tiles_per_mod = (n // nb) // tm
    kin = mixin.shape[1]
    emit = w.dtype != BF16
    assert not emit or n == tm
    out_shape = [jax.ShapeDtypeStruct((n, d), F32)]
    out_specs = [pl.BlockSpec((tm, d), lambda i: (i, 0))]
    if emit:
        out_shape.append(jax.ShapeDtypeStruct((kin, d), BF16))
        out_specs.append(pl.BlockSpec((kin, d), lambda i: (0, 0)))
    res = pl.pallas_call(
        functools.partial(_outln_kernel, alpha=alpha),
        out_shape=out_shape,
        grid=(n // tm,),
        in_specs=[pl.BlockSpec((tm, kin), lambda i: (i, 0)),
                  pl.BlockSpec((tm, d), lambda i: (i, 0)),
                  pl.BlockSpec((None, r, d), lambda i: (i // tiles_per_mod, 0, 0)),
                  pl.BlockSpec((kin, d), lambda i: (0, 0), pipeline_mode=pl.Buffered(1)),
                  pl.BlockSpec((1, d), lambda i: (0, 0)),
                  pl.BlockSpec((1, d), lambda i: (0, 0))],
        out_specs=out_specs,
        compiler_params=_params("arbitrary"),
        name="outln",
    )(mixin, x, gate, w, g, b)
    return res if emit else res[0]


def _ffn_kernel(x_ref, shift_ref, scale_ref, gate_ref, w1_ref, b1_ref, w2_ref, b2_ref, g_ref, b_ref,
                o_ref, *rest, alpha):
    *wb_refs, h_scr = rest
    f = pl.program_id(1)

    @pl.when(f == 0)
    def _():
        h_scr[...] = (x_ref[...] * (1.0 + scale_ref[...]) + shift_ref[...]).astype(BF16)
        o_ref[...] = jnp.zeros_like(o_ref)

    if wb_refs:
        wb_refs[0][...] = w1_ref[...].astype(BF16)
        wb_refs[1][...] = w2_ref[...].astype(BF16)
        w1_ref, w2_ref = wb_refs
    a = jnp.maximum(_dot(h_scr[...], w1_ref[...]) + b1_ref[...], 0.0)
    a2 = (a * a).astype(BF16)
    for c in range(o_ref.shape[1] // FFN_TN):
        cs = slice(c * FFN_TN, (c + 1) * FFN_TN)
        o_ref[:, cs] += _dot(a2, w2_ref[:, cs])

    @pl.when(f == pl.num_programs(1) - 1)
    def _():
        y = alpha * x_ref[...] + (1.0 + gate_ref[...]) * (o_ref[...] + b2_ref[...])
        o_ref[...] = _layer_norm(y, g_ref[...], b_ref[...])


def _ffn(x, shift, scale, gate, w1, b1, w2, b2, g, b, *, tm, tf, alpha):
    n, d = x.shape
    nb, r, _ = gate.shape
    dff = w1.shape[1]
    tiles_per_mod = (n // nb) // tm
    mod_spec = pl.BlockSpec((None, r, d), lambda i, f: (i // tiles_per_mod, 0, 0))
    emit = w1.dtype != BF16
    assert not emit or n == tm
    out_shape = [jax.ShapeDtypeStruct((n, d), F32)]
    out_specs = [pl.BlockSpec((tm, d), lambda i, f: (i, 0))]
    if emit:
        out_shape += [jax.ShapeDtypeStruct((d, dff), BF16), jax.ShapeDtypeStruct((dff, d), BF16)]
        out_specs += [pl.BlockSpec((d, tf), lambda i, f: (0, f)), pl.BlockSpec((tf, d), lambda i, f: (f, 0))]
    res = pl.pallas_call(
        functools.partial(_ffn_kernel, alpha=alpha),
        out_shape=out_shape,
        grid=(n // tm, dff // tf),
        in_specs=[pl.BlockSpec((tm, d), lambda i, f: (i, 0), pipeline_mode=pl.Buffered(1)),
                  mod_spec, mod_spec, mod_spec,
                  pl.BlockSpec((d, tf), lambda i, f: (0, f)),
                  pl.BlockSpec((1, tf), lambda i, f: (0, f)),
                  pl.BlockSpec((tf, d), lambda i, f: (f, 0)),
                  pl.BlockSpec((1, d), lambda i, f: (0, 0)),
                  pl.BlockSpec((1, d), lambda i, f: (0, 0)),
                  pl.BlockSpec((1, d), lambda i, f: (0, 0))],
        out_specs=out_specs,
        scratch_shapes=[pltpu.VMEM((tm, d), BF16)],
        compiler_params=_params("arbitrary", "arbitrary"),
        name="ffn",
    )(x, shift, scale, gate, w1, b1, w2, b2, g, b)
    return res if emit else res[0]


def kernel(x_prompt, x_sample, cache_k, cache_v, cache_kidx, page_table, c_prompt, c_sample,
           w_cond, b_cond, w_in, ln_v_g, ln_v_b, w_spatial, b_spatial, w_out,
           ln1_g, ln1_b, w_ff1, b_ff1, w_ff2, b_ff2, ln2_g, ln2_b):
    batch, seq, d = x_prompt.shape
    bd, ts, _ = x_sample.shape
    depth = w_in.shape[0]
    n_pages = page_table.shape[1]
    past = n_pages * PAGE_SIZE
    n_pool = cache_k.shape[1]
    dff = w_ff1.shape[2]
    assert ts == 1 and seq % KC == 0 and d == A_WIDTH + ATT_WIDTH
    assert w_in.shape[2] == MAIN_WIDTH + IDX_DIM + IDX_HEADS
    n = batch * seq
    alpha = (2 * depth) ** 0.25
    n_sel_p = min(TOPK_MAX, seq // 4)
    n_sel_s = min(TOPK_MAX, (past + ts) // 4)
    tm_p = min(1024, seq)
    tm_o = min(512, seq)
    tf = min(1024, dff)
    pt_flat = page_table.reshape(-1).astype(I32)

    xp = x_prompt.reshape(n, d)
    xs = x_sample.reshape(bd, d)
    c_all = jnp.concatenate([c_prompt, c_sample], axis=0)

    outs = {k: [] for k in ("kp", "vp", "kip", "ks", "vs", "kis", "vc")}
    for l in range(depth):
        wm = jnp.swapaxes(w_in[l], 0, 1).astype(BF16)
        wt = jnp.pad(wm[MAIN_WIDTH:], ((0, 2 * IDX_DIM - IDX_DIM - IDX_HEADS), (0, 0)))
        kidx_t = jnp.swapaxes(cache_kidx[l], 1, 2)
        lng = ln_v_g[l].reshape(1, A_WIDTH)
        lnb = ln_v_b[l].reshape(1, A_WIDTH)
        tril = jnp.tril(jnp.ones((CHUNK, CHUNK), dtype=bool))
        ws = jnp.where(tril[None], w_spatial[l], 0.0).astype(BF16)
        bsp = jnp.repeat(jnp.transpose(b_spatial[l]), A_CH, axis=1)
        ws0 = jnp.repeat(w_spatial[l][:, 0, 0], A_CH).reshape(1, A_WIDTH)
        bs0 = jnp.repeat(b_spatial[l][:, 0], A_CH).reshape(1, A_WIDTH)
        b1 = b_ff1[l].reshape(1, dff)
        b2 = b_ff2[l].reshape(1, d)
        g1, be1 = ln1_g[l].reshape(1, d), ln1_b[l].reshape(1, d)
        g2, be2 = ln2_g[l].reshape(1, d), ln2_b[l].reshape(1, d)

        z = _cond(c_all, w_cond[l], b_cond[l])
        mods = [z[:, i * d:(i + 1) * d] for i in range(N_MOD)]
        mp = [m[:batch].reshape(batch, 1, d) for m in mods]
        ms = [m[batch:].reshape(1, bd, d) for m in mods]

        (u, vn, q, k, v, _, _, qi, ki, _, wit) = _proj(
            xs, ms[0], ms[1], wm, wt, lng, lnb, tm=bd, vn_dtype=F32)
        sc, sc_self = _s_scores(pt_flat, qi.reshape(bd, IDX_HEADS, IDX_DIM),
                                jnp.transpose(wit).reshape(bd, IDX_HEADS, 1),
                                ki.reshape(bd, 1, IDX_DIM), kidx_t, n_pages=n_pages)
        sel, sel_self, a_out = _s_select(sc.reshape(bd, past), sc_self.reshape(bd, 128), u, vn, ws0, bs0,
                                         n_sel=n_sel_s)
        sel2 = jnp.repeat(sel, 2, axis=1).reshape(bd, 1, 2 * past)
        b_out = _s_attn(pt_flat, q.reshape(bd, N_HEADS, HEAD_DIM), sel2, sel_self.reshape(bd, 1, 128),
                        k.reshape(bd, N_KV_HEADS, HEAD_DIM), v.reshape(bd, N_KV_HEADS, HEAD_DIM),
                        cache_k[l].reshape(n_pool, 2 * PAGE_SIZE, HEAD_DIM),
                        cache_v[l].reshape(n_pool, 2 * PAGE_SIZE, HEAD_DIM),
                        n_pages=n_pages, past=past)
        mixin = jnp.concatenate([a_out, b_out.reshape(bd, ATT_WIDTH)], axis=1)
        x1, w_out_b = _outln(mixin, xs, ms[2], w_out[l], g1, be1, tm=bd, alpha=alpha)
        xs, w1_b, w2_b = _ffn(x1, ms[3], ms[4], ms[5], w_ff1[l], b1, w_ff2[l], b2, g2, be2, tm=bd, tf=tf, alpha=alpha)
        outs["ks"].append(k.reshape(bd, ts, N_KV_HEADS, HEAD_DIM))
        outs["vs"].append(v.reshape(bd, ts, N_KV_HEADS, HEAD_DIM))
        outs["kis"].append(ki.reshape(bd, ts, IDX_DIM))
        outs["vc"].append(vn.reshape(bd, ts, A_GROUPS, A_CH))

        (u, vn, q, k, v, kb, vt, qi, ki, ki2, wit) = _proj(
            xp, mp[0], mp[1], wm, wt, lng, lnb, tm=tm_p, vn_dtype=BF16)
        mixin = _mix(u, vn, q, qi, wit, kb, vt, ki2, ws, bsp, batch=batch, seq=seq, n_sel=n_sel_p)
        x1 = _outln(mixin, xp, mp[2], w_out_b, g1, be1, tm=tm_o, alpha=alpha)
        xp = _ffn(x1, mp[3], mp[4], mp[5], w1_b, b1, w2_b, b2, g2, be2, tm=tm_p, tf=tf, alpha=alpha)
        outs["kp"].append(k.reshape(batch, seq, N_KV_HEADS, HEAD_DIM))
        outs["vp"].append(v.reshape(batch, seq, N_KV_HEADS, HEAD_DIM))
        outs["kip"].append(ki.reshape(batch, seq, IDX_DIM))

    st = lambda name: jnp.stack(outs[name])
    return (xp.reshape(batch, seq, d), xs.reshape(bd, ts, d),
            st("kp"), st("vp"), st("kip"), st("ks"), st("vs"), st("kis"), st("vc"))
```

```python
import functools

import jax
import jax.numpy as jnp
import numpy as np
from jax import lax
from jax.experimental import pallas as pl
from jax.experimental.pallas import tpu as pltpu

F32 = jnp.float32
BF16 = jnp.bfloat16
I32 = jnp.int32

CHUNK = 128
A_GROUPS = 8
A_CH = 128
A_WIDTH = A_GROUPS * A_CH
HEAD_DIM = 128
N_HEADS = 8
N_KV_HEADS = 2
GQA_GROUP = N_HEADS // N_KV_HEADS
ATT_WIDTH = N_HEADS * HEAD_DIM
KV_WIDTH = N_KV_HEADS * HEAD_DIM
IDX_HEADS = 16
IDX_DIM = 64
IDX_WIDTH = IDX_HEADS * IDX_DIM
TOPK_MAX = 256
Q_BLOCK = 128
PAGE_SIZE = 128
N_MOD = 6
LN_EPS = 1e-5
ATT_SCALE = HEAD_DIM ** -0.5
MAIN_WIDTH = 2 * A_WIDTH + ATT_WIDTH + 2 * KV_WIDTH + IDX_WIDTH
PROJ_TN = 512
FFN_TN = 512
OUTLN_SLAB = 256
ALIBI_SLOPES = tuple(float(2.0 ** (-8.0 * h / N_HEADS)) for h in range(1, N_HEADS + 1))

VMEM_LIMIT_BYTES = 58 * 1024 * 1024
INT_MIN = -(2 ** 31)
NEG_BIG = -1e30
FLT_MAX = float(np.finfo(np.float32).max)
LOG2E = float(np.log2(np.e))
KC = 512
AC = 512
S_SCORES_SLOTS = 4
S_ATTN_SLOTS = 3
LANE_HEADS = 4
DEN_ROWS = 16

NT_DIMS = (((1,), (1,)), ((), ()))


def _dot(a, b):
    return jnp.dot(a, b, preferred_element_type=F32)


def _dot_nt(a, b):
    return lax.dot_general(a, b, NT_DIMS, preferred_element_type=F32)


def _layer_norm(x, g, b):
    mu = jnp.mean(x, axis=-1, keepdims=True)
    xc = x - mu
    var = jnp.mean(xc * xc, axis=-1, keepdims=True)
    return xc * lax.rsqrt(var + LN_EPS) * g + b


def _key_to_float(key):
    bits = jnp.where(key < 0, key ^ jnp.int32(0x7FFFFFFF), key)
    return pltpu.bitcast(bits, F32)


def _params(*sem):
    return pltpu.CompilerParams(dimension_semantics=sem, vmem_limit_bytes=VMEM_LIMIT_BYTES)


def _cond_kernel(c_ref, w_ref, b_ref, o_ref):
    c = c_ref[...]
    a = (c * jax.nn.sigmoid(c)).astype(BF16)
    o_ref[...] = _dot(a, w_ref[...].astype(BF16)) + b_ref[...]


def _cond(c, w, b):
    m, d = c.shape
    n = w.shape[1]
    tn = 1024
    return pl.pallas_call(
        _cond_kernel,
        out_shape=jax.ShapeDtypeStruct((m, n), F32),
        grid=(n // tn,),
        in_specs=[pl.BlockSpec((m, d), lambda j: (0, 0)),
                  pl.BlockSpec((d, tn), lambda j: (0, j)),
                  pl.BlockSpec((1, tn), lambda j: (0, j))],
        out_specs=pl.BlockSpec((m, tn), lambda j: (0, j)),
        compiler_params=_params("arbitrary"),
        name="cond",
    )(c, w, b.reshape(1, n))


_J_V, _J_Q, _J_KV, _J_QI, _J_END = 2, 4, 6, 7, 9


def _proj_kernel(x_ref, shift_ref, scale_ref, wm_ref, wt_ref, lng_ref, lnb_ref,
                 u_ref, vn_ref, q_ref, k_ref, v_ref, kb_ref, vt_ref, qi_ref, ki_ref, ki2_ref, wit_ref,
                 h_scr):
    j = pl.program_id(1)

    @pl.when(j == 0)
    def _():
        h = (x_ref[...] * (1.0 + scale_ref[...]) + shift_ref[...]).astype(BF16)
        h_scr[...] = h
        tail = _dot_nt(h, wt_ref[...])
        ki_ref[...] = tail[:, :IDX_DIM]
        lane = lax.broadcasted_iota(I32, tail.shape, 1)
        ki2_ref[...] = jnp.where(lane < IDX_DIM, tail, pltpu.roll(tail, IDX_DIM, axis=1)).astype(BF16)
        wit_ref[...] = tail.T[IDX_DIM:IDX_DIM + IDX_HEADS, :] * (IDX_HEADS ** -0.5)

    def tile():
        return _dot_nt(h_scr[...], wm_ref[...])

    @pl.when(j < _J_V)
    def _():
        u_ref[...] = tile()

    @pl.when(jnp.logical_and(j >= _J_V, j < _J_Q))
    def _():
        z = tile()
        for g in range(PROJ_TN // A_CH):
            sl = slice(g * A_CH, (g + 1) * A_CH)
            vn_ref[:, sl] = _layer_norm(z[:, sl], lng_ref[:, sl], lnb_ref[:, sl]).astype(vn_ref.dtype)

    @pl.when(jnp.logical_and(j >= _J_Q, j < _J_KV))
    def _():
        q_ref[...] = tile().astype(BF16)

    @pl.when(j == _J_KV)
    def _():
        z = tile()
        for hh in range(N_KV_HEADS):
            k_ref[:, hh, :] = z[:, hh * HEAD_DIM:(hh + 1) * HEAD_DIM]
            v_ref[:, hh, :] = z[:, KV_WIDTH + hh * HEAD_DIM:KV_WIDTH + (hh + 1) * HEAD_DIM]
        kb_ref[...] = z[:, :KV_WIDTH].astype(BF16)
        vt = z[:, KV_WIDTH:].T.astype(BF16)
        for c in range(vt_ref.shape[0]):
            vt_ref[c] = vt[:, c * 128:(c + 1) * 128]

    @pl.when(j >= _J_QI)
    def _():
        qi_ref[...] = tile().astype(BF16)


def _proj(x, shift, scale, wm, wt, lng, lnb, *, tm, vn_dtype):
    n, d = x.shape
    nb, r, _ = shift.shape
    rows_per_mod = n // nb
    assert n % tm == 0 and rows_per_mod % tm == 0 and r in (1, tm)
    tiles_per_mod = rows_per_mod // tm
    tn = PROJ_TN

    def clipj(lo, cnt):
        return lambda i, j: (i, jnp.clip(j - lo, 0, cnt - 1))

    mod_spec = pl.BlockSpec((None, r, d), lambda i, j: (i // tiles_per_mod, 0, 0))
    out_shape = (
        jax.ShapeDtypeStruct((n, A_WIDTH), F32),
        jax.ShapeDtypeStruct((n, A_WIDTH), vn_dtype),
        jax.ShapeDtypeStruct((n, ATT_WIDTH), BF16),
        jax.ShapeDtypeStruct((n, N_KV_HEADS, HEAD_DIM), F32),
        jax.ShapeDtypeStruct((n, N_KV_HEADS, HEAD_DIM), F32),
        jax.ShapeDtypeStruct((n, KV_WIDTH), BF16),
        jax.ShapeDtypeStruct((n // 128, KV_WIDTH, 128), BF16),
        jax.ShapeDtypeStruct((n, IDX_WIDTH), BF16),
        jax.ShapeDtypeStruct((n, IDX_DIM), F32),
        jax.ShapeDtypeStruct((n, 2 * IDX_DIM), BF16),
        jax.ShapeDtypeStruct((IDX_HEADS, n), F32),
    )
    out_specs = (
        pl.BlockSpec((tm, tn), clipj(0, 2)),
        pl.BlockSpec((tm, tn), clipj(_J_V, 2)),
        pl.BlockSpec((tm, tn), clipj(_J_Q, 2)),
        pl.BlockSpec((tm, N_KV_HEADS, HEAD_DIM), lambda i, j: (i, 0, 0)),
        pl.BlockSpec((tm, N_KV_HEADS, HEAD_DIM), lambda i, j: (i, 0, 0)),
        pl.BlockSpec((tm, KV_WIDTH), lambda i, j: (i, 0)),
        pl.BlockSpec((tm // 128, KV_WIDTH, 128), lambda i, j: (i, 0, 0)),
        pl.BlockSpec((tm, tn), clipj(_J_QI, 2)),
        pl.BlockSpec((tm, IDX_DIM), lambda i, j: (i, 0)),
        pl.BlockSpec((tm, 2 * IDX_DIM), lambda i, j: (i, 0)),
        pl.BlockSpec((IDX_HEADS, tm), lambda i, j: (0, i)),
    )
    in_specs = [
        pl.BlockSpec((tm, d), lambda i, j: (i, 0)),
        mod_spec, mod_spec,
        pl.BlockSpec((tn, d), lambda i, j: (j, 0)),
        pl.BlockSpec((2 * IDX_DIM, d), lambda i, j: (0, 0)),
        pl.BlockSpec((1, tn), lambda i, j: (0, jnp.clip(j - _J_V, 0, 1))),
        pl.BlockSpec((1, tn), lambda i, j: (0, jnp.clip(j - _J_V, 0, 1))),
    ]
    return pl.pallas_call(
        _proj_kernel,
        out_shape=out_shape,
        grid=(n // tm, _J_END),
        in_specs=in_specs,
        out_specs=out_specs,
        scratch_shapes=[pltpu.VMEM((tm, d), BF16)],
        compiler_params=_params("arbitrary", "arbitrary"),
        name="proj",
    )(x, shift, scale, wm, wt, lng, lnb)


def _mix_kernel(u_ref, vn_ref, q_ref, qi_ref, wit_ref, k_ref, vt_ref, ki2_ref, ws_ref, bsp_ref,
                o_ref, sc_scr, msk_scr, qim_scr, ab_scr, acc_scr, thr_scr, *, n_sel):
    jq = pl.program_id(1)
    n_kc = jq // (KC // 128) + 1
    n_ac = jq // (AC // 128) + 1
    row = lax.broadcasted_iota(I32, (128, 128), 0)
    col = lax.broadcasted_iota(I32, (128, 128), 1)

    @pl.when(jnp.logical_and(pl.program_id(0) == 0, jq == 0))
    def _():
        d0 = (lax.broadcasted_iota(I32, (AC, 128), 1) - lax.broadcasted_iota(I32, (AC, 128), 0)).astype(F32)
        for h in range(N_HEADS):
            lanes = slice((h % LANE_HEADS) * 128, (h % LANE_HEADS + 1) * 128)
            ab_scr[h // LANE_HEADS, :, lanes] = (ALIBI_SLOPES[h] * LOG2E) * d0

    for p in range(IDX_HEADS // 2):
        pair = qi_ref[:, p * 128:(p + 1) * 128]
        zero = jnp.zeros_like(pair)
        qim_scr[p, 0:128, :] = jnp.where(col < IDX_DIM, pair, zero)
        qim_scr[p, 128:256, :] = jnp.where(col >= IDX_DIM, pair, zero)
    wis = wit_ref[...] * (IDX_DIM ** -0.5)
    key_minus_query = (lax.broadcasted_iota(I32, (KC, 128), 0) - lax.broadcasted_iota(I32, (KC, 128), 1))

    def idx_body(kc, carry):
        base = kc * KC
        kk = ki2_ref[pl.ds(base, KC), :]
        acc = jnp.zeros((KC, 128), F32)
        for p in range(IDX_HEADS // 2):
            s = _dot_nt(kk, qim_scr[p])
            acc = (acc + jnp.maximum(s[:, :128], 0.0) * wis[2 * p:2 * p + 1, :]
                   + jnp.maximum(s[:, 128:], 0.0) * wis[2 * p + 1:2 * p + 2, :])
        inadmissible = key_minus_query > (jq * 128 - kc * KC)
        sc_scr[pl.ds(base, KC), :] = jnp.where(inadmissible, -jnp.inf, acc)
        return carry

    def index_chunks(k):
        for kc in range(k):
            idx_body(kc, 0)

    for k in range(1, sc_scr.shape[0] // KC + 1):
        pl.when(n_kc == k)(functools.partial(index_chunks, k))

    def count(pred):
        def body(kc, c):
            x = sc_scr[pl.ds(pl.multiple_of(kc * KC, KC), KC), :]
            m = jnp.where(pred(x), 1, 0).astype(I32)
            return c + jnp.sum(m.reshape(KC // 8, 8, 128), axis=0)
        c = lax.fori_loop(0, n_kc, body, jnp.zeros((8, 128), I32))
        return jnp.sum(c, axis=0, keepdims=True)

    def search(k):
        def bit_body(i, t):
            cand = t ^ lax.shift_left(jnp.int32(1), 31 - i)
            cand_f = _key_to_float(cand)
            c = jnp.zeros((8, 128), I32)
            for kc in range(k):
                m = jnp.where(sc_scr[kc * KC:(kc + 1) * KC, :] >= cand_f, 1, 0).astype(I32)
                c = c + jnp.sum(m.reshape(KC // 8, 8, 128), axis=0)
            return jnp.where(jnp.sum(c, axis=0, keepdims=True) >= n_sel, cand, t)

        t = lax.fori_loop(0, 32, bit_body, jnp.full((1, 128), INT_MIN, I32))
        thr_scr[...] = jnp.broadcast_to(t, thr_scr.shape)

    for k in range(1, sc_scr.shape[0] // KC + 1):
        pl.when(n_kc == k)(functools.partial(search, k))
    thr_key = thr_scr[0:1, :]
    thr = jnp.where(thr_key == jnp.int32(INT_MIN), -FLT_MAX, _key_to_float(thr_key))
    c_ge = count(lambda x: x >= thr)
    c_gt = count(lambda x: x > thr)
    has_tie_overflow = jnp.max(c_ge) > n_sel

    @pl.when(jnp.logical_not(has_tie_overflow))
    def _():
        def body(kc, carry):
            sl = pl.ds(pl.multiple_of(kc * KC, KC), KC)
            msk_scr[sl, :] = jnp.where(sc_scr[sl, :] >= thr, 0.0, -jnp.inf).astype(F32)
            return carry
        lax.fori_loop(0, n_kc, body, 0)

    @pl.when(has_tie_overflow)
    def _():
        need = (n_sel - c_gt).astype(F32)
        lstrict = jnp.where(col < row, 1.0, 0.0).astype(BF16)

        def body(kt, before):
            sl = pl.ds(pl.multiple_of(kt * 128, 128), 128)
            x = sc_scr[sl, :]
            eq = x == thr
            eqf = jnp.where(eq, 1.0, 0.0).astype(F32)
            rank = before + _dot(lstrict, eqf.astype(BF16))
            sel = jnp.logical_or(x > thr, jnp.logical_and(eq, rank < need))
            msk_scr[sl, :] = jnp.where(sel, 0.0, -jnp.inf).astype(F32)
            return before + jnp.sum(eqf, axis=0, keepdims=True)
        lax.fori_loop(0, n_kc * (KC // 128), body, jnp.zeros((1, 128), F32))

    gw = LANE_HEADS * 128
    n_lg = N_HEADS // LANE_HEADS
    lane_head = lax.broadcasted_iota(I32, (1, gw), 1) // 128
    q_grp, slope_vec = [], []
    for g in range(n_lg):
        heads = range(g * LANE_HEADS, (g + 1) * LANE_HEADS)
        q_grp.append(jnp.concatenate([q_ref[:, h * HEAD_DIM:(h + 1) * HEAD_DIM] for h in heads], axis=0))
        sv = jnp.zeros((1, gw), F32)
        for i, h in enumerate(heads):
            sv = jnp.where(lane_head == i, ALIBI_SLOPES[h] * LOG2E, sv)
        slope_vec.append(sv)
        acc_scr[g] = jnp.zeros((HEAD_DIM + DEN_ROWS, gw), F32)

    def att_body(ac, carry):
        base = ac * AC
        off = (jq * 128 - ac * AC).astype(F32)
        mk = msk_scr[pl.ds(base, AC), :]
        mkw = jnp.concatenate([mk] * LANE_HEADS, axis=1)
        kv_of = [g * LANE_HEADS // GQA_GROUP for g in range(n_lg)]
        raw = [_dot_nt(k_ref[pl.ds(base, AC), kv_of[g] * HEAD_DIM:(kv_of[g] + 1) * HEAD_DIM], q_grp[g])
               for g in range(n_lg)]
        new = []
        for g in range(n_lg):
            kv = kv_of[g]
            m = carry[g]
            x = raw[g] * (ATT_SCALE * LOG2E) - ab_scr[g] + mkw
            cvec = slope_vec[g] * off
            m_new = jnp.maximum(m, jnp.max(x, axis=0, keepdims=True) - cvec)
            alpha = jnp.exp2(m - m_new)
            p = jnp.exp2(x - (m_new + cvec)).astype(BF16)
            vt_c = jnp.concatenate(
                [jnp.concatenate([vt_ref[ac * (AC // 128) + t, kv * HEAD_DIM:(kv + 1) * HEAD_DIM, :]
                                  for t in range(AC // 128)], axis=1),
                 jnp.ones((DEN_ROWS, AC), BF16)], axis=0)
            acc_scr[g] = alpha * acc_scr[g] + _dot(vt_c, p)
            new.append(m_new)
        return tuple(new)

    def attend(k):
        carry = (jnp.full((1, gw), NEG_BIG, F32),) * n_lg
        for ac in range(k):
            carry = att_body(ac, carry)

    for k in range(1, sc_scr.shape[0] // AC + 1):
        pl.when(n_ac == k)(functools.partial(attend, k))
    for g in range(n_lg):
        out_t = acc_scr[g, :HEAD_DIM, :] / acc_scr[g, HEAD_DIM:HEAD_DIM + 1, :]
        for i in range(LANE_HEADS):
            lo = A_WIDTH + (g * LANE_HEADS + i) * HEAD_DIM
            o_ref[:, lo:lo + HEAD_DIM] = out_t[:, i * 128:(i + 1) * 128].T.astype(o_ref.dtype)

    for g in range(A_GROUPS):
        sl = slice(g * A_CH, (g + 1) * A_CH)
        sp = _dot(ws_ref[g], vn_ref[:, sl]) + bsp_ref[:, sl]
        o_ref[:, sl] = (u_ref[:, sl] * sp).astype(o_ref.dtype)


def _mix(u, vn, q, qi, wit, kb, vt, ki2, ws, bsp, *, batch, seq, n_sel):
    n = batch * seq
    nblk = seq // Q_BLOCK
    ntile = seq // 128
    row_spec = lambda w: pl.BlockSpec((Q_BLOCK, w), lambda b, j: (b * nblk + j, 0))
    in_specs = [
        row_spec(A_WIDTH), row_spec(A_WIDTH), row_spec(ATT_WIDTH), row_spec(IDX_WIDTH),
        pl.BlockSpec((IDX_HEADS, Q_BLOCK), lambda b, j: (0, b * nblk + j)),
        pl.BlockSpec((seq, KV_WIDTH), lambda b, j: (b, 0)),
        pl.BlockSpec((ntile, KV_WIDTH, 128), lambda b, j: (b, 0, 0)),
        pl.BlockSpec((seq, 2 * IDX_DIM), lambda b, j: (b, 0)),
        pl.BlockSpec((A_GROUPS, CHUNK, CHUNK), lambda b, j: (0, 0, 0)),
        pl.BlockSpec((CHUNK, A_WIDTH), lambda b, j: (0, 0)),
    ]
    return pl.pallas_call(
        functools.partial(_mix_kernel, n_sel=n_sel),
        out_shape=jax.ShapeDtypeStruct((n, A_WIDTH + ATT_WIDTH), BF16),
        grid=(batch, nblk),
        in_specs=in_specs,
        out_specs=pl.BlockSpec((Q_BLOCK, A_WIDTH + ATT_WIDTH), lambda b, j: (b * nblk + j, 0)),
        scratch_shapes=[pltpu.VMEM((seq, 128), F32),
                        pltpu.VMEM((seq, 128), F32),
                        pltpu.VMEM((IDX_HEADS // 2, 256, 128), BF16),
                        pltpu.VMEM((N_HEADS // LANE_HEADS, AC, LANE_HEADS * 128), F32),
                        pltpu.VMEM((N_HEADS // LANE_HEADS, HEAD_DIM + DEN_ROWS, LANE_HEADS * 128), F32),
                        pltpu.VMEM((8, 128), I32)],
        compiler_params=_params("arbitrary", "arbitrary"),
        name="mix",
    )(u, vn, q, qi, wit, kb, vt, ki2, ws, bsp)


def _page_copies(pt_ref, row, n_pages, slot, streams, *, for_wait):
    cps = []
    for p in range(n_pages):
        page = 0 if for_wait else pt_ref[row * n_pages + p]
        for hbm, buf, sem, place in streams:
            cps.append(pltpu.make_async_copy(hbm.at[page], buf.at[(slot,) + place(p)], sem.at[slot]))
    return cps


def _gather_pages(pt_ref, n_pages, streams, n_slots):
    b = pl.program_id(0)
    ahead = n_slots - 1

    @pl.when(b == 0)
    def _():
        for r in range(ahead):
            for c in _page_copies(pt_ref, r, n_pages, r, streams, for_wait=False):
                c.start()

    @pl.when(b + ahead < pl.num_programs(0))
    def _():
        for c in _page_copies(pt_ref, b + ahead, n_pages, (b + ahead) % n_slots, streams, for_wait=False):
            c.start()

    slot = b % n_slots
    for c in _page_copies(pt_ref, b, n_pages, slot, streams, for_wait=True):
        c.wait()
    return slot


def _s_scores_kernel(pt_ref, qi_ref, wi_ref, kin_ref, kidx_hbm, sc_ref, self_ref, kbuf, sem, *, n_pages):
    place = lambda p: (slice(None), pl.ds(p * PAGE_SIZE, PAGE_SIZE))
    slot = _gather_pages(pt_ref, n_pages, [(kidx_hbm, kbuf, sem, place)], S_SCORES_SLOTS)
    qi = qi_ref[...]
    w = wi_ref[...] * (IDX_DIM ** -0.5)
    kcat = kbuf[slot].astype(BF16)
    s = _dot(qi, kcat)
    sc_ref[...] = jnp.sum(jnp.maximum(s, 0.0) * w, axis=0, keepdims=True)
    kin = kin_ref[...].astype(BF16).astype(F32)
    s_self = jnp.sum(qi.astype(F32) * kin, axis=1, keepdims=True)
    v_self = jnp.sum(jnp.maximum(s_self, 0.0) * w, axis=0, keepdims=True)
    self_ref[...] = jnp.broadcast_to(v_self, self_ref.shape)


def _s_scores(page_table_flat, qi3, wi_col, ki_new3, cache_kidx_l, *, n_pages):
    bd = qi3.shape[0]
    assert bd >= S_SCORES_SLOTS
    past = n_pages * PAGE_SIZE
    grid_spec = pltpu.PrefetchScalarGridSpec(
        num_scalar_prefetch=1,
        grid=(bd,),
        in_specs=[pl.BlockSpec((None, IDX_HEADS, IDX_DIM), lambda b, pt: (b, 0, 0)),
                  pl.BlockSpec((None, IDX_HEADS, 1), lambda b, pt: (b, 0, 0)),
                  pl.BlockSpec((None, 1, IDX_DIM), lambda b, pt: (b, 0, 0)),
                  pl.BlockSpec(memory_space=pl.ANY)],
        out_specs=(pl.BlockSpec((None, 1, past), lambda b, pt: (b, 0, 0)),
                   pl.BlockSpec((None, 1, 128), lambda b, pt: (b, 0, 0))),
        scratch_shapes=[pltpu.VMEM((S_SCORES_SLOTS, IDX_DIM, past), F32),
                        pltpu.SemaphoreType.DMA((S_SCORES_SLOTS,))],
    )
    return pl.pallas_call(
        functools.partial(_s_scores_kernel, n_pages=n_pages),
        out_shape=(jax.ShapeDtypeStruct((bd, 1, past), F32),
                   jax.ShapeDtypeStruct((bd, 1, 128), F32)),
        grid_spec=grid_spec,
        compiler_params=_params("arbitrary"),
        name="s_scores",
    )(page_table_flat, qi3, wi_col, ki_new3, cache_kidx_l)


def _s_select_kernel(sc_ref, self_ref, u_ref, vn_ref, ws0_ref, bs0_ref,
                     sel_ref, selself_ref, a_ref, *, n_sel):
    keys = sc_ref[...]
    kself = self_ref[:, 0:1]
    bd, past = keys.shape

    def count_ge(cand):
        c = jnp.sum(jnp.where(keys >= cand, 1, 0).astype(I32), axis=1, keepdims=True)
        return c + jnp.where(kself >= cand, 1, 0).astype(I32)

    def bit_body(i, t):
        cand = t ^ lax.shift_left(jnp.int32(1), 31 - i)
        return jnp.where(count_ge(_key_to_float(cand)) >= n_sel, cand, t)

    thr = _key_to_float(lax.fori_loop(0, 32, bit_body, jnp.full((bd, 1), INT_MIN, I32)))
    c_gt = (jnp.sum(jnp.where(keys > thr, 1, 0).astype(I32), axis=1, keepdims=True)
            + jnp.where(kself > thr, 1, 0).astype(I32))
    need = (n_sel - c_gt).astype(F32)
    r = lax.broadcasted_iota(I32, (128, 128), 0)
    c = lax.broadcasted_iota(I32, (128, 128), 1)
    ustrict = jnp.where(r < c, 1.0, 0.0).astype(BF16)
    before = jnp.zeros((bd, 1), F32)
    for t in range(past // 128):
        kt = keys[:, t * 128:(t + 1) * 128]
        eq = kt == thr
        eqf = jnp.where(eq, 1.0, 0.0).astype(F32)
        rank = before + _dot(eqf.astype(BF16), ustrict)
        sel = jnp.logical_or(kt > thr, jnp.logical_and(eq, rank < need))
        sel_ref[:, t * 128:(t + 1) * 128] = jnp.where(sel, 1.0, 0.0).astype(F32)
        before = before + jnp.sum(eqf, axis=1, keepdims=True)
    sel_self = jnp.logical_or(kself > thr, jnp.logical_and(kself == thr, before < need))
    selself_ref[...] = jnp.broadcast_to(jnp.where(sel_self, 1.0, 0.0).astype(F32), selself_ref.shape)
    sp = ws0_ref[...] * vn_ref[...] + bs0_ref[...]
    a_ref[...] = (u_ref[...] * sp).astype(a_ref.dtype)


def _s_select(sc, sc_self, u, vn, ws0, bs0, *, n_sel):
    bd, past = sc.shape
    full = lambda a: pl.BlockSpec(a.shape, lambda i: (0,) * a.ndim)
    args = (sc, sc_self, u, vn, ws0, bs0)
    return pl.pallas_call(
        functools.partial(_s_select_kernel, n_sel=n_sel),
        out_shape=(jax.ShapeDtypeStruct((bd, past), F32),
                   jax.ShapeDtypeStruct((bd, 128), F32),
                   jax.ShapeDtypeStruct((bd, A_WIDTH), BF16)),
        grid=(1,),
        in_specs=[full(a) for a in args],
        out_specs=(pl.BlockSpec((bd, past), lambda i: (0, 0)),
                   pl.BlockSpec((bd, 128), lambda i: (0, 0)),
                   pl.BlockSpec((bd, A_WIDTH), lambda i: (0, 0))),
        compiler_params=_params("arbitrary"),
        name="s_select",
    )(*args)


def _s_attn_kernel(pt_ref, q_ref, sel_ref, selself_ref, knew_ref, vnew_ref, ck_hbm, cv_hbm, o_ref,
                   kbuf, vbuf, ksem, vsem, *, n_pages, past):
    place = lambda p: (pl.ds(p * 2 * PAGE_SIZE, 2 * PAGE_SIZE), slice(None))
    slot = _gather_pages(pt_ref, n_pages, [(ck_hbm, kbuf, ksem, place), (cv_hbm, vbuf, vsem, place)], S_ATTN_SLOTS)
    q = q_ref[...]
    hrow = lax.broadcasted_iota(I32, (N_HEADS, 2 * past), 0)
    ccol = lax.broadcasted_iota(I32, (N_HEADS, 2 * past), 1)
    own_kv = (ccol & 1) == (hrow // GQA_GROUP)
    hcol = lax.broadcasted_iota(I32, (N_HEADS, 1), 0)
    slope = jnp.zeros((N_HEADS, 1), F32)
    for h in range(N_HEADS):
        slope = jnp.where(hcol == h, ALIBI_SLOPES[h], slope)

    lg = _dot_nt(q, kbuf[slot].astype(BF16)) * ATT_SCALE
    lg = lg - slope * (past - (ccol >> 1)).astype(F32)
    lg = jnp.where(jnp.logical_and(own_kv, sel_ref[...] > 0.5), lg, -jnp.inf)
    first_group = lax.broadcasted_iota(I32, (N_HEADS, HEAD_DIM), 0) < GQA_GROUP
    knew = knew_ref[...].astype(BF16).astype(F32)
    vnew = vnew_ref[...].astype(BF16).astype(F32)
    knew8 = jnp.where(first_group, knew[0:1, :], knew[1:2, :])
    vnew8 = jnp.where(first_group, vnew[0:1, :], vnew[1:2, :])
    lg_self = jnp.sum(q.astype(F32) * knew8, axis=1, keepdims=True) * ATT_SCALE
    lg_self = jnp.where(selself_ref[:, 0:1] > 0.5, lg_self, -jnp.inf)

    m = jnp.maximum(jnp.maximum(lg_self, NEG_BIG), jnp.max(lg, axis=1, keepdims=True))
    p_self = jnp.exp(lg_self - m)
    pp = jnp.exp(lg - m)
    l = p_self + jnp.sum(pp, axis=1, keepdims=True)
    acc = p_self.astype(BF16).astype(F32) * vnew8 + _dot(pp.astype(BF16), vbuf[slot].astype(BF16))
    o_ref[...] = (acc / l).astype(o_ref.dtype)


def _s_attn(page_table_flat, q3, sel2, sel_self3, k_new3, v_new3, cache_k2, cache_v2, *, n_pages, past):
    bd = q3.shape[0]
    assert bd >= S_ATTN_SLOTS
    per_b = lambda s1, s2: pl.BlockSpec((None, s1, s2), lambda b, pt: (b, 0, 0))
    grid_spec = pltpu.PrefetchScalarGridSpec(
        num_scalar_prefetch=1,
        grid=(bd,),
        in_specs=[per_b(N_HEADS, HEAD_DIM), per_b(1, 2 * past), per_b(1, 128),
                  per_b(N_KV_HEADS, HEAD_DIM), per_b(N_KV_HEADS, HEAD_DIM),
                  pl.BlockSpec(memory_space=pl.ANY), pl.BlockSpec(memory_space=pl.ANY)],
        out_specs=per_b(N_HEADS, HEAD_DIM),
        scratch_shapes=[pltpu.VMEM((S_ATTN_SLOTS, 2 * past, HEAD_DIM), F32),
                        pltpu.VMEM((S_ATTN_SLOTS, 2 * past, HEAD_DIM), F32),
                        pltpu.SemaphoreType.DMA((S_ATTN_SLOTS,)), pltpu.SemaphoreType.DMA((S_ATTN_SLOTS,))],
    )
    return pl.pallas_call(
        functools.partial(_s_attn_kernel, n_pages=n_pages, past=past),
        out_shape=jax.ShapeDtypeStruct((bd, N_HEADS, HEAD_DIM), BF16),
        grid_spec=grid_spec,
        compiler_params=_params("arbitrary"),
        name="s_attn",
    )(page_table_flat, q3, sel2, sel_self3, k_new3, v_new3, cache_k2, cache_v2)


def _outln_kernel(mix_ref, x_ref, gate_ref, w_ref, g_ref, b_ref, o_ref, *wb_ref, alpha):
    w = w_ref[...]
    if wb_ref:
        w = w.astype(BF16)
        wb_ref[0][...] = w
    tm = o_ref.shape[0]
    slab = OUTLN_SLAB if tm % OUTLN_SLAB == 0 else tm
    gate = gate_ref[...]
    for lo in range(0, tm, slab):
        rows = slice(lo, lo + slab)
        mix = _dot(mix_ref[rows, :], w)
        y = alpha * x_ref[rows, :] + (1.0 + (gate if gate.shape[0] == 1 else gate[rows, :])) * mix
        o_ref[rows, :] = _layer_norm(y, g_ref[...], b_ref[...])


def _outln(mixin, x, gate, w, g, b, *, tm, alpha):
    n, d = x.shape
    nb, r, _ = gate.shape
    tiles_per_mod = (n // nb) // tm
    kin = mixin.shape[1]
    emit = w.dtype != BF16
    assert not emit or n == tm
    out_shape = [jax.ShapeDtypeStruct((n, d), F32)]
    out_specs = [pl.BlockSpec((tm, d), lambda i: (i, 0))]
    if emit:
        out_shape.append(jax.ShapeDtypeStruct((kin, d), BF16))
        out_specs.append(pl.BlockSpec((kin, d), lambda i: (0, 0)))
    res = pl.pallas_call(
        functools.partial(_outln_kernel, alpha=alpha),
        out_shape=out_shape,
        grid=(n // tm,),
        in_specs=[pl.BlockSpec((tm, kin), lambda i: (i, 0)),
                  pl.BlockSpec((tm, d), lambda i: (i, 0)),
                  pl.BlockSpec((None, r, d), lambda i: (i // tiles_per_mod, 0, 0)),
                  pl.BlockSpec((kin, d), lambda i: (0, 0), pipeline_mode=pl.Buffered(1)),
                  pl.BlockSpec((1, d), lambda i: (0, 0)),
                  pl.BlockSpec((1, d), lambda i: (0, 0))],
        out_specs=out_specs,
        compiler_params=_params("arbitrary"),
        name="outln",
    )(mixin, x, gate, w, g, b)
    return res if emit else res[0]


def _ffn_kernel(x_ref, shift_ref, scale_ref, gate_ref, w1_ref, b1_ref, w2_ref, b2_ref, g_ref, b_ref,
                o_ref, *rest, alpha):
    *wb_refs, h_scr = rest
    f = pl.program_id(1)

    @pl.when(f == 0)
    def _():
        h_scr[...] = (x_ref[...] * (1.0 + scale_ref[...]) + shift_ref[...]).astype(BF16)
        o_ref[...] = jnp.zeros_like(o_ref)

    if wb_refs:
        wb_refs[0][...] = w1_ref[...].astype(BF16)
        wb_refs[1][...] = w2_ref[...].astype(BF16)
        w1_ref, w2_ref = wb_refs
    a = jnp.maximum(_dot(h_scr[...], w1_ref[...]) + b1_ref[...], 0.0)
    a2 = (a * a).astype(BF16)
    for c in range(o_ref.shape[1] // FFN_TN):
        cs = slice(c * FFN_TN, (c + 1) * FFN_TN)
        o_ref[:, cs] += _dot(a2, w2_ref[:, cs])

    @pl.when(f == pl.num_programs(1) - 1)
    def _():
        y = alpha * x_ref[...] + (1.0 + gate_ref[...]) * (o_ref[...] + b2_ref[...])
        o_ref[...] = _layer_norm(y, g_ref[...], b_ref[...])


def _ffn(x, shift, scale, gate, w1, b1, w2, b2, g, b, *, tm, tf, alpha):
    n, d = x.shape
    nb, r, _ = gate.shape
    dff = w1.shape[1]
    tiles_per_mod = (n // nb) // tm
    mod_spec = pl.BlockSpec((None, r, d), lambda i, f: (i // tiles_per_mod, 0, 0))
    emit = w1.dtype != BF16
    assert not emit or n == tm
    out_shape = [jax.ShapeDtypeStruct((n, d), F32)]
    out_specs = [pl.BlockSpec((tm, d), lambda i, f: (i, 0))]
    if emit:
        out_shape += [jax.ShapeDtypeStruct((d, dff), BF16), jax.ShapeDtypeStruct((dff, d), BF16)]
        out_specs += [pl.BlockSpec((d, tf), lambda i, f: (0, f)), pl.BlockSpec((tf, d), lambda i, f: (f, 0))]
    res = pl.pallas_call(
        functools.partial(_ffn_kernel, alpha=alpha),
        out_shape=out_shape,
        grid=(n // tm, dff // tf),
        in_specs=[pl.BlockSpec((tm, d), lambda i, f: (i, 0), pipeline_mode=pl.Buffered(1)),
                  mod_spec, mod_spec, mod_spec,
                  pl.BlockSpec((d, tf), lambda i, f: (0, f)),
                  pl.BlockSpec((1, tf), lambda i, f: (0, f)),
                  pl.BlockSpec((tf, d), lambda i, f: (f, 0)),
                  pl.BlockSpec((1, d), lambda i, f: (0, 0)),
                  pl.BlockSpec((1, d), lambda i, f: (0, 0)),
                  pl.BlockSpec((1, d), lambda i, f: (0, 0))],
        out_specs=out_specs,
        scratch_shapes=[pltpu.VMEM((tm, d), BF16)],
        compiler_params=_params("arbitrary", "arbitrary"),
        name="ffn",
    )(x, shift, scale, gate, w1, b1, w2, b2, g, b)
    return res if emit else res[0]


def kernel(x_prompt, x_sample, cache_k, cache_v, cache_kidx, page_table, c_prompt, c_sample,
           w_cond, b_cond, w_in, ln_v_g, ln_v_b, w_spatial, b_spatial, w_out,
           ln1_g, ln1_b, w_ff1, b_ff1, w_ff2, b_ff2, ln2_g, ln2_b):
    batch, seq, d = x_prompt.shape
    bd, ts, _ = x_sample.shape
    depth = w_in.shape[0]
    n_pages = page_table.shape[1]
    past = n_pages * PAGE_SIZE
    n_pool = cache_k.shape[1]
    dff = w_ff1.shape[2]
    assert ts == 1 and seq % KC == 0 and d == A_WIDTH + ATT_WIDTH
    assert w_in.shape[2] == MAIN_WIDTH + IDX_DIM + IDX_HEADS
    n = batch * seq
    alpha = (2 * depth) ** 0.25
    n_sel_p = min(TOPK_MAX, seq // 4)
    n_sel_s = min(TOPK_MAX, (past + ts) // 4)
    tm_p = min(1024, seq)
    tm_o = min(512, seq)
    tf = min(1024, dff)
    pt_flat = page_table.reshape(-1).astype(I32)

    xp = x_prompt.reshape(n, d)
    xs = x_sample.reshape(bd, d)
    c_all = jnp.concatenate([c_prompt, c_sample], axis=0)

    outs = {k: [] for k in ("kp", "vp", "kip", "ks", "vs", "kis", "vc")}
    for l in range(depth):
        wm = jnp.swapaxes(w_in[l], 0, 1).astype(BF16)
        wt = jnp.pad(wm[MAIN_WIDTH:], ((0, 2 * IDX_DIM - IDX_DIM - IDX_HEADS), (0, 0)))
        kidx_t = jnp.swapaxes(cache_kidx[l], 1, 2)
        lng = ln_v_g[l].reshape(1, A_WIDTH)
        lnb = ln_v_b[l].reshape(1, A_WIDTH)
        tril = jnp.tril(jnp.ones((CHUNK, CHUNK), dtype=bool))
        ws = jnp.where(tril[None], w_spatial[l], 0.0).astype(BF16)
        bsp = jnp.repeat(jnp.transpose(b_spatial[l]), A_CH, axis=1)
        ws0 = jnp.repeat(w_spatial[l][:, 0, 0], A_CH).reshape(1, A_WIDTH)
        bs0 = jnp.repeat(b_spatial[l][:, 0], A_CH).reshape(1, A_WIDTH)
        b1 = b_ff1[l].reshape(1, dff)
        b2 = b_ff2[l].reshape(1, d)
        g1, be1 = ln1_g[l].reshape(1, d), ln1_b[l].reshape(1, d)
        g2, be2 = ln2_g[l].reshape(1, d), ln2_b[l].reshape(1, d)

        z = _cond(c_all, w_cond[l], b_cond[l])
        mods = [z[:, i * d:(i + 1) * d] for i in range(N_MOD)]
        mp = [m[:batch].reshape(batch, 1, d) for m in mods]
        ms = [m[batch:].reshape(1, bd, d) for m in mods]

        (u, vn, q, k, v, _, _, qi, ki, _, wit) = _proj(
            xs, ms[0], ms[1], wm, wt, lng, lnb, tm=bd, vn_dtype=F32)
        sc, sc_self = _s_scores(pt_flat, qi.reshape(bd, IDX_HEADS, IDX_DIM),
                                jnp.transpose(wit).reshape(bd, IDX_HEADS, 1),
                                ki.reshape(bd, 1, IDX_DIM), kidx_t, n_pages=n_pages)
        sel, sel_self, a_out = _s_select(sc.reshape(bd, past), sc_self.reshape(bd, 128), u, vn, ws0, bs0,
                                         n_sel=n_sel_s)
        sel2 = jnp.repeat(sel, 2, axis=1).reshape(bd, 1, 2 * past)
        b_out = _s_attn(pt_flat, q.reshape(bd, N_HEADS, HEAD_DIM), sel2, sel_self.reshape(bd, 1, 128),
                        k.reshape(bd, N_KV_HEADS, HEAD_DIM), v.reshape(bd, N_KV_HEADS, HEAD_DIM),
                        cache_k[l].reshape(n_pool, 2 * PAGE_SIZE, HEAD_DIM),
                        cache_v[l].reshape(n_pool, 2 * PAGE_SIZE, HEAD_DIM),
                        n_pages=n_pages, past=past)
        mixin = jnp.concatenate([a_out, b_out.reshape(bd, ATT_WIDTH)], axis=1)
        x1, w_out_b = _outln(mixin, xs, ms[2], w_out[l], g1, be1, tm=bd, alpha=alpha)
        xs, w1_b, w2_b = _ffn(x1, ms[3], ms[4], ms[5], w_ff1[l], b1, w_ff2[l], b2, g2, be2, tm=bd, tf=tf, alpha=alpha)
        outs["ks"].append(k.reshape(bd, ts, N_KV_HEADS, HEAD_DIM))
        outs["vs"].append(v.reshape(bd, ts, N_KV_HEADS, HEAD_DIM))
        outs["kis"].append(ki.reshape(bd, ts, IDX_DIM))
        outs["vc"].append(vn.reshape(bd, ts, A_GROUPS, A_CH))

        (u, vn, q, k, v, kb, vt, qi, ki, ki2, wit) = _proj(
            xp, mp[0], mp[1], wm, wt, lng, lnb, tm=tm_p, vn_dtype=BF16)
        mixin = _mix(u, vn, q, qi, wit, kb, vt, ki2, ws, bsp, batch=batch, seq=seq, n_sel=n_sel_p)
        x1 = _outln(mixin, xp, mp[2], w_out_b, g1, be1, tm=tm_o, alpha=alpha)
        xp = _ffn(x1, mp[3], mp[4], mp[5], w1_b, b1, w2_b, b2, g2, be2, tm=tm_p, tf=tf, alpha=alpha)
        outs["kp"].append(k.reshape(batch, seq, N_KV_HEADS, HEAD_DIM))
        outs["vp"].append(v.reshape(batch, seq, N_KV_HEADS, HEAD_DIM))
        outs["kip"].append(ki.reshape(batch, seq, IDX_DIM))

    st = lambda name: jnp.stack(outs[name])
    return (xp.reshape(batch, seq, d), xs.reshape(bd, ts, d),
            st("kp"), st("vp"), st("kip"), st("ks"), st("vs"), st("kis"), st("vc"))
```

```python
import functools

import jax
import jax.numpy as jnp
import numpy as np
from jax import lax
from jax.experimental import pallas as pl
from jax.experimental.pallas import tpu as pltpu

F32 = jnp.float32
BF16 = jnp.bfloat16
I32 = jnp.int32

CHUNK = 128
A_GROUPS = 8
A_CH = 128
A_WIDTH = A_GROUPS * A_CH
HEAD_DIM = 128
N_HEADS = 8
N_KV_HEADS = 2
GQA_GROUP = N_HEADS // N_KV_HEADS
ATT_WIDTH = N_HEADS * HEAD_DIM
KV_WIDTH = N_KV_HEADS * HEAD_DIM
IDX_HEADS = 16
IDX_DIM = 64
IDX_WIDTH = IDX_HEADS * IDX_DIM
TOPK_MAX = 256
Q_BLOCK = 128
PAGE_SIZE = 128
N_MOD = 6
LN_EPS = 1e-5
ATT_SCALE = HEAD_DIM ** -0.5
MAIN_WIDTH = 2 * A_WIDTH + ATT_WIDTH + 2 * KV_WIDTH + IDX_WIDTH
PROJ_TN = 512
FFN_TN = 512
OUTLN_SLAB = 256
ALIBI_SLOPES = tuple(float(2.0 ** (-8.0 * h / N_HEADS)) for h in range(1, N_HEADS + 1))

VMEM_LIMIT_BYTES = 58 * 1024 * 1024
INT_MIN = -(2 ** 31)
NEG_BIG = -1e30
FLT_MAX = float(np.finfo(np.float32).max)
LOG2E = float(np.log2(np.e))
KC = 512
KU = 256
S_SCORES_SLOTS = 4
S_ATTN_SLOTS = 3
LANE_HEADS = 4
DEN_ROWS = 16

NT_DIMS = (((1,), (1,)), ((), ()))


def _dot(a, b):
    return jnp.dot(a, b, preferred_element_type=F32)


def _dot_nt(a, b):
    return lax.dot_general(a, b, NT_DIMS, preferred_element_type=F32)


def _layer_norm(x, g, b):
    mu = jnp.mean(x, axis=-1, keepdims=True)
    xc = x - mu
    var = jnp.mean(xc * xc, axis=-1, keepdims=True)
    return xc * lax.rsqrt(var + LN_EPS) * g + b


def _key_to_float(key):
    bits = jnp.where(key < 0, key ^ jnp.int32(0x7FFFFFFF), key)
    return pltpu.bitcast(bits, F32)


def _params(*sem):
    return pltpu.CompilerParams(dimension_semantics=sem, vmem_limit_bytes=VMEM_LIMIT_BYTES)


def _cond_kernel(c_ref, w_ref, b_ref, o_ref):
    c = c_ref[...]
    a = (c * jax.nn.sigmoid(c)).astype(BF16)
    o_ref[...] = _dot(a, w_ref[...].astype(BF16)) + b_ref[...]


def _cond(c, w, b):
    m, d = c.shape
    n = w.shape[1]
    tn = 1024
    return pl.pallas_call(
        _cond_kernel,
        out_shape=jax.ShapeDtypeStruct((m, n), F32),
        grid=(n // tn,),
        in_specs=[pl.BlockSpec((m, d), lambda j: (0, 0)),
                  pl.BlockSpec((d, tn), lambda j: (0, j)),
                  pl.BlockSpec((1, tn), lambda j: (0, j))],
        out_specs=pl.BlockSpec((m, tn), lambda j: (0, j)),
        compiler_params=_params("arbitrary"),
        name="cond",
    )(c, w, b.reshape(1, n))


_J_V, _J_Q, _J_KV, _J_QI, _J_END = 2, 4, 6, 7, 9


def _proj_kernel(x_ref, shift_ref, scale_ref, wm_ref, wt_ref, lng_ref, lnb_ref,
                 u_ref, vn_ref, q_ref, k_ref, v_ref, kb_ref, vt_ref, qi_ref, ki_ref, ki2_ref, wit_ref,
                 h_scr):
    j = pl.program_id(1)

    @pl.when(j == 0)
    def _():
        h = (x_ref[...] * (1.0 + scale_ref[...]) + shift_ref[...]).astype(BF16)
        h_scr[...] = h
        tail = _dot_nt(h, wt_ref[...])
        ki_ref[...] = tail[:, :IDX_DIM]
        lane = lax.broadcasted_iota(I32, tail.shape, 1)
        ki2_ref[...] = jnp.where(lane < IDX_DIM, tail, pltpu.roll(tail, IDX_DIM, axis=1)).astype(BF16)
        wit_ref[...] = tail.T[IDX_DIM:IDX_DIM + IDX_HEADS, :] * (IDX_HEADS ** -0.5)

    def tile():
        return _dot_nt(h_scr[...], wm_ref[...])

    @pl.when(j < _J_V)
    def _():
        u_ref[...] = tile()

    @pl.when(jnp.logical_and(j >= _J_V, j < _J_Q))
    def _():
        z = tile()
        for g in range(PROJ_TN // A_CH):
            sl = slice(g * A_CH, (g + 1) * A_CH)
            vn_ref[:, sl] = _layer_norm(z[:, sl], lng_ref[:, sl], lnb_ref[:, sl]).astype(vn_ref.dtype)

    @pl.when(jnp.logical_and(j >= _J_Q, j < _J_KV))
    def _():
        q_ref[...] = tile().astype(BF16)

    @pl.when(j == _J_KV)
    def _():
        z = tile()
        for hh in range(N_KV_HEADS):
            k_ref[:, hh, :] = z[:, hh * HEAD_DIM:(hh + 1) * HEAD_DIM]
            v_ref[:, hh, :] = z[:, KV_WIDTH + hh * HEAD_DIM:KV_WIDTH + (hh + 1) * HEAD_DIM]
        kb_ref[...] = z[:, :KV_WIDTH].astype(BF16)
        vt = z[:, KV_WIDTH:].T.astype(BF16)
        for c in range(vt_ref.shape[0]):
            vt_ref[c] = vt[:, c * 128:(c + 1) * 128]

    @pl.when(j >= _J_QI)
    def _():
        qi_ref[...] = tile().astype(BF16)


def _proj(x, shift, scale, wm, wt, lng, lnb, *, tm, vn_dtype):
    n, d = x.shape
    nb, r, _ = shift.shape
    rows_per_mod = n // nb
    assert n % tm == 0 and rows_per_mod % tm == 0 and r in (1, tm)
    tiles_per_mod = rows_per_mod // tm
    tn = PROJ_TN

    def clipj(lo, cnt):
        return lambda i, j: (i, jnp.clip(j - lo, 0, cnt - 1))

    mod_spec = pl.BlockSpec((None, r, d), lambda i, j: (i // tiles_per_mod, 0, 0))
    out_shape = (
        jax.ShapeDtypeStruct((n, A_WIDTH), F32),
        jax.ShapeDtypeStruct((n, A_WIDTH), vn_dtype),
        jax.ShapeDtypeStruct((n, ATT_WIDTH), BF16),
        jax.ShapeDtypeStruct((n, N_KV_HEADS, HEAD_DIM), F32),
        jax.ShapeDtypeStruct((n, N_KV_HEADS, HEAD_DIM), F32),
        jax.ShapeDtypeStruct((n, KV_WIDTH), BF16),
        jax.ShapeDtypeStruct((n // 128, KV_WIDTH, 128), BF16),
        jax.ShapeDtypeStruct((n, IDX_WIDTH), BF16),
        jax.ShapeDtypeStruct((n, IDX_DIM), F32),
        jax.ShapeDtypeStruct((n, 2 * IDX_DIM), BF16),
        jax.ShapeDtypeStruct((IDX_HEADS, n), F32),
    )
    out_specs = (
        pl.BlockSpec((tm, tn), clipj(0, 2)),
        pl.BlockSpec((tm, tn), clipj(_J_V, 2)),
        pl.BlockSpec((tm, tn), clipj(_J_Q, 2)),
        pl.BlockSpec((tm, N_KV_HEADS, HEAD_DIM), lambda i, j: (i, 0, 0)),
        pl.BlockSpec((tm, N_KV_HEADS, HEAD_DIM), lambda i, j: (i, 0, 0)),
        pl.BlockSpec((tm, KV_WIDTH), lambda i, j: (i, 0)),
        pl.BlockSpec((tm // 128, KV_WIDTH, 128), lambda i, j: (i, 0, 0)),
        pl.BlockSpec((tm, tn), clipj(_J_QI, 2)),
        pl.BlockSpec((tm, IDX_DIM), lambda i, j: (i, 0)),
        pl.BlockSpec((tm, 2 * IDX_DIM), lambda i, j: (i, 0)),
        pl.BlockSpec((IDX_HEADS, tm), lambda i, j: (0, i)),
    )
    in_specs = [
        pl.BlockSpec((tm, d), lambda i, j: (i, 0)),
        mod_spec, mod_spec,
        pl.BlockSpec((tn, d), lambda i, j: (j, 0)),
        pl.BlockSpec((2 * IDX_DIM, d), lambda i, j: (0, 0)),
        pl.BlockSpec((1, tn), lambda i, j: (0, jnp.clip(j - _J_V, 0, 1))),
        pl.BlockSpec((1, tn), lambda i, j: (0, jnp.clip(j - _J_V, 0, 1))),
    ]
    return pl.pallas_call(
        _proj_kernel,
        out_shape=out_shape,
        grid=(n // tm, _J_END),
        in_specs=in_specs,
        out_specs=out_specs,
        scratch_shapes=[pltpu.VMEM((tm, d), BF16)],
        compiler_params=_params("arbitrary", "arbitrary"),
        name="proj",
    )(x, shift, scale, wm, wt, lng, lnb)


def _mix_kernel(u_ref, vn_ref, q_ref, qi_ref, wit_ref, k_ref, vt_ref, ki2_ref, ws_ref, bsp_ref,
                o_ref, sc_scr, msk_scr, qim_scr, ab_scr, acc_scr, thr_scr, *, n_sel):
    jq = pl.program_id(1)
    n_unit = jq // (KU // 128) + 1
    max_units = sc_scr.shape[0] // KU

    def chunks_of(units):
        full = [(c * KC, KC) for c in range(units * KU // KC)]
        return full + ([(len(full) * KC, units * KU - len(full) * KC)] if units * KU % KC else [])

    def per_unit_count(stage):
        for units in range(1, max_units + 1):
            pl.when(n_unit == units)(functools.partial(stage, chunks_of(units)))

    row = lax.broadcasted_iota(I32, (128, 128), 0)
    col = lax.broadcasted_iota(I32, (128, 128), 1)

    @pl.when(jnp.logical_and(pl.program_id(0) == 0, jq == 0))
    def _():
        d0 = (lax.broadcasted_iota(I32, (KC, 128), 1) - lax.broadcasted_iota(I32, (KC, 128), 0)).astype(F32)
        for h in range(N_HEADS):
            lanes = slice((h % LANE_HEADS) * 128, (h % LANE_HEADS + 1) * 128)
            ab_scr[h // LANE_HEADS, :, lanes] = (ALIBI_SLOPES[h] * LOG2E) * d0

    for p in range(IDX_HEADS // 2):
        pair = qi_ref[:, p * 128:(p + 1) * 128]
        zero = jnp.zeros_like(pair)
        qim_scr[p, 0:128, :] = jnp.where(col < IDX_DIM, pair, zero)
        qim_scr[p, 128:256, :] = jnp.where(col >= IDX_DIM, pair, zero)
    wis = wit_ref[...] * (IDX_DIM ** -0.5)
    key_minus_query = (lax.broadcasted_iota(I32, (KC, 128), 0) - lax.broadcasted_iota(I32, (KC, 128), 1))

    def index_chunks(chunks):
        for start, size in chunks:
            kk = ki2_ref[start:start + size, :]
            acc = jnp.zeros((size, 128), F32)
            for p in range(IDX_HEADS // 2):
                s = _dot_nt(kk, qim_scr[p])
                acc = (acc + jnp.maximum(s[:, :128], 0.0) * wis[2 * p:2 * p + 1, :]
                       + jnp.maximum(s[:, 128:], 0.0) * wis[2 * p + 1:2 * p + 2, :])
            inadmissible = key_minus_query[:size, :] > (jq * 128 - start)
            sc_scr[start:start + size, :] = jnp.where(inadmissible, -jnp.inf, acc)

    per_unit_count(index_chunks)

    def count(pred):
        def body(ku, c):
            x = sc_scr[pl.ds(pl.multiple_of(ku * KU, KU), KU), :]
            m = jnp.where(pred(x), 1, 0).astype(I32)
            return c + jnp.sum(m.reshape(KU // 8, 8, 128), axis=0)
        c = lax.fori_loop(0, n_unit, body, jnp.zeros((8, 128), I32))
        return jnp.sum(c, axis=0, keepdims=True)

    def search(chunks):
        def bit_body(i, t):
            cand = t ^ lax.shift_left(jnp.int32(1), 31 - i)
            cand_f = _key_to_float(cand)
            c = jnp.zeros((8, 128), I32)
            for start, size in chunks:
                m = jnp.where(sc_scr[start:start + size, :] >= cand_f, 1, 0).astype(I32)
                c = c + jnp.sum(m.reshape(size // 8, 8, 128), axis=0)
            return jnp.where(jnp.sum(c, axis=0, keepdims=True) >= n_sel, cand, t)

        t = lax.fori_loop(0, 32, bit_body, jnp.full((1, 128), INT_MIN, I32))
        thr_scr[...] = jnp.broadcast_to(t, thr_scr.shape)

    per_unit_count(search)
    thr_key = thr_scr[0:1, :]
    thr = jnp.where(thr_key == jnp.int32(INT_MIN), -FLT_MAX, _key_to_float(thr_key))
    c_ge = count(lambda x: x >= thr)
    c_gt = count(lambda x: x > thr)
    has_tie_overflow = jnp.max(c_ge) > n_sel

    @pl.when(jnp.logical_not(has_tie_overflow))
    def _():
        def body(ku, carry):
            sl = pl.ds(pl.multiple_of(ku * KU, KU), KU)
            msk_scr[sl, :] = jnp.where(sc_scr[sl, :] >= thr, 0.0, -jnp.inf).astype(F32)
            return carry
        lax.fori_loop(0, n_unit, body, 0)

    @pl.when(has_tie_overflow)
    def _():
        need = (n_sel - c_gt).astype(F32)
        lstrict = jnp.where(col < row, 1.0, 0.0).astype(BF16)

        def body(kt, before):
            sl = pl.ds(pl.multiple_of(kt * 128, 128), 128)
            x = sc_scr[sl, :]
            eq = x == thr
            eqf = jnp.where(eq, 1.0, 0.0).astype(F32)
            rank = before + _dot(lstrict, eqf.astype(BF16))
            sel = jnp.logical_or(x > thr, jnp.logical_and(eq, rank < need))
            msk_scr[sl, :] = jnp.where(sel, 0.0, -jnp.inf).astype(F32)
            return before + jnp.sum(eqf, axis=0, keepdims=True)
        lax.fori_loop(0, n_unit * (KU // 128), body, jnp.zeros((1, 128), F32))

    gw = LANE_HEADS * 128
    n_lg = N_HEADS // LANE_HEADS
    lane_head = lax.broadcasted_iota(I32, (1, gw), 1) // 128
    q_grp, slope_vec = [], []
    for g in range(n_lg):
        heads = range(g * LANE_HEADS, (g + 1) * LANE_HEADS)
        q_grp.append(jnp.concatenate([q_ref[:, h * HEAD_DIM:(h + 1) * HEAD_DIM] for h in heads], axis=0))
        sv = jnp.zeros((1, gw), F32)
        for i, h in enumerate(heads):
            sv = jnp.where(lane_head == i, ALIBI_SLOPES[h] * LOG2E, sv)
        slope_vec.append(sv)
        acc_scr[g] = jnp.zeros((HEAD_DIM + DEN_ROWS, gw), F32)

    def att_chunk(start, size, carry):
        off = (jq * 128 - start).astype(F32)
        mk = msk_scr[start:start + size, :]
        mkw = jnp.concatenate([mk] * LANE_HEADS, axis=1)
        kv_of = [g * LANE_HEADS // GQA_GROUP for g in range(n_lg)]
        raw = [_dot_nt(k_ref[start:start + size, kv_of[g] * HEAD_DIM:(kv_of[g] + 1) * HEAD_DIM], q_grp[g])
               for g in range(n_lg)]
        new = []
        for g in range(n_lg):
            kv = kv_of[g]
            m = carry[g]
            x = raw[g] * (ATT_SCALE * LOG2E) - ab_scr[g, :size, :] + mkw
            cvec = slope_vec[g] * off
            m_new = jnp.maximum(m, jnp.max(x, axis=0, keepdims=True) - cvec)
            alpha = jnp.exp2(m - m_new)
            p = jnp.exp2(x - (m_new + cvec)).astype(BF16)
            vt_c = jnp.concatenate(
                [jnp.concatenate([vt_ref[start // 128 + t, kv * HEAD_DIM:(kv + 1) * HEAD_DIM, :]
                                  for t in range(size // 128)], axis=1),
                 jnp.ones((DEN_ROWS, size), BF16)], axis=0)
            acc_scr[g] = alpha * acc_scr[g] + _dot(vt_c, p)
            new.append(m_new)
        return tuple(new)

    def attend(chunks):
        carry = (jnp.full((1, gw), NEG_BIG, F32),) * n_lg
        for start, size in chunks:
            carry = att_chunk(start, size, carry)

    per_unit_count(attend)
    for g in range(n_lg):
        out_t = acc_scr[g, :HEAD_DIM, :] / acc_scr[g, HEAD_DIM:HEAD_DIM + 1, :]
        for i in range(LANE_HEADS):
            lo = A_WIDTH + (g * LANE_HEADS + i) * HEAD_DIM
            o_ref[:, lo:lo + HEAD_DIM] = out_t[:, i * 128:(i + 1) * 128].T.astype(o_ref.dtype)

    for g in range(A_GROUPS):
        sl = slice(g * A_CH, (g + 1) * A_CH)
        sp = _dot(ws_ref[g], vn_ref[:, sl]) + bsp_ref[:, sl]
        o_ref[:, sl] = (u_ref[:, sl] * sp).astype(o_ref.dtype)


def _mix(u, vn, q, qi, wit, kb, vt, ki2, ws, bsp, *, batch, seq, n_sel):
    n = batch * seq
    nblk = seq // Q_BLOCK
    ntile = seq // 128
    row_spec = lambda w: pl.BlockSpec((Q_BLOCK, w), lambda b, j: (b * nblk + j, 0))
    in_specs = [
        row_spec(A_WIDTH), row_spec(A_WIDTH), row_spec(ATT_WIDTH), row_spec(IDX_WIDTH),
        pl.BlockSpec((IDX_HEADS, Q_BLOCK), lambda b, j: (0, b * nblk + j)),
        pl.BlockSpec((seq, KV_WIDTH), lambda b, j: (b, 0)),
        pl.BlockSpec((ntile, KV_WIDTH, 128), lambda b, j: (b, 0, 0)),
        pl.BlockSpec((seq, 2 * IDX_DIM), lambda b, j: (b, 0)),
        pl.BlockSpec((A_GROUPS, CHUNK, CHUNK), lambda b, j: (0, 0, 0)),
        pl.BlockSpec((CHUNK, A_WIDTH), lambda b, j: (0, 0)),
    ]
    return pl.pallas_call(
        functools.partial(_mix_kernel, n_sel=n_sel),
        out_shape=jax.ShapeDtypeStruct((n, A_WIDTH + ATT_WIDTH), BF16),
        grid=(batch, nblk),
        in_specs=in_specs,
        out_specs=pl.BlockSpec((Q_BLOCK, A_WIDTH + ATT_WIDTH), lambda b, j: (b * nblk + j, 0)),
        scratch_shapes=[pltpu.VMEM((seq, 128), F32),
                        pltpu.VMEM((seq, 128), F32),
                        pltpu.VMEM((IDX_HEADS // 2, 256, 128), BF16),
                        pltpu.VMEM((N_HEADS // LANE_HEADS, KC, LANE_HEADS * 128), F32),
                        pltpu.VMEM((N_HEADS // LANE_HEADS, HEAD_DIM + DEN_ROWS, LANE_HEADS * 128), F32),
                        pltpu.VMEM((8, 128), I32)],
        compiler_params=_params("arbitrary", "arbitrary"),
        name="mix",
    )(u, vn, q, qi, wit, kb, vt, ki2, ws, bsp)


def _page_copies(pt_ref, row, n_pages, slot, streams, *, for_wait):
    cps = []
    for p in range(n_pages):
        page = 0 if for_wait else pt_ref[row * n_pages + p]
        for hbm, buf, sem, place in streams:
            cps.append(pltpu.make_async_copy(hbm.at[page], buf.at[(slot,) + place(p)], sem.at[slot]))
    return cps


def _gather_pages(pt_ref, n_pages, streams, n_slots):
    b = pl.program_id(0)
    ahead = n_slots - 1

    @pl.when(b == 0)
    def _():
        for r in range(ahead):
            for c in _page_copies(pt_ref, r, n_pages, r, streams, for_wait=False):
                c.start()

    @pl.when(b + ahead < pl.num_programs(0))
    def _():
        for c in _page_copies(pt_ref, b + ahead, n_pages, (b + ahead) % n_slots, streams, for_wait=False):
            c.start()

    slot = b % n_slots
    for c in _page_copies(pt_ref, b, n_pages, slot, streams, for_wait=True):
        c.wait()
    return slot


def _s_scores_kernel(pt_ref, qi_ref, wi_ref, kin_ref, kidx_hbm, sc_ref, self_ref, kbuf, sem, *, n_pages):
    place = lambda p: (slice(None), pl.ds(p * PAGE_SIZE, PAGE_SIZE))
    slot = _gather_pages(pt_ref, n_pages, [(kidx_hbm, kbuf, sem, place)], S_SCORES_SLOTS)
    qi = qi_ref[...]
    w = wi_ref[...] * (IDX_DIM ** -0.5)
    kcat = kbuf[slot].astype(BF16)
    s = _dot(qi, kcat)
    sc_ref[...] = jnp.sum(jnp.maximum(s, 0.0) * w, axis=0, keepdims=True)
    kin = kin_ref[...].astype(BF16).astype(F32)
    s_self = jnp.sum(qi.astype(F32) * kin, axis=1, keepdims=True)
    v_self = jnp.sum(jnp.maximum(s_self, 0.0) * w, axis=0, keepdims=True)
    self_ref[...] = jnp.broadcast_to(v_self, self_ref.shape)


def _s_scores(page_table_flat, qi3, wi_col, ki_new3, cache_kidx_l, *, n_pages):
    bd = qi3.shape[0]
    assert bd >= S_SCORES_SLOTS
    past = n_pages * PAGE_SIZE
    grid_spec = pltpu.PrefetchScalarGridSpec(
        num_scalar_prefetch=1,
        grid=(bd,),
        in_specs=[pl.BlockSpec((None, IDX_HEADS, IDX_DIM), lambda b, pt: (b, 0, 0)),
                  pl.BlockSpec((None, IDX_HEADS, 1), lambda b, pt: (b, 0, 0)),
                  pl.BlockSpec((None, 1, IDX_DIM), lambda b, pt: (b, 0, 0)),
                  pl.BlockSpec(memory_space=pl.ANY)],
        out_specs=(pl.BlockSpec((None, 1, past), lambda b, pt: (b, 0, 0)),
                   pl.BlockSpec((None, 1, 128), lambda b, pt: (b, 0, 0))),
        scratch_shapes=[pltpu.VMEM((S_SCORES_SLOTS, IDX_DIM, past), F32),
                        pltpu.SemaphoreType.DMA((S_SCORES_SLOTS,))],
    )
    return pl.pallas_call(
        functools.partial(_s_scores_kernel, n_pages=n_pages),
        out_shape=(jax.ShapeDtypeStruct((bd, 1, past), F32),
                   jax.ShapeDtypeStruct((bd, 1, 128), F32)),
        grid_spec=grid_spec,
        compiler_params=_params("arbitrary"),
        name="s_scores",
    )(page_table_flat, qi3, wi_col, ki_new3, cache_kidx_l)


def _s_select_kernel(sc_ref, self_ref, u_ref, vn_ref, ws0_ref, bs0_ref,
                     sel_ref, selself_ref, a_ref, *, n_sel):
    keys = sc_ref[...]
    kself = self_ref[:, 0:1]
    bd, past = keys.shape

    def count_ge(cand):
        c = jnp.sum(jnp.where(keys >= cand, 1, 0).astype(I32), axis=1, keepdims=True)
        return c + jnp.where(kself >= cand, 1, 0).astype(I32)

    def bit_body(i, t):
        cand = t ^ lax.shift_left(jnp.int32(1), 31 - i)
        return jnp.where(count_ge(_key_to_float(cand)) >= n_sel, cand, t)

    thr = _key_to_float(lax.fori_loop(0, 32, bit_body, jnp.full((bd, 1), INT_MIN, I32)))
    c_gt = (jnp.sum(jnp.where(keys > thr, 1, 0).astype(I32), axis=1, keepdims=True)
            + jnp.where(kself > thr, 1, 0).astype(I32))
    need = (n_sel - c_gt).astype(F32)
    r = lax.broadcasted_iota(I32, (128, 128), 0)
    c = lax.broadcasted_iota(I32, (128, 128), 1)
    ustrict = jnp.where(r < c, 1.0, 0.0).astype(BF16)
    before = jnp.zeros((bd, 1), F32)
    for t in range(past // 128):
        kt = keys[:, t * 128:(t + 1) * 128]
        eq = kt == thr
        eqf = jnp.where(eq, 1.0, 0.0).astype(F32)
        rank = before + _dot(eqf.astype(BF16), ustrict)
        sel = jnp.logical_or(kt > thr, jnp.logical_and(eq, rank < need))
        sel_ref[:, t * 128:(t + 1) * 128] = jnp.where(sel, 1.0, 0.0).astype(F32)
        before = before + jnp.sum(eqf, axis=1, keepdims=True)
    sel_self = jnp.logical_or(kself > thr, jnp.logical_and(kself == thr, before < need))
    selself_ref[...] = jnp.broadcast_to(jnp.where(sel_self, 1.0, 0.0).astype(F32), selself_ref.shape)
    sp = ws0_ref[...] * vn_ref[...] + bs0_ref[...]
    a_ref[...] = (u_ref[...] * sp).astype(a_ref.dtype)


def _s_select(sc, sc_self, u, vn, ws0, bs0, *, n_sel):
    bd, past = sc.shape
    full = lambda a: pl.BlockSpec(a.shape, lambda i: (0,) * a.ndim)
    args = (sc, sc_self, u, vn, ws0, bs0)
    return pl.pallas_call(
        functools.partial(_s_select_kernel, n_sel=n_sel),
        out_shape=(jax.ShapeDtypeStruct((bd, past), F32),
                   jax.ShapeDtypeStruct((bd, 128), F32),
                   jax.ShapeDtypeStruct((bd, A_WIDTH), BF16)),
        grid=(1,),
        in_specs=[full(a) for a in args],
        out_specs=(pl.BlockSpec((bd, past), lambda i: (0, 0)),
                   pl.BlockSpec((bd, 128), lambda i: (0, 0)),
                   pl.BlockSpec((bd, A_WIDTH), lambda i: (0, 0))),
        compiler_params=_params("arbitrary"),
        name="s_select",
    )(*args)


def _s_attn_kernel(pt_ref, q_ref, sel_ref, selself_ref, knew_ref, vnew_ref, ck_hbm, cv_hbm, o_ref,
                   kbuf, vbuf, ksem, vsem, *, n_pages, past):
    place = lambda p: (pl.ds(p * 2 * PAGE_SIZE, 2 * PAGE_SIZE), slice(None))
    slot = _gather_pages(pt_ref, n_pages, [(ck_hbm, kbuf, ksem, place), (cv_hbm, vbuf, vsem, place)], S_ATTN_SLOTS)
    q = q_ref[...]
    hrow = lax.broadcasted_iota(I32, (N_HEADS, 2 * past), 0)
    ccol = lax.broadcasted_iota(I32, (N_HEADS, 2 * past), 1)
    own_kv = (ccol & 1) == (hrow // GQA_GROUP)
    hcol = lax.broadcasted_iota(I32, (N_HEADS, 1), 0)
    slope = jnp.zeros((N_HEADS, 1), F32)
    for h in range(N_HEADS):
        slope = jnp.where(hcol == h, ALIBI_SLOPES[h], slope)

    lg = _dot_nt(q, kbuf[slot].astype(BF16)) * ATT_SCALE
    lg = lg - slope * (past - (ccol >> 1)).astype(F32)
    lg = jnp.where(jnp.logical_and(own_kv, sel_ref[...] > 0.5), lg, -jnp.inf)
    first_group = lax.broadcasted_iota(I32, (N_HEADS, HEAD_DIM), 0) < GQA_GROUP
    knew = knew_ref[...].astype(BF16).astype(F32)
    vnew = vnew_ref[...].astype(BF16).astype(F32)
    knew8 = jnp.where(first_group, knew[0:1, :], knew[1:2, :])
    vnew8 = jnp.where(first_group, vnew[0:1, :], vnew[1:2, :])
    lg_self = jnp.sum(q.astype(F32) * knew8, axis=1, keepdims=True) * ATT_SCALE
    lg_self = jnp.where(selself_ref[:, 0:1] > 0.5, lg_self, -jnp.inf)

    m = jnp.maximum(jnp.maximum(lg_self, NEG_BIG), jnp.max(lg, axis=1, keepdims=True))
    p_self = jnp.exp(lg_self - m)
    pp = jnp.exp(lg - m)
    l = p_self + jnp.sum(pp, axis=1, keepdims=True)
    acc = p_self.astype(BF16).astype(F32) * vnew8 + _dot(pp.astype(BF16), vbuf[slot].astype(BF16))
    o_ref[...] = (acc / l).astype(o_ref.dtype)


def _s_attn(page_table_flat, q3, sel2, sel_self3, k_new3, v_new3, cache_k2, cache_v2, *, n_pages, past):
    bd = q3.shape[0]
    assert bd >= S_ATTN_SLOTS
    per_b = lambda s1, s2: pl.BlockSpec((None, s1, s2), lambda b, pt: (b, 0, 0))
    grid_spec = pltpu.PrefetchScalarGridSpec(
        num_scalar_prefetch=1,
        grid=(bd,),
        in_specs=[per_b(N_HEADS, HEAD_DIM), per_b(1, 2 * past), per_b(1, 128),
                  per_b(N_KV_HEADS, HEAD_DIM), per_b(N_KV_HEADS, HEAD_DIM),
                  pl.BlockSpec(memory_space=pl.ANY), pl.BlockSpec(memory_space=pl.ANY)],
        out_specs=per_b(N_HEADS, HEAD_DIM),
        scratch_shapes=[pltpu.VMEM((S_ATTN_SLOTS, 2 * past, HEAD_DIM), F32),
                        pltpu.VMEM((S_ATTN_SLOTS, 2 * past, HEAD_DIM), F32),
                        pltpu.SemaphoreType.DMA((S_ATTN_SLOTS,)), pltpu.SemaphoreType.DMA((S_ATTN_SLOTS,))],
    )
    return pl.pallas_call(
        functools.partial(_s_attn_kernel, n_pages=n_pages, past=past),
        out_shape=jax.ShapeDtypeStruct((bd, N_HEADS, HEAD_DIM), BF16),
        grid_spec=grid_spec,
        compiler_params=_params("arbitrary"),
        name="s_attn",
    )(page_table_flat, q3, sel2, sel_self3, k_new3, v_new3, cache_k2, cache_v2)


def _outln_kernel(mix_ref, x_ref, gate_ref, w_ref, g_ref, b_ref, o_ref, *wb_ref, alpha):
    w = w_ref[...]
    if wb_ref:
        w = w.astype(BF16)
        wb_ref[0][...] = w
    tm = o_ref.shape[0]
    slab = OUTLN_SLAB if tm % OUTLN_SLAB == 0 else tm
    gate = gate_ref[...]
    for lo in range(0, tm, slab):
        rows = slice(lo, lo + slab)
        mix = _dot(mix_ref[rows, :], w)
        y = alpha * x_ref[rows, :] + (1.0 + (gate if gate.shape[0] == 1 else gate[rows, :])) * mix
        o_ref[rows, :] = _layer_norm(y, g_ref[...], b_ref[...])


def _outln(mixin, x, gate, w, g, b, *, tm, alpha):
    n, d = x.shape
    nb, r, _ = gate.shape
    tiles_per_mod = (n // nb) // tm
    kin = mixin.shape[1]
    emit = w.dtype != BF16
    assert not emit or n == tm
    out_shape = [jax.ShapeDtypeStruct((n, d), F32)]
    out_specs = [pl.BlockSpec((tm, d), lambda i: (i, 0))]
    if emit:
        out_shape.append(jax.ShapeDtypeStruct((kin, d), BF16))
        out_specs.append(pl.BlockSpec((kin, d), lambda i: (0, 0)))
    res = pl.pallas_call(
        functools.partial(_outln_kernel, alpha=alpha),
        out_shape=out_shape,
        grid=(n // tm,),
        in_specs=[pl.BlockSpec((tm, kin), lambda i: (i, 0)),
                  pl.BlockSpec((tm, d), lambda i: (i, 0)),
                  pl.BlockSpec((None, r, d), lambda i: (i // tiles_per_mod, 0, 0)),
                  pl.BlockSpec((kin, d), lambda i: (0, 0), pipeline_mode=pl.Buffered(1)),
                  pl.BlockSpec((1, d), lambda i: (0, 0)),
                  pl.BlockSpec((1, d), lambda i: (0, 0))],
        out_specs=out_specs,
        compiler_params=_params("arbitrary"),
        name="outln",
    )(mixin, x, gate, w, g, b)
    return res if emit else res[0]


def _ffn_kernel(x_ref, shift_ref, scale_ref, gate_ref, w1_ref, b1_ref, w2_ref, b2_ref, g_ref, b_ref,
                o_ref, *rest, alpha):
    *wb_refs, h_scr = rest
    f = pl.program_id(1)

    @pl.when(f == 0)
    def _():
        h_scr[...] = (x_ref[...] * (1.0 + scale_ref[...]) + shift_ref[...]).astype(BF16)
        o_ref[...] = jnp.zeros_like(o_ref)

    if wb_refs:
        wb_refs[0][...] = w1_ref[...].astype(BF16)
        wb_refs[1][...] = w2_ref[...].astype(BF16)
        w1_ref, w2_ref = wb_refs
    a = jnp.maximum(_dot(h_scr[...], w1_ref[...]) + b1_ref[...], 0.0)
    a2 = (a * a).astype(BF16)
    for c in range(o_ref.shape[1] // FFN_TN):
        cs = slice(c * FFN_TN, (c + 1) * FFN_TN)
        o_ref[:, cs] += _dot(a2, w2_ref[:, cs])

    @pl.when(f == pl.num_programs(1) - 1)
    def _():
        y = alpha * x_ref[...] + (1.0 + gate_ref[...]) * (o_ref[...] + b2_ref[...])
        o_ref[...] = _layer_norm(y, g_ref[...], b_ref[...])


def _ffn(x, shift, scale, gate, w1, b1, w2, b2, g, b, *, tm, tf, alpha):
    n, d = x.shape
    nb, r, _ = gate.shape
    dff = w1.shape[1]
    tiles_per_mod = (n // nb) // tm
    mod_spec = pl.BlockSpec((None, r, d), lambda i, f: (i // tiles_per_mod, 0, 0))
    emit = w1.dtype != BF16
    assert not emit or n == tm
    out_shape = [jax.ShapeDtypeStruct((n, d), F32)]
    out_specs = [pl.BlockSpec((tm, d), lambda i, f: (i, 0))]
    if emit:
        out_shape += [jax.ShapeDtypeStruct((d, dff), BF16), jax.ShapeDtypeStruct((dff, d), BF16)]
        out_specs += [pl.BlockSpec((d, tf), lambda i, f: (0, f)), pl.BlockSpec((tf, d), lambda i, f: (f, 0))]
    res = pl.pallas_call(
        functools.partial(_ffn_kernel, alpha=alpha),
        out_shape=out_shape,
        grid=(n // tm, dff // tf),
        in_specs=[pl.BlockSpec((tm, d), lambda i, f: (i, 0), pipeline_mode=pl.Buffered(1)),
                  mod_spec, mod_spec, mod_spec,
                  pl.BlockSpec((d, tf), lambda i, f: (0, f)),
                  pl.BlockSpec((1, tf), lambda i, f: (0, f)),
                  pl.BlockSpec((tf, d), lambda i, f: (f, 0)),
                  pl.BlockSpec((1, d), lambda i, f: (0, 0)),
                  pl.BlockSpec((1, d), lambda i, f: (0, 0)),
                  pl.BlockSpec((1, d), lambda i, f: (0, 0))],
        out_specs=out_specs,
        scratch_shapes=[pltpu.VMEM((tm, d), BF16)],
        compiler_params=_params("arbitrary", "arbitrary"),
        name="ffn",
    )(x, shift, scale, gate, w1, b1, w2, b2, g, b)
    return res if emit else res[0]


def kernel(x_prompt, x_sample, cache_k, cache_v, cache_kidx, page_table, c_prompt, c_sample,
           w_cond, b_cond, w_in, ln_v_g, ln_v_b, w_spatial, b_spatial, w_out,
           ln1_g, ln1_b, w_ff1, b_ff1, w_ff2, b_ff2, ln2_g, ln2_b):
    batch, seq, d = x_prompt.shape
    bd, ts, _ = x_sample.shape
    depth = w_in.shape[0]
    n_pages = page_table.shape[1]
    past = n_pages * PAGE_SIZE
    n_pool = cache_k.shape[1]
    dff = w_ff1.shape[2]
    assert ts == 1 and seq % KC == 0 and d == A_WIDTH + ATT_WIDTH
    assert w_in.shape[2] == MAIN_WIDTH + IDX_DIM + IDX_HEADS
    n = batch * seq
    alpha = (2 * depth) ** 0.25
    n_sel_p = min(TOPK_MAX, seq // 4)
    n_sel_s = min(TOPK_MAX, (past + ts) // 4)
    tm_p = min(1024, seq)
    tm_o = min(512, seq)
    tf = min(1024, dff)
    pt_flat = page_table.reshape(-1).astype(I32)

    xp = x_prompt.reshape(n, d)
    xs = x_sample.reshape(bd, d)
    c_all = jnp.concatenate([c_prompt, c_sample], axis=0)

    outs = {k: [] for k in ("kp", "vp", "kip", "ks", "vs", "kis", "vc")}
    for l in range(depth):
        wm = jnp.swapaxes(w_in[l], 0, 1).astype(BF16)
        wt = jnp.pad(wm[MAIN_WIDTH:], ((0, 2 * IDX_DIM - IDX_DIM - IDX_HEADS), (0, 0)))
        kidx_t = jnp.swapaxes(cache_kidx[l], 1, 2)
        lng = ln_v_g[l].reshape(1, A_WIDTH)
        lnb = ln_v_b[l].reshape(1, A_WIDTH)
        tril = jnp.tril(jnp.ones((CHUNK, CHUNK), dtype=bool))
        ws = jnp.where(tril[None], w_spatial[l], 0.0).astype(BF16)
        bsp = jnp.repeat(jnp.transpose(b_spatial[l]), A_CH, axis=1)
        ws0 = jnp.repeat(w_spatial[l][:, 0, 0], A_CH).reshape(1, A_WIDTH)
        bs0 = jnp.repeat(b_spatial[l][:, 0], A_CH).reshape(1, A_WIDTH)
        b1 = b_ff1[l].reshape(1, dff)
        b2 = b_ff2[l].reshape(1, d)
        g1, be1 = ln1_g[l].reshape(1, d), ln1_b[l].reshape(1, d)
        g2, be2 = ln2_g[l].reshape(1, d), ln2_b[l].reshape(1, d)

        z = _cond(c_all, w_cond[l], b_cond[l])
        mods = [z[:, i * d:(i + 1) * d] for i in range(N_MOD)]
        mp = [m[:batch].reshape(batch, 1, d) for m in mods]
        ms = [m[batch:].reshape(1, bd, d) for m in mods]

        (u, vn, q, k, v, _, _, qi, ki, _, wit) = _proj(
            xs, ms[0], ms[1], wm, wt, lng, lnb, tm=bd, vn_dtype=F32)
        sc, sc_self = _s_scores(pt_flat, qi.reshape(bd, IDX_HEADS, IDX_DIM),
                                jnp.transpose(wit).reshape(bd, IDX_HEADS, 1),
                                ki.reshape(bd, 1, IDX_DIM), kidx_t, n_pages=n_pages)
        sel, sel_self, a_out = _s_select(sc.reshape(bd, past), sc_self.reshape(bd, 128), u, vn, ws0, bs0,
                                         n_sel=n_sel_s)
        sel2 = jnp.repeat(sel, 2, axis=1).reshape(bd, 1, 2 * past)
        b_out = _s_attn(pt_flat, q.reshape(bd, N_HEADS, HEAD_DIM), sel2, sel_self.reshape(bd, 1, 128),
                        k.reshape(bd, N_KV_HEADS, HEAD_DIM), v.reshape(bd, N_KV_HEADS, HEAD_DIM),
                        cache_k[l].reshape(n_pool, 2 * PAGE_SIZE, HEAD_DIM),
                        cache_v[l].reshape(n_pool, 2 * PAGE_SIZE, HEAD_DIM),
                        n_pages=n_pages, past=past)
        mixin = jnp.concatenate([a_out, b_out.reshape(bd, ATT_WIDTH)], axis=1)
        x1, w_out_b = _outln(mixin, xs, ms[2], w_out[l], g1, be1, tm=bd, alpha=alpha)
        xs, w1_b, w2_b = _ffn(x1, ms[3], ms[4], ms[5], w_ff1[l], b1, w_ff2[l], b2, g2, be2, tm=bd, tf=tf, alpha=alpha)
        outs["ks"].append(k.reshape(bd, ts, N_KV_HEADS, HEAD_DIM))
        outs["vs"].append(v.reshape(bd, ts, N_KV_HEADS, HEAD_DIM))
        outs["kis"].append(ki.reshape(bd, ts, IDX_DIM))
        outs["vc"].append(vn.reshape(bd, ts, A_GROUPS, A_CH))

        (u, vn, q, k, v, kb, vt, qi, ki, ki2, wit) = _proj(
            xp, mp[0], mp[1], wm, wt, lng, lnb, tm=tm_p, vn_dtype=BF16)
        mixin = _mix(u, vn, q, qi, wit, kb, vt, ki2, ws, bsp, batch=batch, seq=seq, n_sel=n_sel_p)
        x1 = _outln(mixin, xp, mp[2], w_out_b, g1, be1, tm=tm_o, alpha=alpha)
        xp = _ffn(x1, mp[3], mp[4], mp[5], w1_b, b1, w2_b, b2, g2, be2, tm=tm_p, tf=tf, alpha=alpha)
        outs["kp"].append(k.reshape(batch, seq, N_KV_HEADS, HEAD_DIM))
        outs["vp"].append(v.reshape(batch, seq, N_KV_HEADS, HEAD_DIM))
        outs["kip"].append(ki.reshape(batch, seq, IDX_DIM))

    st = lambda name: jnp.stack(outs[name])
    return (xp.reshape(batch, seq, d), xs.reshape(bd, ts, d),
            st("kp"), st("vp"), st("kip"), st("ks"), st("vs"), st("kis"), st("vc"))
```

```python
import functools

import jax
import jax.numpy as jnp
import numpy as np
from jax import lax
from jax.experimental import pallas as pl
from jax.experimental.pallas import tpu as pltpu

F32 = jnp.float32
BF16 = jnp.bfloat16
I32 = jnp.int32

CHUNK = 128
A_GROUPS = 8
A_CH = 128
A_WIDTH = A_GROUPS * A_CH
HEAD_DIM = 128
N_HEADS = 8
N_KV_HEADS = 2
GQA_GROUP = N_HEADS // N_KV_HEADS
ATT_WIDTH = N_HEADS * HEAD_DIM
KV_WIDTH = N_KV_HEADS * HEAD_DIM
IDX_HEADS = 16
IDX_DIM = 64
IDX_WIDTH = IDX_HEADS * IDX_DIM
TOPK_MAX = 256
Q_BLOCK = 128
PAGE_SIZE = 128
N_MOD = 6
LN_EPS = 1e-5
ATT_SCALE = HEAD_DIM ** -0.5
MAIN_WIDTH = 2 * A_WIDTH + ATT_WIDTH + 2 * KV_WIDTH + IDX_WIDTH
PROJ_TN = 512
FFN_TN = 512
OUTLN_SLAB = 256
ALIBI_SLOPES = tuple(float(2.0 ** (-8.0 * h / N_HEADS)) for h in range(1, N_HEADS + 1))

VMEM_LIMIT_BYTES = 58 * 1024 * 1024
INT_MIN = -(2 ** 31)
NEG_BIG = -1e30
FLT_MAX = float(np.finfo(np.float32).max)
LOG2E = float(np.log2(np.e))
KC = 512
KU = 128
S_SCORES_SLOTS = 4
S_ATTN_SLOTS = 3
LANE_HEADS = 4
DEN_ROWS = 16

NT_DIMS = (((1,), (1,)), ((), ()))


def _dot(a, b):
    return jnp.dot(a, b, preferred_element_type=F32)


def _dot_nt(a, b):
    return lax.dot_general(a, b, NT_DIMS, preferred_element_type=F32)


def _layer_norm(x, g, b):
    mu = jnp.mean(x, axis=-1, keepdims=True)
    xc = x - mu
    var = jnp.mean(xc * xc, axis=-1, keepdims=True)
    return xc * lax.rsqrt(var + LN_EPS) * g + b


def _key_to_float(key):
    bits = jnp.where(key < 0, key ^ jnp.int32(0x7FFFFFFF), key)
    return pltpu.bitcast(bits, F32)


def _params(*sem):
    return pltpu.CompilerParams(dimension_semantics=sem, vmem_limit_bytes=VMEM_LIMIT_BYTES)


def _cond_kernel(c_ref, w_ref, b_ref, o_ref):
    c = c_ref[...]
    a = (c * jax.nn.sigmoid(c)).astype(BF16)
    o_ref[...] = _dot(a, w_ref[...].astype(BF16)) + b_ref[...]


def _cond(c, w, b):
    m, d = c.shape
    n = w.shape[1]
    tn = 1024
    return pl.pallas_call(
        _cond_kernel,
        out_shape=jax.ShapeDtypeStruct((m, n), F32),
        grid=(n // tn,),
        in_specs=[pl.BlockSpec((m, d), lambda j: (0, 0)),
                  pl.BlockSpec((d, tn), lambda j: (0, j)),
                  pl.BlockSpec((1, tn), lambda j: (0, j))],
        out_specs=pl.BlockSpec((m, tn), lambda j: (0, j)),
        compiler_params=_params("arbitrary"),
        name="cond",
    )(c, w, b.reshape(1, n))


_J_V, _J_Q, _J_KV, _J_QI, _J_END = 2, 4, 6, 7, 9


def _proj_kernel(x_ref, shift_ref, scale_ref, wm_ref, wt_ref, lng_ref, lnb_ref,
                 u_ref, vn_ref, q_ref, k_ref, v_ref, kb_ref, vt_ref, qi_ref, ki_ref, ki2_ref, wit_ref,
                 h_scr):
    j = pl.program_id(1)

    @pl.when(j == 0)
    def _():
        h = (x_ref[...] * (1.0 + scale_ref[...]) + shift_ref[...]).astype(BF16)
        h_scr[...] = h
        tail = _dot_nt(h, wt_ref[...])
        ki_ref[...] = tail[:, :IDX_DIM]
        lane = lax.broadcasted_iota(I32, tail.shape, 1)
        ki2_ref[...] = jnp.where(lane < IDX_DIM, tail, pltpu.roll(tail, IDX_DIM, axis=1)).astype(BF16)
        wit_ref[...] = tail.T[IDX_DIM:IDX_DIM + IDX_HEADS, :] * (IDX_HEADS ** -0.5)

    def tile():
        return _dot_nt(h_scr[...], wm_ref[...])

    @pl.when(j < _J_V)
    def _():
        u_ref[...] = tile()

    @pl.when(jnp.logical_and(j >= _J_V, j < _J_Q))
    def _():
        z = tile()
        for g in range(PROJ_TN // A_CH):
            sl = slice(g * A_CH, (g + 1) * A_CH)
            vn_ref[:, sl] = _layer_norm(z[:, sl], lng_ref[:, sl], lnb_ref[:, sl]).astype(vn_ref.dtype)

    @pl.when(jnp.logical_and(j >= _J_Q, j < _J_KV))
    def _():
        q_ref[...] = tile().astype(BF16)

    @pl.when(j == _J_KV)
    def _():
        z = tile()
        for hh in range(N_KV_HEADS):
            k_ref[:, hh, :] = z[:, hh * HEAD_DIM:(hh + 1) * HEAD_DIM]
            v_ref[:, hh, :] = z[:, KV_WIDTH + hh * HEAD_DIM:KV_WIDTH + (hh + 1) * HEAD_DIM]
        kb_ref[...] = z[:, :KV_WIDTH].astype(BF16)
        vt = z[:, KV_WIDTH:].T.astype(BF16)
        for c in range(vt_ref.shape[0]):
            vt_ref[c] = vt[:, c * 128:(c + 1) * 128]

    @pl.when(j >= _J_QI)
    def _():
        qi_ref[...] = tile().astype(BF16)


def _proj(x, shift, scale, wm, wt, lng, lnb, *, tm, vn_dtype):
    n, d = x.shape
    nb, r, _ = shift.shape
    rows_per_mod = n // nb
    assert n % tm == 0 and rows_per_mod % tm == 0 and r in (1, tm)
    tiles_per_mod = rows_per_mod // tm
    tn = PROJ_TN

    def clipj(lo, cnt):
        return lambda i, j: (i, jnp.clip(j - lo, 0, cnt - 1))

    mod_spec = pl.BlockSpec((None, r, d), lambda i, j: (i // tiles_per_mod, 0, 0))
    out_shape = (
        jax.ShapeDtypeStruct((n, A_WIDTH), F32),
        jax.ShapeDtypeStruct((n, A_WIDTH), vn_dtype),
        jax.ShapeDtypeStruct((n, ATT_WIDTH), BF16),
        jax.ShapeDtypeStruct((n, N_KV_HEADS, HEAD_DIM), F32),
        jax.ShapeDtypeStruct((n, N_KV_HEADS, HEAD_DIM), F32),
        jax.ShapeDtypeStruct((n, KV_WIDTH), BF16),
        jax.ShapeDtypeStruct((n // 128, KV_WIDTH, 128), BF16),
        jax.ShapeDtypeStruct((n, IDX_WIDTH), BF16),
        jax.ShapeDtypeStruct((n, IDX_DIM), F32),
        jax.ShapeDtypeStruct((n, 2 * IDX_DIM), BF16),
        jax.ShapeDtypeStruct((IDX_HEADS, n), F32),
    )
    out_specs = (
        pl.BlockSpec((tm, tn), clipj(0, 2)),
        pl.BlockSpec((tm, tn), clipj(_J_V, 2)),
        pl.BlockSpec((tm, tn), clipj(_J_Q, 2)),
        pl.BlockSpec((tm, N_KV_HEADS, HEAD_DIM), lambda i, j: (i, 0, 0)),
        pl.BlockSpec((tm, N_KV_HEADS, HEAD_DIM), lambda i, j: (i, 0, 0)),
        pl.BlockSpec((tm, KV_WIDTH), lambda i, j: (i, 0)),
        pl.BlockSpec((tm // 128, KV_WIDTH, 128), lambda i, j: (i, 0, 0)),
        pl.BlockSpec((tm, tn), clipj(_J_QI, 2)),
        pl.BlockSpec((tm, IDX_DIM), lambda i, j: (i, 0)),
        pl.BlockSpec((tm, 2 * IDX_DIM), lambda i, j: (i, 0)),
        pl.BlockSpec((IDX_HEADS, tm), lambda i, j: (0, i)),
    )
    in_specs = [
        pl.BlockSpec((tm, d), lambda i, j: (i, 0)),
        mod_spec, mod_spec,
        pl.BlockSpec((tn, d), lambda i, j: (j, 0)),
        pl.BlockSpec((2 * IDX_DIM, d), lambda i, j: (0, 0)),
        pl.BlockSpec((1, tn), lambda i, j: (0, jnp.clip(j - _J_V, 0, 1))),
        pl.BlockSpec((1, tn), lambda i, j: (0, jnp.clip(j - _J_V, 0, 1))),
    ]
    return pl.pallas_call(
        _proj_kernel,
        out_shape=out_shape,
        grid=(n // tm, _J_END),
        in_specs=in_specs,
        out_specs=out_specs,
        scratch_shapes=[pltpu.VMEM((tm, d), BF16)],
        compiler_params=_params("arbitrary", "arbitrary"),
        name="proj",
    )(x, shift, scale, wm, wt, lng, lnb)


def _mix_kernel(u_ref, vn_ref, q_ref, qi_ref, wit_ref, k_ref, vt_ref, ki2_ref, ws_ref, bsp_ref,
                o_ref, sc_scr, msk_scr, qim_scr, ab_scr, acc_scr, thr_scr, *, n_sel):
    jq = pl.program_id(1)
    n_unit = jq // (KU // 128) + 1
    max_units = sc_scr.shape[0] // KU

    def chunks_of(units):
        full = [(c * KC, KC) for c in range(units * KU // KC)]
        return full + ([(len(full) * KC, units * KU - len(full) * KC)] if units * KU % KC else [])

    def per_unit_count(stage):
        for units in range(1, max_units + 1):
            pl.when(n_unit == units)(functools.partial(stage, chunks_of(units)))

    row = lax.broadcasted_iota(I32, (128, 128), 0)
    col = lax.broadcasted_iota(I32, (128, 128), 1)

    @pl.when(jnp.logical_and(pl.program_id(0) == 0, jq == 0))
    def _():
        d0 = (lax.broadcasted_iota(I32, (KC, 128), 1) - lax.broadcasted_iota(I32, (KC, 128), 0)).astype(F32)
        for h in range(N_HEADS):
            lanes = slice((h % LANE_HEADS) * 128, (h % LANE_HEADS + 1) * 128)
            ab_scr[h // LANE_HEADS, :, lanes] = (ALIBI_SLOPES[h] * LOG2E) * d0

    for p in range(IDX_HEADS // 2):
        pair = qi_ref[:, p * 128:(p + 1) * 128]
        zero = jnp.zeros_like(pair)
        qim_scr[p, 0:128, :] = jnp.where(col < IDX_DIM, pair, zero)
        qim_scr[p, 128:256, :] = jnp.where(col >= IDX_DIM, pair, zero)
    wis = wit_ref[...] * (IDX_DIM ** -0.5)
    key_minus_query = (lax.broadcasted_iota(I32, (KC, 128), 0) - lax.broadcasted_iota(I32, (KC, 128), 1))

    def index_chunks(chunks):
        for start, size in chunks:
            kk = ki2_ref[start:start + size, :]
            acc = jnp.zeros((size, 128), F32)
            for p in range(IDX_HEADS // 2):
                s = _dot_nt(kk, qim_scr[p])
                acc = (acc + jnp.maximum(s[:, :128], 0.0) * wis[2 * p:2 * p + 1, :]
                       + jnp.maximum(s[:, 128:], 0.0) * wis[2 * p + 1:2 * p + 2, :])
            inadmissible = key_minus_query[:size, :] > (jq * 128 - start)
            sc_scr[start:start + size, :] = jnp.where(inadmissible, -jnp.inf, acc)

    per_unit_count(index_chunks)

    def count(pred):
        def body(ku, c):
            x = sc_scr[pl.ds(pl.multiple_of(ku * KU, KU), KU), :]
            m = jnp.where(pred(x), 1, 0).astype(I32)
            return c + jnp.sum(m.reshape(KU // 8, 8, 128), axis=0)
        c = lax.fori_loop(0, n_unit, body, jnp.zeros((8, 128), I32))
        return jnp.sum(c, axis=0, keepdims=True)

    def search(chunks):
        def bit_body(i, t):
            cand = t ^ lax.shift_left(jnp.int32(1), 31 - i)
            cand_f = _key_to_float(cand)
            c = jnp.zeros((8, 128), I32)
            for start, size in chunks:
                m = jnp.where(sc_scr[start:start + size, :] >= cand_f, 1, 0).astype(I32)
                c = c + jnp.sum(m.reshape(size // 8, 8, 128), axis=0)
            return jnp.where(jnp.sum(c, axis=0, keepdims=True) >= n_sel, cand, t)

        t = lax.fori_loop(0, 32, bit_body, jnp.full((1, 128), INT_MIN, I32))
        thr_scr[...] = jnp.broadcast_to(t, thr_scr.shape)

    per_unit_count(search)
    thr_key = thr_scr[0:1, :]
    thr = jnp.where(thr_key == jnp.int32(INT_MIN), -FLT_MAX, _key_to_float(thr_key))
    c_ge = count(lambda x: x >= thr)
    c_gt = count(lambda x: x > thr)
    has_tie_overflow = jnp.max(c_ge) > n_sel

    @pl.when(jnp.logical_not(has_tie_overflow))
    def _():
        def body(ku, carry):
            sl = pl.ds(pl.multiple_of(ku * KU, KU), KU)
            msk_scr[sl, :] = jnp.where(sc_scr[sl, :] >= thr, 0.0, -jnp.inf).astype(F32)
            return carry
        lax.fori_loop(0, n_unit, body, 0)

    @pl.when(has_tie_overflow)
    def _():
        need = (n_sel - c_gt).astype(F32)
        lstrict = jnp.where(col < row, 1.0, 0.0).astype(BF16)

        def body(kt, before):
            sl = pl.ds(pl.multiple_of(kt * 128, 128), 128)
            x = sc_scr[sl, :]
            eq = x == thr
            eqf = jnp.where(eq, 1.0, 0.0).astype(F32)
            rank = before + _dot(lstrict, eqf.astype(BF16))
            sel = jnp.logical_or(x > thr, jnp.logical_and(eq, rank < need))
            msk_scr[sl, :] = jnp.where(sel, 0.0, -jnp.inf).astype(F32)
            return before + jnp.sum(eqf, axis=0, keepdims=True)
        lax.fori_loop(0, n_unit * (KU // 128), body, jnp.zeros((1, 128), F32))

    gw = LANE_HEADS * 128
    n_lg = N_HEADS // LANE_HEADS
    lane_head = lax.broadcasted_iota(I32, (1, gw), 1) // 128
    q_grp, slope_vec = [], []
    for g in range(n_lg):
        heads = range(g * LANE_HEADS, (g + 1) * LANE_HEADS)
        q_grp.append(jnp.concatenate([q_ref[:, h * HEAD_DIM:(h + 1) * HEAD_DIM] for h in heads], axis=0))
        sv = jnp.zeros((1, gw), F32)
        for i, h in enumerate(heads):
            sv = jnp.where(lane_head == i, ALIBI_SLOPES[h] * LOG2E, sv)
        slope_vec.append(sv)
        acc_scr[g] = jnp.zeros((HEAD_DIM + DEN_ROWS, gw), F32)

    def att_chunk(start, size, carry):
        off = (jq * 128 - start).astype(F32)
        mk = msk_scr[start:start + size, :]
        mkw = jnp.concatenate([mk] * LANE_HEADS, axis=1)
        kv_of = [g * LANE_HEADS // GQA_GROUP for g in range(n_lg)]
        raw = [_dot_nt(k_ref[start:start + size, kv_of[g] * HEAD_DIM:(kv_of[g] + 1) * HEAD_DIM], q_grp[g])
               for g in range(n_lg)]
        new = []
        for g in range(n_lg):
            kv = kv_of[g]
            m = carry[g]
            x = raw[g] * (ATT_SCALE * LOG2E) - ab_scr[g, :size, :] + mkw
            cvec = slope_vec[g] * off
            m_new = jnp.maximum(m, jnp.max(x, axis=0, keepdims=True) - cvec)
            alpha = jnp.exp2(m - m_new)
            p = jnp.exp2(x - (m_new + cvec)).astype(BF16)
            vt_c = jnp.concatenate(
                [jnp.concatenate([vt_ref[start // 128 + t, kv * HEAD_DIM:(kv + 1) * HEAD_DIM, :]
                                  for t in range(size // 128)], axis=1),
                 jnp.ones((DEN_ROWS, size), BF16)], axis=0)
            acc_scr[g] = alpha * acc_scr[g] + _dot(vt_c, p)
            new.append(m_new)
        return tuple(new)

    def attend(chunks):
        carry = (jnp.full((1, gw), NEG_BIG, F32),) * n_lg
        for start, size in chunks:
            carry = att_chunk(start, size, carry)

    per_unit_count(attend)
    for g in range(n_lg):
        out_t = acc_scr[g, :HEAD_DIM, :] / acc_scr[g, HEAD_DIM:HEAD_DIM + 1, :]
        for i in range(LANE_HEADS):
            lo = A_WIDTH + (g * LANE_HEADS + i) * HEAD_DIM
            o_ref[:, lo:lo + HEAD_DIM] = out_t[:, i * 128:(i + 1) * 128].T.astype(o_ref.dtype)

    for g in range(A_GROUPS):
        sl = slice(g * A_CH, (g + 1) * A_CH)
        sp = _dot(ws_ref[g], vn_ref[:, sl]) + bsp_ref[:, sl]
        o_ref[:, sl] = (u_ref[:, sl] * sp).astype(o_ref.dtype)


def _mix(u, vn, q, qi, wit, kb, vt, ki2, ws, bsp, *, batch, seq, n_sel):
    n = batch * seq
    nblk = seq // Q_BLOCK
    ntile = seq // 128
    row_spec = lambda w: pl.BlockSpec((Q_BLOCK, w), lambda b, j: (b * nblk + j, 0))
    in_specs = [
        row_spec(A_WIDTH), row_spec(A_WIDTH), row_spec(ATT_WIDTH), row_spec(IDX_WIDTH),
        pl.BlockSpec((IDX_HEADS, Q_BLOCK), lambda b, j: (0, b * nblk + j)),
        pl.BlockSpec((seq, KV_WIDTH), lambda b, j: (b, 0)),
        pl.BlockSpec((ntile, KV_WIDTH, 128), lambda b, j: (b, 0, 0)),
        pl.BlockSpec((seq, 2 * IDX_DIM), lambda b, j: (b, 0)),
        pl.BlockSpec((A_GROUPS, CHUNK, CHUNK), lambda b, j: (0, 0, 0)),
        pl.BlockSpec((CHUNK, A_WIDTH), lambda b, j: (0, 0)),
    ]
    return pl.pallas_call(
        functools.partial(_mix_kernel, n_sel=n_sel),
        out_shape=jax.ShapeDtypeStruct((n, A_WIDTH + ATT_WIDTH), BF16),
        grid=(batch, nblk),
        in_specs=in_specs,
        out_specs=pl.BlockSpec((Q_BLOCK, A_WIDTH + ATT_WIDTH), lambda b, j: (b * nblk + j, 0)),
        scratch_shapes=[pltpu.VMEM((seq, 128), F32),
                        pltpu.VMEM((seq, 128), F32),
                        pltpu.VMEM((IDX_HEADS // 2, 256, 128), BF16),
                        pltpu.VMEM((N_HEADS // LANE_HEADS, KC, LANE_HEADS * 128), F32),
                        pltpu.VMEM((N_HEADS // LANE_HEADS, HEAD_DIM + DEN_ROWS, LANE_HEADS * 128), F32),
                        pltpu.VMEM((8, 128), I32)],
        compiler_params=_params("arbitrary", "arbitrary"),
        name="mix",
    )(u, vn, q, qi, wit, kb, vt, ki2, ws, bsp)


def _page_copies(pt_ref, row, n_pages, slot, streams, *, for_wait):
    cps = []
    for p in range(n_pages):
        page = 0 if for_wait else pt_ref[row * n_pages + p]
        for hbm, buf, sem, place in streams:
            cps.append(pltpu.make_async_copy(hbm.at[page], buf.at[(slot,) + place(p)], sem.at[slot]))
    return cps


def _gather_pages(pt_ref, n_pages, streams, n_slots):
    b = pl.program_id(0)
    ahead = n_slots - 1

    @pl.when(b == 0)
    def _():
        for r in range(ahead):
            for c in _page_copies(pt_ref, r, n_pages, r, streams, for_wait=False):
                c.start()

    @pl.when(b + ahead < pl.num_programs(0))
    def _():
        for c in _page_copies(pt_ref, b + ahead, n_pages, (b + ahead) % n_slots, streams, for_wait=False):
            c.start()

    slot = b % n_slots
    for c in _page_copies(pt_ref, b, n_pages, slot, streams, for_wait=True):
        c.wait()
    return slot


def _s_scores_kernel(pt_ref, qi_ref, wi_ref, kin_ref, kidx_hbm, sc_ref, self_ref, kbuf, sem, *, n_pages):
    place = lambda p: (slice(None), pl.ds(p * PAGE_SIZE, PAGE_SIZE))
    slot = _gather_pages(pt_ref, n_pages, [(kidx_hbm, kbuf, sem, place)], S_SCORES_SLOTS)
    qi = qi_ref[...]
    w = wi_ref[...] * (IDX_DIM ** -0.5)
    kcat = kbuf[slot].astype(BF16)
    s = _dot(qi, kcat)
    sc_ref[...] = jnp.sum(jnp.maximum(s, 0.0) * w, axis=0, keepdims=True)
    kin = kin_ref[...].astype(BF16).astype(F32)
    s_self = jnp.sum(qi.astype(F32) * kin, axis=1, keepdims=True)
    v_self = jnp.sum(jnp.maximum(s_self, 0.0) * w, axis=0, keepdims=True)
    self_ref[...] = jnp.broadcast_to(v_self, self_ref.shape)


def _s_scores(page_table_flat, qi3, wi_col, ki_new3, cache_kidx_l, *, n_pages):
    bd = qi3.shape[0]
    assert bd >= S_SCORES_SLOTS
    past = n_pages * PAGE_SIZE
    grid_spec = pltpu.PrefetchScalarGridSpec(
        num_scalar_prefetch=1,
        grid=(bd,),
        in_specs=[pl.BlockSpec((None, IDX_HEADS, IDX_DIM), lambda b, pt: (b, 0, 0)),
                  pl.BlockSpec((None, IDX_HEADS, 1), lambda b, pt: (b, 0, 0)),
                  pl.BlockSpec((None, 1, IDX_DIM), lambda b, pt: (b, 0, 0)),
                  pl.BlockSpec(memory_space=pl.ANY)],
        out_specs=(pl.BlockSpec((None, 1, past), lambda b, pt: (b, 0, 0)),
                   pl.BlockSpec((None, 1, 128), lambda b, pt: (b, 0, 0))),
        scratch_shapes=[pltpu.VMEM((S_SCORES_SLOTS, IDX_DIM, past), F32),
                        pltpu.SemaphoreType.DMA((S_SCORES_SLOTS,))],
    )
    return pl.pallas_call(
        functools.partial(_s_scores_kernel, n_pages=n_pages),
        out_shape=(jax.ShapeDtypeStruct((bd, 1, past), F32),
                   jax.ShapeDtypeStruct((bd, 1, 128), F32)),
        grid_spec=grid_spec,
        compiler_params=_params("arbitrary"),
        name="s_scores",
    )(page_table_flat, qi3, wi_col, ki_new3, cache_kidx_l)


def _s_select_kernel(sc_ref, self_ref, u_ref, vn_ref, ws0_ref, bs0_ref,
                     sel_ref, selself_ref, a_ref, *, n_sel):
    keys = sc_ref[...]
    kself = self_ref[:, 0:1]
    bd, past = keys.shape

    def count_ge(cand):
        c = jnp.sum(jnp.where(keys >= cand, 1, 0).astype(I32), axis=1, keepdims=True)
        return c + jnp.where(kself >= cand, 1, 0).astype(I32)

    def bit_body(i, t):
        cand = t ^ lax.shift_left(jnp.int32(1), 31 - i)
        return jnp.where(count_ge(_key_to_float(cand)) >= n_sel, cand, t)

    thr = _key_to_float(lax.fori_loop(0, 32, bit_body, jnp.full((bd, 1), INT_MIN, I32)))
    c_gt = (jnp.sum(jnp.where(keys > thr, 1, 0).astype(I32), axis=1, keepdims=True)
            + jnp.where(kself > thr, 1, 0).astype(I32))
    need = (n_sel - c_gt).astype(F32)
    r = lax.broadcasted_iota(I32, (128, 128), 0)
    c = lax.broadcasted_iota(I32, (128, 128), 1)
    ustrict = jnp.where(r < c, 1.0, 0.0).astype(BF16)
    before = jnp.zeros((bd, 1), F32)
    for t in range(past // 128):
        kt = keys[:, t * 128:(t + 1) * 128]
        eq = kt == thr
        eqf = jnp.where(eq, 1.0, 0.0).astype(F32)
        rank = before + _dot(eqf.astype(BF16), ustrict)
        sel = jnp.logical_or(kt > thr, jnp.logical_and(eq, rank < need))
        sel_ref[:, t * 128:(t + 1) * 128] = jnp.where(sel, 1.0, 0.0).astype(F32)
        before = before + jnp.sum(eqf, axis=1, keepdims=True)
    sel_self = jnp.logical_or(kself > thr, jnp.logical_and(kself == thr, before < need))
    selself_ref[...] = jnp.broadcast_to(jnp.where(sel_self, 1.0, 0.0).astype(F32), selself_ref.shape)
    sp = ws0_ref[...] * vn_ref[...] + bs0_ref[...]
    a_ref[...] = (u_ref[...] * sp).astype(a_ref.dtype)


def _s_select(sc, sc_self, u, vn, ws0, bs0, *, n_sel):
    bd, past = sc.shape
    full = lambda a: pl.BlockSpec(a.shape, lambda i: (0,) * a.ndim)
    args = (sc, sc_self, u, vn, ws0, bs0)
    return pl.pallas_call(
        functools.partial(_s_select_kernel, n_sel=n_sel),
        out_shape=(jax.ShapeDtypeStruct((bd, past), F32),
                   jax.ShapeDtypeStruct((bd, 128), F32),
                   jax.ShapeDtypeStruct((bd, A_WIDTH), BF16)),
        grid=(1,),
        in_specs=[full(a) for a in args],
        out_specs=(pl.BlockSpec((bd, past), lambda i: (0, 0)),
                   pl.BlockSpec((bd, 128), lambda i: (0, 0)),
                   pl.BlockSpec((bd, A_WIDTH), lambda i: (0, 0))),
        compiler_params=_params("arbitrary"),
        name="s_select",
    )(*args)


def _s_attn_kernel(pt_ref, q_ref, sel_ref, selself_ref, knew_ref, vnew_ref, ck_hbm, cv_hbm, o_ref,
                   kbuf, vbuf, ksem, vsem, *, n_pages, past):
    place = lambda p: (pl.ds(p * 2 * PAGE_SIZE, 2 * PAGE_SIZE), slice(None))
    slot = _gather_pages(pt_ref, n_pages, [(ck_hbm, kbuf, ksem, place), (cv_hbm, vbuf, vsem, place)], S_ATTN_SLOTS)
    q = q_ref[...]
    hrow = lax.broadcasted_iota(I32, (N_HEADS, 2 * past), 0)
    ccol = lax.broadcasted_iota(I32, (N_HEADS, 2 * past), 1)
    own_kv = (ccol & 1) == (hrow // GQA_GROUP)
    hcol = lax.broadcasted_iota(I32, (N_HEADS, 1), 0)
    slope = jnp.zeros((N_HEADS, 1), F32)
    for h in range(N_HEADS):
        slope = jnp.where(hcol == h, ALIBI_SLOPES[h], slope)

    lg = _dot_nt(q, kbuf[slot].astype(BF16)) * ATT_SCALE
    lg = lg - slope * (past - (ccol >> 1)).astype(F32)
    lg = jnp.where(jnp.logical_and(own_kv, sel_ref[...] > 0.5), lg, -jnp.inf)
    first_group = lax.broadcasted_iota(I32, (N_HEADS, HEAD_DIM), 0) < GQA_GROUP
    knew = knew_ref[...].astype(BF16).astype(F32)
    vnew = vnew_ref[...].astype(BF16).astype(F32)
    knew8 = jnp.where(first_group, knew[0:1, :], knew[1:2, :])
    vnew8 = jnp.where(first_group, vnew[0:1, :], vnew[1:2, :])
    lg_self = jnp.sum(q.astype(F32) * knew8, axis=1, keepdims=True) * ATT_SCALE
    lg_self = jnp.where(selself_ref[:, 0:1] > 0.5, lg_self, -jnp.inf)

    m = jnp.maximum(jnp.maximum(lg_self, NEG_BIG), jnp.max(lg, axis=1, keepdims=True))
    p_self = jnp.exp(lg_self - m)
    pp = jnp.exp(lg - m)
    l = p_self + jnp.sum(pp, axis=1, keepdims=True)
    acc = p_self.astype(BF16).astype(F32) * vnew8 + _dot(pp.astype(BF16), vbuf[slot].astype(BF16))
    o_ref[...] = (acc / l).astype(o_ref.dtype)


def _s_attn(page_table_flat, q3, sel2, sel_self3, k_new3, v_new3, cache_k2, cache_v2, *, n_pages, past):
    bd = q3.shape[0]
    assert bd >= S_ATTN_SLOTS
    per_b = lambda s1, s2: pl.BlockSpec((None, s1, s2), lambda b, pt: (b, 0, 0))
    grid_spec = pltpu.PrefetchScalarGridSpec(
        num_scalar_prefetch=1,
        grid=(bd,),
        in_specs=[per_b(N_HEADS, HEAD_DIM), per_b(1, 2 * past), per_b(1, 128),
                  per_b(N_KV_HEADS, HEAD_DIM), per_b(N_KV_HEADS, HEAD_DIM),
                  pl.BlockSpec(memory_space=pl.ANY), pl.BlockSpec(memory_space=pl.ANY)],
        out_specs=per_b(N_HEADS, HEAD_DIM),
        scratch_shapes=[pltpu.VMEM((S_ATTN_SLOTS, 2 * past, HEAD_DIM), F32),
                        pltpu.VMEM((S_ATTN_SLOTS, 2 * past, HEAD_DIM), F32),
                        pltpu.SemaphoreType.DMA((S_ATTN_SLOTS,)), pltpu.SemaphoreType.DMA((S_ATTN_SLOTS,))],
    )
    return pl.pallas_call(
        functools.partial(_s_attn_kernel, n_pages=n_pages, past=past),
        out_shape=jax.ShapeDtypeStruct((bd, N_HEADS, HEAD_DIM), BF16),
        grid_spec=grid_spec,
        compiler_params=_params("arbitrary"),
        name="s_attn",
    )(page_table_flat, q3, sel2, sel_self3, k_new3, v_new3, cache_k2, cache_v2)


def _outln_kernel(mix_ref, x_ref, gate_ref, w_ref, g_ref, b_ref, o_ref, *wb_ref, alpha):
    w = w_ref[...]
    if wb_ref:
        w = w.astype(BF16)
        wb_ref[0][...] = w
    tm = o_ref.shape[0]
    slab = OUTLN_SLAB if tm % OUTLN_SLAB == 0 else tm
    gate = gate_ref[...]
    for lo in range(0, tm, slab):
        rows = slice(lo, lo + slab)
        mix = _dot(mix_ref[rows, :], w)
        y = alpha * x_ref[rows, :] + (1.0 + (gate if gate.shape[0] == 1 else gate[rows, :])) * mix
        o_ref[rows, :] = _layer_norm(y, g_ref[...], b_ref[...])


def _outln(mixin, x, gate, w, g, b, *, tm, alpha):
    n, d = x.shape
    nb, r, _ = gate.shape
    tiles_per_mod = (n // nb) // tm
    kin = mixin.shape[1]
    emit = w.dtype != BF16
    assert not emit or n == tm
    out_shape = [jax.ShapeDtypeStruct((n, d), F32)]
    out_specs = [pl.BlockSpec((tm, d), lambda i: (i, 0))]
    if emit:
        out_shape.append(jax.ShapeDtypeStruct((kin, d), BF16))
        out_specs.append(pl.BlockSpec((kin, d), lambda i: (0, 0)))
    res = pl.pallas_call(
        functools.partial(_outln_kernel, alpha=alpha),
        out_shape=out_shape,
        grid=(n // tm,),
        in_specs=[pl.BlockSpec((tm, kin), lambda i: (i, 0)),
                  pl.BlockSpec((tm, d), lambda i: (i, 0)),
                  pl.BlockSpec((None, r, d), lambda i: (i // tiles_per_mod, 0, 0)),
                  pl.BlockSpec((kin, d), lambda i: (0, 0), pipeline_mode=pl.Buffered(1)),
                  pl.BlockSpec((1, d), lambda i: (0, 0)),
                  pl.BlockSpec((1, d), lambda i: (0, 0))],
        out_specs=out_specs,
        compiler_params=_params("arbitrary"),
        name="outln",
    )(mixin, x, gate, w, g, b)
    return res if emit else res[0]


def _ffn_kernel(x_ref, shift_ref, scale_ref, gate_ref, w1_ref, b1_ref, w2_ref, b2_ref, g_ref, b_ref,
                o_ref, *rest, alpha):
    *wb_refs, h_scr = rest
    f = pl.program_id(1)

    @pl.when(f == 0)
    def _():
        h_scr[...] = (x_ref[...] * (1.0 + scale_ref[...]) + shift_ref[...]).astype(BF16)
        o_ref[...] = jnp.zeros_like(o_ref)

    if wb_refs:
        wb_refs[0][...] = w1_ref[...].astype(BF16)
        wb_refs[1][...] = w2_ref[...].astype(BF16)
        w1_ref, w2_ref = wb_refs
    a = jnp.maximum(_dot(h_scr[...], w1_ref[...]) + b1_ref[...], 0.0)
    a2 = (a * a).astype(BF16)
    for c in range(o_ref.shape[1] // FFN_TN):
        cs = slice(c * FFN_TN, (c + 1) * FFN_TN)
        o_ref[:, cs] += _dot(a2, w2_ref[:, cs])

    @pl.when(f == pl.num_programs(1) - 1)
    def _():
        y = alpha * x_ref[...] + (1.0 + gate_ref[...]) * (o_ref[...] + b2_ref[...])
        o_ref[...] = _layer_norm(y, g_ref[...], b_ref[...])


def _ffn(x, shift, scale, gate, w1, b1, w2, b2, g, b, *, tm, tf, alpha):
    n, d = x.shape
    nb, r, _ = gate.shape
    dff = w1.shape[1]
    tiles_per_mod = (n // nb) // tm
    mod_spec = pl.BlockSpec((None, r, d), lambda i, f: (i // tiles_per_mod, 0, 0))
    emit = w1.dtype != BF16
    assert not emit or n == tm
    out_shape = [jax.ShapeDtypeStruct((n, d), F32)]
    out_specs = [pl.BlockSpec((tm, d), lambda i, f: (i, 0))]
    if emit:
        out_shape += [jax.ShapeDtypeStruct((d, dff), BF16), jax.ShapeDtypeStruct((dff, d), BF16)]
        out_specs += [pl.BlockSpec((d, tf), lambda i, f: (0, f)), pl.BlockSpec((tf, d), lambda i, f: (f, 0))]
    res = pl.pallas_call(
        functools.partial(_ffn_kernel, alpha=alpha),
        out_shape=out_shape,
        grid=(n // tm, dff // tf),
        in_specs=[pl.BlockSpec((tm, d), lambda i, f: (i, 0), pipeline_mode=pl.Buffered(1)),
                  mod_spec, mod_spec, mod_spec,
                  pl.BlockSpec((d, tf), lambda i, f: (0, f)),
                  pl.BlockSpec((1, tf), lambda i, f: (0, f)),
                  pl.BlockSpec((tf, d), lambda i, f: (f, 0)),
                  pl.BlockSpec((1, d), lambda i, f: (0, 0)),
                  pl.BlockSpec((1, d), lambda i, f: (0, 0)),
                  pl.BlockSpec((1, d), lambda i, f: (0, 0))],
        out_specs=out_specs,
        scratch_shapes=[pltpu.VMEM((tm, d), BF16)],
        compiler_params=_params("arbitrary", "arbitrary"),
        name="ffn",
    )(x, shift, scale, gate, w1, b1, w2, b2, g, b)
    return res if emit else res[0]


def kernel(x_prompt, x_sample, cache_k, cache_v, cache_kidx, page_table, c_prompt, c_sample,
           w_cond, b_cond, w_in, ln_v_g, ln_v_b, w_spatial, b_spatial, w_out,
           ln1_g, ln1_b, w_ff1, b_ff1, w_ff2, b_ff2, ln2_g, ln2_b):
    batch, seq, d = x_prompt.shape
    bd, ts, _ = x_sample.shape
    depth = w_in.shape[0]
    n_pages = page_table.shape[1]
    past = n_pages * PAGE_SIZE
    n_pool = cache_k.shape[1]
    dff = w_ff1.shape[2]
    assert ts == 1 and seq % KC == 0 and d == A_WIDTH + ATT_WIDTH
    assert w_in.shape[2] == MAIN_WIDTH + IDX_DIM + IDX_HEADS
    n = batch * seq
    alpha = (2 * depth) ** 0.25
    n_sel_p = min(TOPK_MAX, seq // 4)
    n_sel_s = min(TOPK_MAX, (past + ts) // 4)
    tm_p = min(1024, seq)
    tm_o = min(512, seq)
    tf = min(1024, dff)
    pt_flat = page_table.reshape(-1).astype(I32)

    xp = x_prompt.reshape(n, d)
    xs = x_sample.reshape(bd, d)
    c_all = jnp.concatenate([c_prompt, c_sample], axis=0)

    outs = {k: [] for k in ("kp", "vp", "kip", "ks", "vs", "kis", "vc")}
    for l in range(depth):
        wm = jnp.swapaxes(w_in[l], 0, 1).astype(BF16)
        wt = jnp.pad(wm[MAIN_WIDTH:], ((0, 2 * IDX_DIM - IDX_DIM - IDX_HEADS), (0, 0)))
        kidx_t = jnp.swapaxes(cache_kidx[l], 1, 2)
        lng = ln_v_g[l].reshape(1, A_WIDTH)
        lnb = ln_v_b[l].reshape(1, A_WIDTH)
        tril = jnp.tril(jnp.ones((CHUNK, CHUNK), dtype=bool))
        ws = jnp.where(tril[None], w_spatial[l], 0.0).astype(BF16)
        bsp = jnp.repeat(jnp.transpose(b_spatial[l]), A_CH, axis=1)
        ws0 = jnp.repeat(w_spatial[l][:, 0, 0], A_CH).reshape(1, A_WIDTH)
        bs0 = jnp.repeat(b_spatial[l][:, 0], A_CH).reshape(1, A_WIDTH)
        b1 = b_ff1[l].reshape(1, dff)
        b2 = b_ff2[l].reshape(1, d)
        g1, be1 = ln1_g[l].reshape(1, d), ln1_b[l].reshape(1, d)
        g2, be2 = ln2_g[l].reshape(1, d), ln2_b[l].reshape(1, d)

        z = _cond(c_all, w_cond[l], b_cond[l])
        mods = [z[:, i * d:(i + 1) * d] for i in range(N_MOD)]
        mp = [m[:batch].reshape(batch, 1, d) for m in mods]
        ms = [m[batch:].reshape(1, bd, d) for m in mods]

        (u, vn, q, k, v, _, _, qi, ki, _, wit) = _proj(
            xs, ms[0], ms[1], wm, wt, lng, lnb, tm=bd, vn_dtype=F32)
        sc, sc_self = _s_scores(pt_flat, qi.reshape(bd, IDX_HEADS, IDX_DIM),
                                jnp.transpose(wit).reshape(bd, IDX_HEADS, 1),
                                ki.reshape(bd, 1, IDX_DIM), kidx_t, n_pages=n_pages)
        sel, sel_self, a_out = _s_select(sc.reshape(bd, past), sc_self.reshape(bd, 128), u, vn, ws0, bs0,
                                         n_sel=n_sel_s)
        sel2 = jnp.repeat(sel, 2, axis=1).reshape(bd, 1, 2 * past)
        b_out = _s_attn(pt_flat, q.reshape(bd, N_HEADS, HEAD_DIM), sel2, sel_self.reshape(bd, 1, 128),
                        k.reshape(bd, N_KV_HEADS, HEAD_DIM), v.reshape(bd, N_KV_HEADS, HEAD_DIM),
                        cache_k[l].reshape(n_pool, 2 * PAGE_SIZE, HEAD_DIM),
                        cache_v[l].reshape(n_pool, 2 * PAGE_SIZE, HEAD_DIM),
                        n_pages=n_pages, past=past)
        mixin = jnp.concatenate([a_out, b_out.reshape(bd, ATT_WIDTH)], axis=1)
        x1, w_out_b = _outln(mixin, xs, ms[2], w_out[l], g1, be1, tm=bd, alpha=alpha)
        xs, w1_b, w2_b = _ffn(x1, ms[3], ms[4], ms[5], w_ff1[l], b1, w_ff2[l], b2, g2, be2, tm=bd, tf=tf, alpha=alpha)
        outs["ks"].append(k.reshape(bd, ts, N_KV_HEADS, HEAD_DIM))
        outs["vs"].append(v.reshape(bd, ts, N_KV_HEADS, HEAD_DIM))
        outs["kis"].append(ki.reshape(bd, ts, IDX_DIM))
        outs["vc"].append(vn.reshape(bd, ts, A_GROUPS, A_CH))

        (u, vn, q, k, v, kb, vt, qi, ki, ki2, wit) = _proj(
            xp, mp[0], mp[1], wm, wt, lng, lnb, tm=tm_p, vn_dtype=BF16)
        mixin = _mix(u, vn, q, qi, wit, kb, vt, ki2, ws, bsp, batch=batch, seq=seq, n_sel=n_sel_p)
        x1 = _outln(mixin, xp, mp[2], w_out_b, g1, be1, tm=tm_o, alpha=alpha)
        xp = _ffn(x1, mp[3], mp[4], mp[5], w1_b, b1, w2_b, b2, g2, be2, tm=tm_p, tf=tf, alpha=alpha)
        outs["kp"].append(k.reshape(batch, seq, N_KV_HEADS, HEAD_DIM))
        outs["vp"].append(v.reshape(batch, seq, N_KV_HEADS, HEAD_DIM))
        outs["kip"].append(ki.reshape(batch, seq, IDX_DIM))

    st = lambda name: jnp.stack(outs[name])
    return (xp.reshape(batch, seq, d), xs.reshape(bd, ts, d),
            st("kp"), st("vp"), st("kip"), st("ks"), st("vs"), st("kis"), st("vc"))
```
